```python
import math
import jax, jax.numpy as jnp
from jax import lax
import numpy as np

D_MODEL = 1024
BATCH = 16
SEQ = 2048
DEPTH = 2

GRID_W = 64
CTX_LEN = 256
D_MIX = D_MODEL
N_GROUPS = 4
GROUP_W = D_MIX // N_GROUPS
CONV_W = 4
LRU_HEADS = 4
LRU_HD = GROUP_W // LRU_HEADS
LRU_C = 8.0
HGRN_HEADS = 4
HGRN_HD = GROUP_W // HGRN_HEADS
HGRN_CHUNK = 64
SSD_HEADS = 4
SSD_HD = GROUP_W // SSD_HEADS
SSD_BC_GROUPS = 2
SSD_STATE = 64
SSD_CHUNK = 64
SSD_XBC = GROUP_W + 2 * SSD_BC_GROUPS * SSD_STATE
MLA_HEADS = 4
MLA_Q_RANK = 192
MLA_KV_RANK = 128
MLA_NOPE = 64
MLA_ROPE = 32
MLA_V = GROUP_W // MLA_HEADS
ROPE_BASE = 10000.0
Q_BLOCK = 128
N_EXPERTS = 16
EXPERT_FF = 512
EC_FACTOR = 2
N_MOD = 6
EPS = 1e-6
LRU_COLS = 2 * GROUP_W
HGRN_COLS = 5 * GROUP_W
SSD_COLS = GROUP_W + SSD_XBC + 2 * SSD_HEADS
MLA_COLS = MLA_Q_RANK + MLA_KV_RANK + MLA_ROPE
IN_COLS = LRU_COLS + HGRN_COLS + SSD_COLS + MLA_COLS
SPLITS = [LRU_COLS, LRU_COLS + HGRN_COLS, LRU_COLS + HGRN_COLS + SSD_COLS]

kernel_name = 'hybrid_dit_rglru_hgrn2_ssd_mla_ecmoe'


def rmsnorm(x, w):
    xf = x.astype(jnp.float32)
    y = xf * lax.rsqrt(jnp.mean(xf * xf, axis=-1, keepdims=True) + EPS)
    return (y * w).astype(x.dtype)


def modulate(h, shift, scale):
    return h * (1.0 + scale) + shift


def flip(t, d):
    return t[:, ::-1] if d else t


def to_chunks(t, chunk):
    b, n = t.shape[:2]
    return t.reshape(b, n // chunk, chunk, *t.shape[2:]).swapaxes(0, 1)


def from_chunks(t):
    nc, b, ch = t.shape[:3]
    return t.swapaxes(0, 1).reshape(b, nc * ch, *t.shape[3:])


def dwconv(x, w, b):
    k = w.shape[0]
    n = x.shape[1]
    xp = jnp.pad(x, ((0, 0), (k // 2, k - 1 - k // 2), (0, 0)))
    return sum(xp[:, j:j + n] * w[j] for j in range(k)) + b


def linear_scan(a, b, h0):
    b = b.at[:, 0].add(a[:, 0] * h0)

    def combine(left, right):
        a_l, b_l = left
        a_r, b_r = right
        return a_l * a_r, a_r * b_l + b_r

    _, h = lax.associative_scan(combine, (a, b), axis=1)
    return h


def masked_decay(seg, mask):
    return jnp.where(mask, jnp.exp(jnp.where(mask, seg, 0.0)), 0.0)


def rglru_coeffs(xc, w_r, b_r, w_i, b_i, lam):
    bn, nn, _ = xc.shape
    xh = xc.reshape(bn, nn, LRU_HEADS, LRU_HD)
    r = jax.nn.sigmoid((jnp.einsum('bnhd,hde->bnhe', xh, w_r).reshape(bn, nn, GROUP_W) + b_r).astype(jnp.float32))
    i = jax.nn.sigmoid((jnp.einsum('bnhd,hde->bnhe', xh, w_i).reshape(bn, nn, GROUP_W) + b_i).astype(jnp.float32))
    log_a = -LRU_C * r * jax.nn.softplus(-lam.astype(jnp.float32))
    a = jnp.exp(log_a)
    b = jnp.sqrt(-jnp.expm1(2.0 * log_a)) * (i * xc.astype(jnp.float32))
    return a, b


def rglru_mixer(u, u_ctx, conv_w, conv_b, w_r, b_r, w_i, b_i, lam, need_ctx):
    x_l, gate_l = jnp.split(u, 2, axis=-1)
    x_c, gate_c = jnp.split(u_ctx, 2, axis=-1)
    x_l = dwconv(x_l, conv_w, conv_b)
    x_c = dwconv(x_c, conv_w, conv_b)
    h_l, h_c = [], []
    for d in range(2):
        a, b = rglru_coeffs(flip(x_c, d), w_r[d], b_r[d], w_i[d], b_i[d], lam[d])
        hc = linear_scan(a, b, jnp.zeros((u_ctx.shape[0], GROUP_W), jnp.float32))
        a, b = rglru_coeffs(flip(x_l, d), w_r[d], b_r[d], w_i[d], b_i[d], lam[d])
        hl = linear_scan(a, b, hc[:, -1])
        h_l.append(flip(hl, d))
        h_c.append(flip(hc, d))
    y_l = ((h_l[0] + h_l[1]) * jax.nn.gelu(gate_l.astype(jnp.float32))).astype(u.dtype)
    y_c = ((h_c[0] + h_c[1]) * jax.nn.gelu(gate_c.astype(jnp.float32))).astype(u.dtype) if need_ctx else None
    return y_l, y_c


def gla_chunk_scan(q, k, v, log_f, s0):
    f32 = jnp.float32
    mask = jnp.tril(jnp.ones((HGRN_CHUNK, HGRN_CHUNK), bool))[None, :, :, None, None]

    def step(s, blk):
        qc, kc, vc, gc = blk
        cum = jnp.cumsum(gc, axis=1)
        seg = cum[:, :, None] - cum[:, None]
        decay = masked_decay(seg, mask)
        scores = jnp.einsum('bthk,bshk,btshk->btsh', qc, kc, decay)
        y = jnp.einsum('btsh,bshv->bthv', scores, vc) + jnp.einsum('bthk,bhkv->bthv', qc * jnp.exp(cum), s)
        k_end = kc * jnp.exp(cum[:, -1:] - cum)
        s_new = s * jnp.exp(cum[:, -1])[..., None] + jnp.einsum('bshk,bshv->bhkv', k_end, vc)
        return s_new, y

    blks = tuple(to_chunks(t.astype(f32), HGRN_CHUNK) for t in (q, k, v, log_f))
    s_fin, ys = lax.scan(step, s0.astype(f32), blks)
    return from_chunks(ys), s_fin


def hgrn2_mixer(u, u_ctx, lb, norm_w, need_ctx):
    lb = lb.astype(jnp.float32).reshape(HGRN_HEADS, HGRN_HD)

    def heads(t):
        return t.reshape(t.shape[0], t.shape[1], HGRN_HEADS, HGRN_HD)

    def prep(v):
        q, f_fwd, f_bwd, inp, g = jnp.split(v, 5, axis=-1)
        return heads(jax.nn.silu(q)), (heads(f_fwd), heads(f_bwd)), heads(inp), g

    def gates(f_raw):
        sig = jax.nn.sigmoid(f_raw.astype(jnp.float32))
        f = lb + (1.0 - lb) * sig
        return jnp.log(f), 1.0 - f

    q_l, f_l, v_l, g_l = prep(u)
    q_c, f_c, v_c, g_c = prep(u_ctx)
    o_l, o_c = [], []
    for d in range(2):
        log_f, k = gates(f_c[d])
        s0 = jnp.zeros((u_ctx.shape[0], HGRN_HEADS, HGRN_HD, HGRN_HD), jnp.float32)
        oc, s_c = gla_chunk_scan(flip(q_c, d), flip(k, d), flip(v_c, d), flip(log_f, d), s0)
        log_f, k = gates(f_l[d])
        ol, _ = gla_chunk_scan(flip(q_l, d), flip(k, d), flip(v_l, d), flip(log_f, d), s_c)
        o_l.append(flip(ol, d))
        o_c.append(flip(oc, d))

    def out(o, g):
        o = rmsnorm(o, norm_w.reshape(HGRN_HEADS, HGRN_HD)).reshape(o.shape[0], o.shape[1], GROUP_W)
        return (o * jax.nn.silu(g.astype(jnp.float32))).astype(u.dtype)

    y_l = out(o_l[0] + o_l[1], g_l)
    y_c = out(o_c[0] + o_c[1], g_c) if need_ctx else None
    return y_l, y_c


def ssd_chunk_scan(xs, bm, cm, dt, log_a, s0):
    f32 = jnp.float32
    mask = jnp.tril(jnp.ones((SSD_CHUNK, SSD_CHUNK), bool))[None, :, :, None]

    def step(s, blk):
        xc, bc, cc, dtc, lac = blk
        cum = jnp.cumsum(lac, axis=1)
        seg = cum[:, :, None, :] - cum[:, None, :, :]
        decay = masked_decay(seg, mask)
        scores = jnp.einsum('bthn,bshn->btsh', cc, bc) * decay
        y = jnp.einsum('btsh,bsh,bshp->bthp', scores, dtc, xc)
        y = y + jnp.einsum('bthn,bhpn->bthp', cc, s) * jnp.exp(cum)[..., None]
        w_end = dtc * jnp.exp(cum[:, -1:, :] - cum)
        s_new = s * jnp.exp(cum[:, -1])[:, :, None, None] + jnp.einsum('bshn,bsh,bshp->bhpn', bc, w_end, xc)
        return s_new, y

    blks = tuple(to_chunks(t.astype(f32), SSD_CHUNK) for t in (xs, bm, cm, dt, log_a))
    s_fin, ys = lax.scan(step, s0.astype(f32), blks)
    return from_chunks(ys), s_fin


def ssd_mixer(u, u_ctx, conv_w, conv_b, a_log, dt_bias, d_skip, norm_w, need_ctx):
    rep = SSD_HEADS // SSD_BC_GROUPS

    def prep(v):
        bn, nn = v.shape[:2]
        z, xbc, dt_raw = jnp.split(v, [GROUP_W, GROUP_W + SSD_XBC], axis=-1)
        xbc = jax.nn.silu(dwconv(xbc, conv_w, conv_b))
        xs, bm, cm = jnp.split(xbc, [GROUP_W, GROUP_W + SSD_BC_GROUPS * SSD_STATE], axis=-1)
        xs = xs.reshape(bn, nn, SSD_HEADS, SSD_HD)
        bm = jnp.repeat(bm.reshape(bn, nn, SSD_BC_GROUPS, SSD_STATE), rep, axis=2)
        cm = jnp.repeat(cm.reshape(bn, nn, SSD_BC_GROUPS, SSD_STATE), rep, axis=2)
        return z, xs, bm, cm, dt_raw.reshape(bn, nn, 2, SSD_HEADS)

    z_l, x_l, b_l, c_l, dt_l = prep(u)
    z_c, x_c, b_c, c_c, dt_c = prep(u_ctx)
    y_l, y_c = [], []
    for d in range(2):
        a = -jnp.exp(a_log[d].astype(jnp.float32))
        dt = jax.nn.softplus(dt_c[:, :, d].astype(jnp.float32) + dt_bias[d])
        s0 = jnp.zeros((u_ctx.shape[0], SSD_HEADS, SSD_HD, SSD_STATE), jnp.float32)
        yc, s_c = ssd_chunk_scan(flip(x_c, d), flip(b_c, d), flip(c_c, d), flip(dt, d), flip(dt * a, d), s0)
        dt = jax.nn.softplus(dt_l[:, :, d].astype(jnp.float32) + dt_bias[d])
        yl, _ = ssd_chunk_scan(flip(x_l, d), flip(b_l, d), flip(c_l, d), flip(dt, d), flip(dt * a, d), s_c)
        y_l.append(flip(yl, d))
        y_c.append(flip(yc, d))

    def out(y, xs, z):
        y = (y + d_skip[:, None] * xs).reshape(y.shape[0], y.shape[1], GROUP_W)
        return rmsnorm(y * jax.nn.silu(z.astype(jnp.float32)), norm_w).astype(u.dtype)

    out_l = out(y_l[0] + y_l[1], x_l, z_l)
    out_c = out(y_c[0] + y_c[1], x_c, z_c) if need_ctx else None
    return out_l, out_c


def axial_rope_tables(n_tokens):
    rows = n_tokens // GRID_W
    row = jnp.repeat(jnp.arange(rows, dtype=jnp.float32), GRID_W)
    col = jnp.tile(jnp.arange(GRID_W, dtype=jnp.float32), rows)
    half = MLA_ROPE // 2
    inv = ROPE_BASE ** (-jnp.arange(0, half, 2, dtype=jnp.float32) / half)
    ang = jnp.concatenate([row[:, None] * inv, col[:, None] * inv], axis=-1)
    return jnp.cos(ang), jnp.sin(ang)


def apply_rope(t, cos, sin):
    t1, t2 = t[..., 0::2], t[..., 1::2]
    cos = cos[:, None, :]
    sin = sin[:, None, :]
    return jnp.stack([t1 * cos - t2 * sin, t1 * sin + t2 * cos], axis=-1).reshape(t.shape)


def block_attention(q, k, v):
    scale = q.shape[-1] ** -0.5

    def one_block(qi):
        s = jnp.einsum('bqhd,bkhd->bhqk', qi, k).astype(jnp.float32) * scale
        p = jax.nn.softmax(s, axis=-1).astype(v.dtype)
        return jnp.einsum('bhqk,bkhd->bqhd', p, v)

    return from_chunks(lax.map(one_block, to_chunks(q, Q_BLOCK)))


def mla_mixer(u, u_ctx, cos, sin, q_a_norm, w_q_up, kv_a_norm, w_kv_up, q_norm, k_norm, need_ctx):
    def project(v, rotary):
        bn, nn = v.shape[:2]
        cq, ckv, kr = jnp.split(v, [MLA_Q_RANK, MLA_Q_RANK + MLA_KV_RANK], axis=-1)
        q = (rmsnorm(cq, q_a_norm) @ w_q_up).reshape(bn, nn, MLA_HEADS, MLA_NOPE + MLA_ROPE)
        kv = (rmsnorm(ckv, kv_a_norm) @ w_kv_up).reshape(bn, nn, MLA_HEADS, MLA_NOPE + MLA_V)
        q_nope = rmsnorm(q[..., :MLA_NOPE], q_norm[:MLA_NOPE])
        q_rope = rmsnorm(q[..., MLA_NOPE:], q_norm[MLA_NOPE:])
        k_nope = rmsnorm(kv[..., :MLA_NOPE], k_norm[:MLA_NOPE])
        k_rope = rmsnorm(kr, k_norm[MLA_NOPE:])[:, :, None, :]
        if rotary:
            q_rope = apply_rope(q_rope, cos, sin)
            k_rope = apply_rope(k_rope, cos, sin)
        k = jnp.concatenate([k_nope, jnp.broadcast_to(k_rope, k_nope.shape[:-1] + (MLA_ROPE,))], axis=-1)
        q = jnp.concatenate([q_nope, q_rope], axis=-1)
        return q, k, kv[..., MLA_NOPE:]

    q_l, k_l, v_l = project(u, True)
    q_c, k_c, v_c = project(u_ctx, False)
    k_all = jnp.concatenate([k_c, k_l], axis=1)
    v_all = jnp.concatenate([v_c, v_l], axis=1)
    o_l = block_attention(q_l, k_all, v_all)
    y_l = o_l.reshape(o_l.shape[0], o_l.shape[1], GROUP_W)
    if need_ctx:
        o_c = block_attention(q_c, k_c, v_c)
        return y_l, o_c.reshape(o_c.shape[0], o_c.shape[1], GROUP_W)
    return y_l, None


def expert_choice_ffn(h, w_router, w_gate, w_up, w_down):
    n, dm = h.shape[1], h.shape[2]
    cap = EC_FACTOR * n // N_EXPERTS
    aff = jax.nn.softmax(jnp.einsum('bnd,de->bne', h, w_router).astype(jnp.float32), axis=-1)
    g, idx = lax.top_k(aff.swapaxes(1, 2), cap)
    xs = jax.vmap(lambda hb, ib: hb[ib])(h, idx)
    act = jax.nn.silu(jnp.einsum('becd,edf->becf', xs, w_gate)) * jnp.einsum('becd,edf->becf', xs, w_up)
    ys = jnp.einsum('becf,efd->becd', act, w_down) * g[..., None].astype(h.dtype)

    def combine(ib, yb):
        return jnp.zeros((n, dm), yb.dtype).at[ib.reshape(-1)].add(yb.reshape(-1, dm))

    return jax.vmap(combine)(idx, ys)


def setup_inputs(seed: int = 0) -> dict:
    key = jax.random.key(seed)
    keys = iter(jax.random.split(key, 48))
    f32 = jnp.float32
    L = DEPTH

    def normal(shape, scale):
        return jax.random.normal(next(keys), shape, f32) * scale

    def gain(shape):
        return 1.0 + normal(shape, 0.02)

    a0 = jax.random.uniform(next(keys), (L, 2, GROUP_W), f32, 0.9, 0.999)
    dt0 = jnp.exp(jax.random.uniform(next(keys), (L, 2, SSD_HEADS), f32, math.log(1e-3), math.log(1e-1)))
    a_init = jax.random.uniform(next(keys), (L, 2, SSD_HEADS), f32, 1.0, 16.0)
    return {
        'x': normal((BATCH, SEQ, D_MODEL), 1.0),
        'c': normal((BATCH, D_MODEL), 1.0),
        'ctx': normal((BATCH, CTX_LEN, D_MODEL), 1.0),
        'c_ctx': normal((D_MODEL,), 1.0),
        'ada_w': normal((L, D_MODEL, N_MOD * D_MODEL), 0.5 * D_MODEL ** -0.5),
        'ada_b': normal((L, N_MOD * D_MODEL), 0.02),
        'norm1_w': gain((L, D_MODEL)),
        'norm2_w': gain((L, D_MODEL)),
        'w_in': normal((L, D_MODEL, IN_COLS), D_MODEL ** -0.5),
        'w_out': normal((L, D_MIX, D_MODEL), D_MIX ** -0.5),
        'lru_conv_w': normal((L, CONV_W, GROUP_W), CONV_W ** -0.5),
        'lru_conv_b': normal((L, GROUP_W), 0.02),
        'lru_w_r': normal((L, 2, LRU_HEADS, LRU_HD, LRU_HD), LRU_HD ** -0.5),
        'lru_b_r': normal((L, 2, GROUP_W), 0.02),
        'lru_w_i': normal((L, 2, LRU_HEADS, LRU_HD, LRU_HD), LRU_HD ** -0.5),
        'lru_b_i': normal((L, 2, GROUP_W), 0.02),
        'lru_lam': jnp.log(a0) - jnp.log1p(-a0),
        'hgrn_lb_logits': normal((L, GROUP_W), 0.5),
        'hgrn_norm_w': gain((L, GROUP_W)),
        'ssd_conv_w': normal((L, CONV_W, SSD_XBC), CONV_W ** -0.5),
        'ssd_conv_b': normal((L, SSD_XBC), 0.02),
        'ssd_a_log': jnp.log(a_init),
        'ssd_dt_bias': dt0 + jnp.log(-jnp.expm1(-dt0)),
        'ssd_d_skip': gain((L, SSD_HEADS)),
        'ssd_norm_w': gain((L, GROUP_W)),
        'mla_q_a_norm': gain((L, MLA_Q_RANK)),
        'mla_w_q_up': normal((L, MLA_Q_RANK, MLA_HEADS * (MLA_NOPE + MLA_ROPE)), MLA_Q_RANK ** -0.5),
        'mla_kv_a_norm': gain((L, MLA_KV_RANK)),
        'mla_w_kv_up': normal((L, MLA_KV_RANK, MLA_HEADS * (MLA_NOPE + MLA_V)), MLA_KV_RANK ** -0.5),
        'mla_q_norm': gain((L, MLA_NOPE + MLA_ROPE)),
        'mla_k_norm': gain((L, MLA_NOPE + MLA_ROPE)),
        'moe_router': normal((L, D_MODEL, N_EXPERTS), D_MODEL ** -0.5),
        'moe_w_gate': normal((L, N_EXPERTS, D_MODEL, EXPERT_FF), D_MODEL ** -0.5),
        'moe_w_up': normal((L, N_EXPERTS, D_MODEL, EXPERT_FF), D_MODEL ** -0.5),
        'moe_w_down': normal((L, N_EXPERTS, EXPERT_FF, D_MODEL), EXPERT_FF ** -0.5),
    }


def reference(x, c, ctx, c_ctx, ada_w, ada_b, norm1_w, norm2_w, w_in, w_out,
              lru_conv_w, lru_conv_b, lru_w_r, lru_b_r, lru_w_i, lru_b_i, lru_lam,
              hgrn_lb_logits, hgrn_norm_w,
              ssd_conv_w, ssd_conv_b, ssd_a_log, ssd_dt_bias, ssd_d_skip, ssd_norm_w,
              mla_q_a_norm, mla_w_q_up, mla_kv_a_norm, mla_w_kv_up, mla_q_norm, mla_k_norm,
              moe_router, moe_w_gate, moe_w_up, moe_w_down):
    cos, sin = axial_rope_tables(x.shape[1])
    lb_w = jax.nn.softmax(hgrn_lb_logits.astype(jnp.float32), axis=0)
    lb_all = jnp.cumsum(lb_w, axis=0) - lb_w[0]
    s_c = jax.nn.silu(c)
    s_cc = jax.nn.silu(c_ctx)
    for l in range(DEPTH):
        need_ctx = l < DEPTH - 1
        mod_l = jnp.split((s_c @ ada_w[l] + ada_b[l])[:, None, :], N_MOD, axis=-1)
        mod_c = jnp.split((s_cc @ ada_w[l] + ada_b[l])[None, None, :], N_MOD, axis=-1)
        h = modulate(rmsnorm(x, norm1_w[l]), mod_l[0], mod_l[1])
        hc = modulate(rmsnorm(ctx, norm1_w[l]), mod_c[0], mod_c[1])
        u_a, u_b, u_s, u_m = jnp.split(h @ w_in[l], SPLITS, axis=-1)
        c_a, c_b, c_s, c_m = jnp.split(hc @ w_in[l], SPLITS, axis=-1)
        ya, yca = rglru_mixer(u_a, c_a, lru_conv_w[l], lru_conv_b[l], lru_w_r[l], lru_b_r[l],
                              lru_w_i[l], lru_b_i[l], lru_lam[l], need_ctx)
        yb, ycb = hgrn2_mixer(u_b, c_b, lb_all[l], hgrn_norm_w[l], need_ctx)
        ys, ycs = ssd_mixer(u_s, c_s, ssd_conv_w[l], ssd_conv_b[l], ssd_a_log[l], ssd_dt_bias[l],
                            ssd_d_skip[l], ssd_norm_w[l], need_ctx)
        ym, ycm = mla_mixer(u_m, c_m, cos, sin, mla_q_a_norm[l], mla_w_q_up[l], mla_kv_a_norm[l],
                            mla_w_kv_up[l], mla_q_norm[l], mla_k_norm[l], need_ctx)
        x = x + mod_l[2] * (jnp.concatenate([ya, yb, ys, ym], axis=-1) @ w_out[l])
        h = modulate(rmsnorm(x, norm2_w[l]), mod_l[3], mod_l[4])
        x = x + mod_l[5] * expert_choice_ffn(h, moe_router[l], moe_w_gate[l], moe_w_up[l], moe_w_down[l])
        if need_ctx:
            ctx = ctx + mod_c[2] * (jnp.concatenate([yca, ycb, ycs, ycm], axis=-1) @ w_out[l])
            hc = modulate(rmsnorm(ctx, norm2_w[l]), mod_c[3], mod_c[4])
            ctx = ctx + mod_c[5] * expert_choice_ffn(hc, moe_router[l], moe_w_gate[l], moe_w_up[l], moe_w_down[l])
    return x
```

```python
import functools
import math

import jax
import jax.numpy as jnp
import numpy as np
from jax import lax
from jax.experimental import pallas as pl
from jax.experimental.pallas import tpu as pltpu

F32 = jnp.float32
BF16 = jnp.bfloat16
HI = lax.Precision.HIGHEST

D = 1024
DEPTH = 2
N_LAT = 2048
N_CTX = 256
NT = N_LAT + N_CTX
GRID_W = 64
GW = 256
CONV_W = 4
LRU_HEADS, LRU_HD, LRU_C = 4, 64, 8.0
HG_HEADS, HG_HD = 4, 64
SSD_HEADS, SSD_HD, SSD_GROUPS, SSD_STATE = 4, 64, 2, 64
SSD_XBC = GW + 2 * SSD_GROUPS * SSD_STATE
MLA_HEADS, MLA_QR, MLA_KVR, MLA_NOPE, MLA_ROPE = 4, 192, 128, 64, 32
MLA_V = GW // MLA_HEADS
ROPE_BASE = 10000.0
N_EXP, FF, EC_FACTOR = 16, 512, 2
N_MOD = 6
EPS = 1e-6
LRU_COLS = 2 * GW
HG_COLS = 5 * GW
SSD_COLS = GW + SSD_XBC + 2 * SSD_HEADS
MLA_COLS = MLA_QR + MLA_KVR + MLA_ROPE

LANES = 128
SUBLANES = 8
ROW_TILE = 256
N_TILES = NT // ROW_TILE
LAT_TILES = N_LAT // ROW_TILE
UA_W, UB_W, US_W, UM_W = 512, 1280, 896, 512
U_OFFS = (0, UA_W, UA_W + UB_W, UA_W + UB_W + US_W, UA_W + UB_W + US_W + UM_W)
SSD_CH = 128
HG_CH = 64
VMEM_LIMIT = 52 * 1024 * 1024
HEAD_SHIFT = 6
HB_SHIFT = 7
assert LRU_HD == HG_HD == SSD_HD == MLA_V == 1 << HEAD_SHIFT and LANES == 1 << HB_SHIFT


def _cparams(sem):
    return pltpu.CompilerParams(dimension_semantics=sem, vmem_limit_bytes=VMEM_LIMIT)


def _sigmoid(x):
    return 1.0 / (1.0 + jnp.exp(-x))


def _silu(x):
    return x * _sigmoid(x)


def _softplus(x):
    return jnp.maximum(x, 0.0) + jnp.log(1.0 + jnp.exp(-jnp.abs(x)))


def _gelu_tanh(x):
    return 0.5 * x * (1.0 + jnp.tanh(math.sqrt(2.0 / math.pi) * (x + 0.044715 * (x * x * x))))


def _dot(a, b, **kw):
    return jnp.dot(a, b, preferred_element_type=F32, **kw)


def _dot_nt(a, b, **kw):
    return lax.dot_general(a, b, (((1,), (1,)), ((), ())), preferred_element_type=F32, **kw)


def _dot_tn(a, b, **kw):
    return lax.dot_general(a, b, (((0,), (0,)), ((), ())), preferred_element_type=F32, **kw)


ADA_ROWS = 24
ADA_TN = 512


def _ada_kernel(c_ref, w_ref, b_ref, o_ref):
    s = _silu(c_ref[...])
    o_ref[0] = _dot(s, w_ref[0], precision=HI) + b_ref[0]


def _ada_call(cc, ada_w, ada_b):
    return pl.pallas_call(
        _ada_kernel,
        grid=(DEPTH, N_MOD * D // ADA_TN),
        in_specs=[
            pl.BlockSpec((ADA_ROWS, D), lambda l, j: (0, 0)),
            pl.BlockSpec((1, D, ADA_TN), lambda l, j: (l, 0, j)),
            pl.BlockSpec((1, 1, ADA_TN), lambda l, j: (l, 0, j)),
        ],
        out_specs=pl.BlockSpec((1, ADA_ROWS, ADA_TN), lambda l, j: (l, 0, j)),
        out_shape=jax.ShapeDtypeStruct((DEPTH, ADA_ROWS, N_MOD * D), F32),
        compiler_params=_cparams(("parallel", "parallel")),
        name="ada",
    )(cc, ada_w, ada_b.reshape(DEPTH, 1, N_MOD * D))


def _norm_mod(x, nw, shift, scale):
    ms = jnp.mean(x * x, axis=-1, keepdims=True)
    return (x * lax.rsqrt(ms + EPS) * nw) * (1.0 + scale) + shift


def _in_kernel(add_moe, *refs):
    if add_moe:
        x_ref, moe_ref, mod_ref, nw_ref, w_ref, xo_ref, ua_ref, ub_ref, us_ref, um_ref = refs
        x = x_ref[0] + moe_ref[0]
        xo_ref[0] = x
    else:
        x_ref, mod_ref, nw_ref, w_ref, ua_ref, ub_ref, us_ref, um_ref = refs
        x = x_ref[0]
    h = _norm_mod(x, nw_ref[...], mod_ref[0, 0:1, :], mod_ref[0, 1:2, :]).astype(BF16)
    for k, ref in enumerate((ua_ref, ub_ref, us_ref, um_ref)):
        ref[0] = _dot(h, w_ref[:, U_OFFS[k]:U_OFFS[k + 1]])


def _mod_index(b, t, nb):
    return (jnp.where(t >= LAT_TILES, nb, b), 0, 0)


def _in_call(x, moe, modt, nw, w_all):
    nb = x.shape[0]
    tok = pl.BlockSpec((1, ROW_TILE, D), lambda b, t: (b, t, 0))
    in_specs = [tok] + ([tok] if moe is not None else []) + [
        pl.BlockSpec((1, N_MOD, D), lambda b, t: _mod_index(b, t, nb)),
        pl.BlockSpec((1, D), lambda b, t: (0, 0)),
        pl.BlockSpec((D, U_OFFS[-1]), lambda b, t: (0, 0)),
    ]
    widths = (UA_W, UB_W, US_W, UM_W)
    out_specs = [pl.BlockSpec((1, ROW_TILE, w), lambda b, t: (b, t, 0)) for w in widths]
    out_shape = [jax.ShapeDtypeStruct((nb, NT, w), F32) for w in widths]
    if moe is not None:
        out_specs = [tok] + out_specs
        out_shape = [jax.ShapeDtypeStruct((nb, NT, D), F32)] + out_shape
    args = (x,) + ((moe,) if moe is not None else ()) + (modt, nw, w_all)
    return pl.pallas_call(
        functools.partial(_in_kernel, moe is not None),
        grid=(nb, N_TILES),
        in_specs=in_specs,
        out_specs=out_specs,
        out_shape=out_shape,
        compiler_params=_cparams(("parallel", "parallel")),
        name="in_proj",
    )(*args)


def _conv_chunk(u_ref, r0, rows, c0, c1, cw_ref, cb_ref, at_seg_start, at_seg_end):
    x = u_ref[0, pl.ds(r0, rows), c0:c1]
    rp = pl.multiple_of(jnp.maximum(r0 - SUBLANES, 0), SUBLANES)
    rn = pl.multiple_of(jnp.minimum(r0 + rows, NT - SUBLANES), SUBLANES)
    xp = u_ref[0, pl.ds(rp, SUBLANES), c0:c1] * jnp.where(at_seg_start, 0.0, 1.0)
    xn = u_ref[0, pl.ds(rn, SUBLANES), c0:c1] * jnp.where(at_seg_end, 0.0, 1.0)
    xe = jnp.concatenate([xp, x, xn], axis=0)
    tot = rows + 2 * SUBLANES
    lo, hi = SUBLANES, SUBLANES + rows
    xm2 = pltpu.roll(xe, 2, 0)[lo:hi]
    xm1 = pltpu.roll(xe, 1, 0)[lo:hi]
    xp1 = pltpu.roll(xe, tot - 1, 0)[lo:hi]
    return cw_ref[0:1, :] * xm2 + cw_ref[1:2, :] * xm1 + cw_ref[2:3, :] * x + cw_ref[3:4, :] * xp1 + cb_ref[...]


LRU_CH = 256
LRU_NCH = NT // LRU_CH
LRU_LAT_CH = N_LAT // LRU_CH


def _scan_chunk(a, b, rows, reverse):
    n = a.shape[0]
    s = 1
    while s < n:
        if reverse:
            a_s = pltpu.roll(a, n - s, 0)
            b_s = pltpu.roll(b, n - s, 0)
            valid = rows < n - s
        else:
            a_s = pltpu.roll(a, s, 0)
            b_s = pltpu.roll(b, s, 0)
            valid = rows >= s
        b = b + a * jnp.where(valid, b_s, 0.0)
        a = a * jnp.where(valid, a_s, 1.0)
        s *= 2
    return a, b


def _lru_kernel(u_ref, cw_ref, cb_ref, wri_ref, bri_ref, lam_ref, y_ref, h0_ref):
    rows = lax.broadcasted_iota(jnp.int32, (LRU_CH, GW), 0)
    for d in (0, 1):
        sp = _softplus(-lam_ref[d:d + 1, :])

        def body(i, hprev, d=d, sp=sp):
            c = (i + LRU_LAT_CH) % LRU_NCH if d == 0 else LRU_LAT_CH - i
            r0 = pl.multiple_of(c * LRU_CH, LRU_CH)
            is_ctx = c == LRU_LAT_CH
            xc = _conv_chunk(u_ref, r0, LRU_CH, 0, GW, cw_ref, cb_ref,
                             (c == 0) | is_ctx, (c == LRU_LAT_CH - 1) | is_ctx)
            g = _dot(xc.astype(BF16), wri_ref[:, d * 2 * GW:(d + 1) * 2 * GW]) + bri_ref[:, d * 2 * GW:(d + 1) * 2 * GW]
            r = _sigmoid(g[:, :GW])
            ig = _sigmoid(g[:, GW:])
            a = jnp.exp(-LRU_C * r * sp)
            b = jnp.sqrt(1.0 - a * a) * (ig * xc)
            a_tot, h = _scan_chunk(a, b, rows, reverse=(d == 1))
            h = h + a_tot * hprev
            if d == 0:
                h0_ref[pl.ds(r0, LRU_CH), :] = h
                return h[LRU_CH - 1:LRU_CH, :]
            gate = u_ref[0, pl.ds(r0, LRU_CH), GW:2 * GW]
            y_ref[0, pl.ds(r0, LRU_CH), :] = ((h0_ref[pl.ds(r0, LRU_CH), :] + h) * _gelu_tanh(gate)).astype(BF16)
            return h[0:1, :]

        lax.fori_loop(0, LRU_NCH, body, jnp.zeros((1, GW), F32))


def _lru_call(ua, cw, cb, wri, bri, lam):
    nb = ua.shape[0]
    full = lambda shape: pl.BlockSpec(shape, lambda b: (0,) * len(shape))
    return pl.pallas_call(
        _lru_kernel,
        grid=(nb,),
        in_specs=[
            pl.BlockSpec((1, NT, UA_W), lambda b: (b, 0, 0)),
            full((CONV_W, GW)), full((1, GW)), full((GW, 4 * GW)), full((1, 4 * GW)), full((2, GW)),
        ],
        out_specs=pl.BlockSpec((1, NT, GW), lambda b: (b, 0, 0)),
        out_shape=jax.ShapeDtypeStruct((nb, NT, GW), BF16),
        scratch_shapes=[pltpu.VMEM((NT, GW), F32)],
        compiler_params=_cparams(("parallel",)),
        name="rglru",
    )(ua, cw, cb, wri, bri, lam)


HG_NCH = NT // HG_CH
HG_LAT_CH = N_LAT // HG_CH
HG_LEVELS = HG_CH.bit_length() - 1
HG_EXP_BLOCKS = 2 + 2 * HG_LEVELS
HG_SPLIT = 3


def _hgrn_tables():
    t_ = HG_CH
    dm = np.zeros((2, HG_EXP_BLOCKS, t_, t_), np.float32)
    lm = np.zeros((2, HG_LEVELS + 1, t_, t_), np.float32)
    for d in (0, 1):
        for t in range(t_):
            if d == 0:
                dm[d, 0, t, :t + 1] = 1.0
                dm[d, 1, t, t + 1:] = 1.0
            else:
                dm[d, 0, t, t:] = 1.0
                dm[d, 1, t, :t] = 1.0
            lm[d, 0, t, t] = 1.0
        for lev in range(1, HG_LEVELS + 1):
            m = 1 << (lev - 1)
            for t in range(t_):
                start = (t // (2 * m)) * 2 * m
                mid = start + m
                upper = t >= mid
                if d == 0:
                    if upper:
                        dm[d, 2 * lev, t, mid:t + 1] = 1.0
                        lm[d, lev, t, start:mid] = 1.0
                    else:
                        dm[d, 2 * lev + 1, t, t + 1:mid] = 1.0
                else:
                    if not upper:
                        dm[d, 2 * lev, t, t:mid] = 1.0
                        lm[d, lev, t, mid:start + 2 * m] = 1.0
                    else:
                        dm[d, 2 * lev + 1, t, mid:t] = 1.0
    dm = dm.reshape(2, HG_EXP_BLOCKS * t_, t_)
    d3 = np.concatenate([dm] * HG_SPLIT, axis=-1)
    lm = np.tile(lm, (1, 1, 1, HG_HEADS))
    heads = np.arange(GW) // HG_HD
    bm = (heads[:, None] == heads[None, :]).astype(np.float32)
    return jnp.asarray(d3, BF16), jnp.asarray(lm, F32), jnp.asarray(bm, F32), jnp.asarray(bm, BF16)


def _hgrn_kernel(u_ref, lb_ref, nw_ref, d3_ref, lm_ref, bm_ref, bmb_ref, y_ref, o_s, st_s, stb_s):
    t = HG_CH
    lb = lb_ref[...]
    st_s[...] = jnp.zeros(st_s.shape, F32)
    stb_s[...] = jnp.zeros(stb_s.shape, BF16)

    def by_head(xb):
        return jnp.concatenate([xb] * HG_HEADS, axis=0) * bmb_ref[...]

    def chunk(d, c):
        r0 = pl.multiple_of(c * t, t)
        q = _silu(u_ref[0, pl.ds(r0, t), 0:GW])
        fr = u_ref[0, pl.ds(r0, t), (1 + d) * GW:(2 + d) * GW]
        v = u_ref[0, pl.ds(r0, t), 3 * GW:4 * GW]
        f = lb + (1.0 - lb) * _sigmoid(fr)
        k = 1.0 - f
        lf = jnp.log2(f)
        parts = []
        rest = lf
        for _ in range(HG_SPLIT):
            p = rest.astype(BF16)
            parts.append(p)
            rest = rest - p.astype(F32)
        ex = _dot(d3_ref[d], jnp.concatenate(parts, axis=0))
        cum = ex[0:t]
        to_end = ex[t:2 * t]
        a = _dot_nt(q.astype(BF16), by_head(k.astype(BF16))) * lm_ref[d, 0]
        for lev in range(1, HG_LEVELS + 1):
            eq = ex[2 * lev * t:(2 * lev + 1) * t]
            ek = ex[(2 * lev + 1) * t:(2 * lev + 2) * t]
            qt = (q * jnp.exp2(eq)).astype(BF16)
            kt = (k * jnp.exp2(ek)).astype(BF16)
            a = a + _dot_nt(qt, by_head(kt)) * lm_ref[d, lev]
        vb = v.astype(BF16)
        o = _dot(a.astype(BF16), by_head(vb)) + _dot_nt((q * jnp.exp2(cum)).astype(BF16), stb_s[d])
        o_s[d, pl.ds(r0, t), :] = o
        cend = cum[t - 1:t, :] if d == 0 else cum[0:1, :]
        kend = (k * jnp.exp2(to_end)).astype(BF16)
        st = st_s[d] * jnp.exp2(cend) + _dot_tn(vb, kend) * bm_ref[...]
        st_s[d] = st
        stb_s[d] = st.astype(BF16)

    def body(i, carry):
        chunk(0, (i + HG_LAT_CH) % HG_NCH)
        chunk(1, HG_NCH - 1 - i)
        return carry

    lax.fori_loop(0, HG_NCH, body, 0)

    def finish(c, carry):
        r0 = pl.multiple_of(c * t, t)
        osum = o_s[0, pl.ds(r0, t), :] + o_s[1, pl.ds(r0, t), :]
        ms = _dot(osum * osum, bm_ref[...], precision=HI) * (1.0 / HG_HD)
        g = u_ref[0, pl.ds(r0, t), 4 * GW:5 * GW]
        y_ref[0, pl.ds(r0, t), :] = (osum * lax.rsqrt(ms + EPS) * nw_ref[...] * _silu(g)).astype(BF16)
        return carry

    lax.fori_loop(0, HG_NCH, finish, 0)


def _hgrn_call(ub, lb, nw, tables):
    nb = ub.shape[0]
    d3, lm, bm, bmb = tables
    full = lambda shape: pl.BlockSpec(shape, lambda b: (0,) * len(shape))
    return pl.pallas_call(
        _hgrn_kernel,
        grid=(nb,),
        in_specs=[pl.BlockSpec((1, NT, UB_W), lambda b: (b, 0, 0)), full((1, GW)), full((1, GW)),
                  full(d3.shape), full(lm.shape), full(bm.shape), full(bmb.shape)],
        out_specs=pl.BlockSpec((1, NT, GW), lambda b: (b, 0, 0)),
        out_shape=jax.ShapeDtypeStruct((nb, NT, GW), BF16),
        scratch_shapes=[pltpu.VMEM((2, NT, GW), F32), pltpu.VMEM((2, GW, GW), F32), pltpu.VMEM((2, GW, GW), BF16)],
        compiler_params=_cparams(("parallel",)),
        name="hgrn2",
    )(ub, lb, nw, d3, lm, bm, bmb)


SSD_NCH = NT // SSD_CH
SSD_LAT_CH = N_LAT // SSD_CH
SSD_DT0 = GW + SSD_XBC
GRP_W = SSD_HEADS // SSD_GROUPS * SSD_HD


def _ssd_kernel(u_ref, cw_ref, cb_ref, ex_ref, alog_ref, dtb_ref, dsk_ref, nw_ref, y_ref, y0_ref, st_ref):
    t = SSD_CH
    rr = lax.broadcasted_iota(jnp.int32, (t, t), 0)
    cc = lax.broadcasted_iota(jnp.int32, (t, t), 1)
    lane_head = lax.broadcasted_iota(jnp.int32, (t, GW), 1) >> HEAD_SHIFT
    for d in (0, 1):
        keep = rr >= cc if d == 0 else rr <= cc
        tri = jnp.where(keep, 1.0, 0.0)
        a_neg = -jnp.exp(alog_ref[d:d + 1, :])
        st_ref[...] = jnp.zeros((SSD_STATE, GW), F32)

        def body(i, carry, d=d, keep=keep, tri=tri, a_neg=a_neg):
            c = (i + SSD_LAT_CH) % SSD_NCH if d == 0 else SSD_NCH - 1 - i
            r0 = pl.multiple_of(c * t, t)
            seg_start = (c == 0) | (c == SSD_LAT_CH)
            seg_end = (c == SSD_LAT_CH - 1) | (c == SSD_NCH - 1)
            xbc = _silu(_conv_chunk(u_ref, r0, t, GW, GW + SSD_XBC, cw_ref, cb_ref, seg_start, seg_end))
            xs = xbc[:, 0:GW]
            bm = xbc[:, GW:GW + 2 * SSD_STATE].astype(BF16)
            cm = xbc[:, GW + 2 * SSD_STATE:GW + 4 * SSD_STATE].astype(BF16)
            dtr = _dot(u_ref[0, pl.ds(r0, t), SSD_DT0:SSD_DT0 + LANES], ex_ref[d], precision=HI)
            dt = _softplus(dtr + dtb_ref[d:d + 1, :])
            cum = _dot(tri, dt * a_neg, precision=HI)
            ecum = jnp.exp(cum)
            st = st_ref[...]
            xdt = xs * dt
            y_in = []
            y = jnp.zeros((t, GW), F32)
            for g in range(SSD_GROUPS):
                cg = cm[:, g * SSD_STATE:(g + 1) * SSD_STATE]
                bg = bm[:, g * SSD_STATE:(g + 1) * SSD_STATE]
                y_in.append(_dot(cg, st[:, g * GRP_W:(g + 1) * GRP_W].astype(BF16)))
                scores = _dot_nt(cg, bg)
                for hh in range(SSD_HEADS // SSD_GROUPS):
                    h = g * (SSD_HEADS // SSD_GROUPS) + hh
                    col = jnp.broadcast_to(cum[:, h * SSD_HD:h * SSD_HD + 1], (t, t))
                    decay = jnp.where(keep, jnp.exp(col - col.T), 0.0)
                    xh = jnp.where(lane_head == h, xdt, 0.0).astype(BF16)
                    y = y + _dot((scores * decay).astype(BF16), xh)
            y = y + jnp.concatenate(y_in, axis=1) * ecum
            cend = cum[t - 1:t, :] if d == 0 else cum[0:1, :]
            xw = (xdt * jnp.exp(cend - cum)).astype(BF16)
            upd = [_dot_tn(bm[:, g * SSD_STATE:(g + 1) * SSD_STATE], xw[:, g * GRP_W:(g + 1) * GRP_W])
                   for g in range(SSD_GROUPS)]
            st_ref[...] = st * jnp.exp(cend) + jnp.concatenate(upd, axis=1)
            if d == 0:
                y0_ref[pl.ds(r0, t), :] = y
            else:
                z = u_ref[0, pl.ds(r0, t), 0:GW]
                yy = (y0_ref[pl.ds(r0, t), :] + y + dsk_ref[...] * xs) * _silu(z)
                ms = jnp.mean(yy * yy, axis=-1, keepdims=True)
                y_ref[0, pl.ds(r0, t), :] = (yy * lax.rsqrt(ms + EPS) * nw_ref[...]).astype(BF16)
            return carry

        lax.fori_loop(0, SSD_NCH, body, 0)


def _ssd_call(us, cw, cb, ex, alog, dtb, dsk, nw):
    nb = us.shape[0]
    full = lambda shape: pl.BlockSpec(shape, lambda b: (0,) * len(shape))
    return pl.pallas_call(
        _ssd_kernel,
        grid=(nb,),
        in_specs=[
            pl.BlockSpec((1, NT, US_W), lambda b: (b, 0, 0)),
            full((CONV_W, SSD_XBC)), full((1, SSD_XBC)), full((2, LANES, GW)),
            full((2, GW)), full((2, GW)), full((1, GW)), full((1, GW)),
        ],
        out_specs=pl.BlockSpec((1, NT, GW), lambda b: (b, 0, 0)),
        out_shape=jax.ShapeDtypeStruct((nb, NT, GW), BF16),
        scratch_shapes=[pltpu.VMEM((NT, GW), F32), pltpu.VMEM((SSD_STATE, GW), F32)],
        compiler_params=_cparams(("parallel",)),
        name="ssd",
    )(us, cw, cb, ex, alog, dtb, dsk, nw)


HB = LANES
QW = MLA_HEADS * HB
MLA_QB = 256
MLA_SCALE = (MLA_NOPE + MLA_ROPE) ** -0.5


def _mla_kernel(need_ctx, u_ref, qa_ref, wq_ref, qn_ref, kva_ref, wkv_ref, kn_ref, krw_ref,
                cos_ref, sa_ref, sb_ref, y_ref, q_s, k_s, v_s):
    lane = lax.broadcasted_iota(jnp.int32, (QW, QW), 1)
    row = lax.broadcasted_iota(jnp.int32, (QW, QW), 0)

    def grp(i):
        within = i & (HB - 1)
        return (i >> HB_SHIFT) * 3 + jnp.where(within < MLA_NOPE, 0, jnp.where(within < MLA_NOPE + MLA_ROPE, 1, 2))

    gmat = jnp.where(grp(row) == grp(lane), 1.0, 0.0).astype(BF16)
    l1 = lax.broadcasted_iota(jnp.int32, (1, QW), 1) & (HB - 1)
    inv_size = jnp.where(l1 < MLA_NOPE, 1.0 / MLA_NOPE, 1.0 / MLA_ROPE)
    vlane_head = lax.broadcasted_iota(jnp.int32, (MLA_QB, GW), 1) >> HEAD_SHIFT

    def rope(x, r0, reps):
        cosr = cos_ref[pl.ds(r0, MLA_QB), :]
        sar = sa_ref[pl.ds(r0, MLA_QB), :]
        sbr = sb_ref[pl.ds(r0, MLA_QB), :]
        if reps > 1:
            cosr, sar, sbr = (jnp.concatenate([tbl] * reps, axis=1) for tbl in (cosr, sar, sbr))
        w = x.shape[1]
        return x * cosr + pltpu.roll(x, w - MLA_ROPE // 2, 1) * sar + pltpu.roll(x, MLA_ROPE // 2, 1) * sbr

    def project(ci, carry):
        r0 = pl.multiple_of(ci * MLA_QB, MLA_QB)
        cq = u_ref[0, pl.ds(r0, MLA_QB), 0:256]
        ms = jnp.sum(cq * cq, axis=-1, keepdims=True) * (1.0 / MLA_QR)
        qraw = _dot((cq * lax.rsqrt(ms + EPS) * qa_ref[...]).astype(BF16), wq_ref[...])
        ss = _dot((qraw * qraw).astype(BF16), gmat) * inv_size
        q = rope(qraw * lax.rsqrt(ss + EPS) * qn_ref[...], r0, MLA_HEADS) * MLA_SCALE
        q_s[pl.ds(r0, MLA_QB), :] = q.astype(BF16)
        ckv = u_ref[0, pl.ds(r0, MLA_QB), 256:256 + MLA_KVR]
        ms = jnp.mean(ckv * ckv, axis=-1, keepdims=True)
        kv = _dot((ckv * lax.rsqrt(ms + EPS) * kva_ref[...]).astype(BF16), wkv_ref[...])
        kraw = kv[:, 0:QW]
        ss = _dot((kraw * kraw).astype(BF16), gmat) * inv_size
        knope = kraw * lax.rsqrt(ss + EPS) * kn_ref[...]
        kr = u_ref[0, pl.ds(r0, MLA_QB), 384:512]
        ms = jnp.sum(kr * kr, axis=-1, keepdims=True) * (1.0 / MLA_ROPE)
        krope = rope(kr * lax.rsqrt(ms + EPS) * krw_ref[...], r0, 1)
        k_s[pl.ds(r0, MLA_QB), :] = (knope + jnp.concatenate([krope] * MLA_HEADS, axis=1)).astype(BF16)
        vv = kv[:, QW:QW + GW]
        for h in range(MLA_HEADS):
            v_s[h, pl.ds(r0, MLA_QB), :] = jnp.where(vlane_head == h, vv, 0.0).astype(BF16)
        return carry

    lax.fori_loop(0, NT // MLA_QB, project, 0)

    def attend(r0, k0, klen):
        o = jnp.zeros((MLA_QB, GW), F32)
        for h in range(MLA_HEADS):
            qh = q_s[pl.ds(r0, MLA_QB), h * HB:(h + 1) * HB]
            s = _dot_nt(qh, k_s[k0:k0 + klen, h * HB:(h + 1) * HB])
            p = jnp.exp(s - jnp.max(s, axis=-1, keepdims=True))
            l = jnp.sum(p, axis=-1, keepdims=True)
            o = o + _dot(p.astype(BF16), v_s[h, k0:k0 + klen, :]) * (1.0 / l)
        y_ref[0, pl.ds(r0, MLA_QB), :] = o.astype(BF16)

    def lat_block(qi, carry):
        attend(pl.multiple_of(qi * MLA_QB, MLA_QB), 0, NT)
        return carry

    lax.fori_loop(0, N_LAT // MLA_QB, lat_block, 0)
    if need_ctx:
        attend(N_LAT, N_LAT, N_CTX)
    else:
        y_ref[0, N_LAT:NT, :] = jnp.zeros((N_CTX, GW), BF16)


def _mla_call(um, need_ctx, qa, wq, qn, kva, wkv, kn, krw, cos_t, sa_t, sb_t):
    nb = um.shape[0]
    full = lambda shape: pl.BlockSpec(shape, lambda b: (0,) * len(shape))
    return pl.pallas_call(
        functools.partial(_mla_kernel, need_ctx),
        grid=(nb,),
        in_specs=[
            pl.BlockSpec((1, NT, UM_W), lambda b: (b, 0, 0)),
            full((1, 256)), full((256, QW)), full((1, QW)),
            full((1, MLA_KVR)), full((MLA_KVR, QW + GW)), full((1, QW)), full((1, HB)),
            full((NT, HB)), full((NT, HB)), full((NT, HB)),
        ],
        out_specs=pl.BlockSpec((1, NT, GW), lambda b: (b, 0, 0)),
        out_shape=jax.ShapeDtypeStruct((nb, NT, GW), BF16),
        scratch_shapes=[pltpu.VMEM((NT, QW), BF16), pltpu.VMEM((NT, QW), BF16),
                        pltpu.VMEM((MLA_HEADS, NT, GW), BF16)],
        compiler_params=_cparams(("parallel",)),
        name="mla",
    )(um, qa, wq, qn, kva, wkv, kn, krw, cos_t, sa_t, sb_t)


def _out_kernel(x_ref, ya_ref, yb_ref, ys_ref, ym_ref, wo_ref, mod_ref, nw_ref, wr_ref, xo_ref, h2_ref, aff_ref):
    acc = _dot(ya_ref[0], wo_ref[0])
    acc = acc + _dot(yb_ref[0], wo_ref[1])
    acc = acc + _dot(ys_ref[0], wo_ref[2])
    acc = acc + _dot(ym_ref[0], wo_ref[3])
    x = x_ref[0] + mod_ref[0, 2:3, :] * acc
    xo_ref[0] = x
    h2 = _norm_mod(x, nw_ref[...], mod_ref[0, 3:4, :], mod_ref[0, 4:5, :])
    h2_ref[0] = h2.astype(BF16)
    logit = _dot_nt(wr_ref[...], h2, precision=HI)
    e = jnp.exp(logit - jnp.max(logit, axis=0, keepdims=True))
    aff_ref[0] = e / jnp.sum(e, axis=0, keepdims=True)


def _out_call(x, ys, wo4, modt, nw, wr_t, n_tiles):
    nb = x.shape[0]
    tok = lambda w: pl.BlockSpec((1, ROW_TILE, w), lambda b, t: (b, t, 0))
    return pl.pallas_call(
        _out_kernel,
        grid=(nb, n_tiles),
        in_specs=[tok(D), tok(GW), tok(GW), tok(GW), tok(GW),
                  pl.BlockSpec((4, GW, D), lambda b, t: (0, 0, 0)),
                  pl.BlockSpec((1, N_MOD, D), lambda b, t: _mod_index(b, t, nb)),
                  pl.BlockSpec((1, D), lambda b, t: (0, 0)),
                  pl.BlockSpec((N_EXP, D), lambda b, t: (0, 0))],
        out_specs=[tok(D), tok(D), pl.BlockSpec((1, N_EXP, ROW_TILE), lambda b, t: (b, 0, t))],
        out_shape=[jax.ShapeDtypeStruct((nb, NT, D), F32),
                   jax.ShapeDtypeStruct((nb, NT, D), BF16),
                   jax.ShapeDtypeStruct((nb, N_EXP, NT), F32)],
        compiler_params=_cparams(("parallel", "parallel")),
        name="out_proj",
    )(x, *ys, wo4, modt, nw, wr_t)


PRE_W = 256
ROUTE_MAX_IT = 160


def _prefix_count(m, tri_bf):
    n = m.shape[1]
    off = jnp.zeros((m.shape[0], 1), F32)
    outs = []
    for j in range(n // PRE_W):
        blk = m[:, j * PRE_W:(j + 1) * PRE_W]
        outs.append(_dot(blk.astype(BF16), tri_bf) + off)
        off = off + jnp.sum(blk, axis=1, keepdims=True)
    return outs[0] if len(outs) == 1 else jnp.concatenate(outs, axis=1)


def _route(aff, cap, tri):
    n_lo0 = jnp.full((N_EXP, 1), float(aff.shape[1]), F32)

    def cond(s):
        it, _, _, n_lo, n_hi = s
        return jnp.logical_and(it < ROUTE_MAX_IT, jnp.max(n_lo - n_hi) > 1.0)

    def step(s):
        it, lo, hi, n_lo, n_hi = s
        mid = 0.5 * (lo + hi)
        cnt = jnp.sum(jnp.where(aff > mid, 1.0, 0.0), axis=1, keepdims=True)
        below = jnp.max(jnp.where(aff <= mid, aff, -1.0), axis=1, keepdims=True)
        up = cnt >= cap
        return (it + 1, jnp.where(up, mid, lo), jnp.where(up, hi, below),
                jnp.where(up, cnt, n_lo), jnp.where(up, n_hi, cnt))

    init = (jnp.int32(0), jnp.full((N_EXP, 1), -1.0, F32), jnp.max(aff, axis=1, keepdims=True),
            n_lo0, jnp.zeros((N_EXP, 1), F32))
    _, _, thr, _, n_gt = lax.while_loop(cond, step, init)
    gt = jnp.where(aff > thr, 1.0, 0.0)
    eq = jnp.where(aff == thr, 1.0, 0.0)
    sel = gt + eq * jnp.where(_prefix_count(eq, tri) < cap - n_gt, 1.0, 0.0)
    return jnp.where(sel > 0.0, _prefix_count(sel, tri), -1.0)


def _moe_kernel(segs, aff_ref, h_ref, gl_ref, gc_ref, wg_ref, wu_ref, wd_ref, o_ref, pos_ref):
    e = pl.program_id(1)

    @pl.when(e == 0)
    def _init():
        tri = jnp.where(lax.broadcasted_iota(jnp.int32, (PRE_W, PRE_W), 0)
                        < lax.broadcasted_iota(jnp.int32, (PRE_W, PRE_W), 1), 1.0, 0.0).astype(BF16)
        for r0, n, cap in segs:
            pos_ref[:, r0:r0 + n] = _route(aff_ref[0, :, r0:r0 + n], cap, tri)
        o_ref[0] = jnp.zeros(o_ref.shape[1:], F32)

    onehots, gates, xs = [], [], []
    for (r0, n, cap), g_ref in zip(segs, (gl_ref, gc_ref)):
        posrow = pos_ref[pl.ds(e, 1), r0:r0 + n]
        affrow = aff_ref[0, pl.ds(e, 1), r0:r0 + n]
        hit = lax.broadcasted_iota(jnp.int32, (cap, n), 0).astype(F32) == posrow
        onehot = jnp.where(hit, 1.0, 0.0).astype(BF16)
        onehots.append(onehot)
        gates.append(jnp.sum(jnp.where(hit, affrow, 0.0), axis=1, keepdims=True) * g_ref[0, N_MOD - 1:N_MOD, :])
        xs.append(_dot(onehot, h_ref[0, r0:r0 + n, :]).astype(BF16))
    xs = xs[0] if len(xs) == 1 else jnp.concatenate(xs, axis=0)
    act = (_silu(_dot(xs, wg_ref[0])) * _dot(xs, wu_ref[0])).astype(BF16)
    ys = _dot(act, wd_ref[0])
    s0 = 0
    for (r0, n, cap), onehot, gate in zip(segs, onehots, gates):
        o_ref[0, r0:r0 + n, :] += _dot_tn(onehot, (ys[s0:s0 + cap] * gate).astype(BF16))
        s0 += cap


def _moe_call(aff, h2, modt, wg, wu, wd, with_ctx):
    nb = h2.shape[0]
    segs = ((0, N_LAT, EC_FACTOR * N_LAT // N_EXP),)
    if with_ctx:
        segs += ((N_LAT, N_CTX, EC_FACTOR * N_CTX // N_EXP),)
    rows = NT if with_ctx else N_LAT
    return pl.pallas_call(
        functools.partial(_moe_kernel, segs),
        grid=(nb, N_EXP),
        in_specs=[
            pl.BlockSpec((1, N_EXP, rows), lambda b, e: (b, 0, 0)),
            pl.BlockSpec((1, rows, D), lambda b, e: (b, 0, 0)),
            pl.BlockSpec((1, N_MOD, D), lambda b, e: (b, 0, 0)),
            pl.BlockSpec((1, N_MOD, D), lambda b, e: (nb, 0, 0)),
            pl.BlockSpec((1, D, FF), lambda b, e: (e, 0, 0)),
            pl.BlockSpec((1, D, FF), lambda b, e: (e, 0, 0)),
            pl.BlockSpec((1, FF, D), lambda b, e: (e, 0, 0)),
        ],
        out_specs=pl.BlockSpec((1, rows, D), lambda b, e: (b, 0, 0)),
        out_shape=jax.ShapeDtypeStruct((nb, rows, D), F32),
        scratch_shapes=[pltpu.VMEM((N_EXP, rows), F32)],
        compiler_params=_cparams(("parallel", "arbitrary")),
        name="moe",
    )(aff, h2, modt, modt, wg, wu, wd)


def _add_kernel(a_ref, b_ref, o_ref):
    o_ref[0] = a_ref[0] + b_ref[0]


def _add_call(a, b):
    nb = b.shape[0]
    blk = pl.BlockSpec((1, ROW_TILE, D), lambda i, t: (i, t, 0))
    return pl.pallas_call(
        _add_kernel, grid=(nb, LAT_TILES), in_specs=[blk, blk], out_specs=blk,
        out_shape=jax.ShapeDtypeStruct((nb, N_LAT, D), F32),
        compiler_params=_cparams(("parallel", "parallel")), name="residual_add",
    )(a, b)


def _pad_cols(w, width):
    return jnp.pad(w, ((0, 0), (0, width - w.shape[1])))


def _rope_perm():
    half = MLA_ROPE // 2
    return jnp.concatenate([jnp.arange(half) * 2, jnp.arange(half) * 2 + 1])


def _head_block(nope, rope):
    pad = jnp.zeros(nope.shape[:-1] + (HB - MLA_NOPE - MLA_ROPE,), nope.dtype)
    blk = jnp.concatenate([nope, rope, pad], axis=-1)
    return blk.reshape(blk.shape[:-2] + (MLA_HEADS * HB,))


def _prep_in_weights(w_in):
    o1, o2, o3 = LRU_COLS, LRU_COLS + HG_COLS, LRU_COLS + HG_COLS + SSD_COLS
    wa, wb, ws, wm = w_in[:, :o1], w_in[:, o1:o2], w_in[:, o2:o3], w_in[:, o3:]
    perm = _rope_perm()
    cq, ckv, kr = wm[:, :MLA_QR], wm[:, MLA_QR:MLA_QR + MLA_KVR], wm[:, MLA_QR + MLA_KVR:]
    zeros = lambda n: jnp.zeros((D, n), w_in.dtype)
    wm_p = jnp.concatenate([cq, zeros(256 - MLA_QR), ckv, zeros(MLA_NOPE), kr[:, perm],
                            zeros(HB - MLA_NOPE - MLA_ROPE)], axis=1)
    return jnp.concatenate([wa, wb, _pad_cols(ws, US_W), wm_p], axis=1).astype(BF16)


def _block_diag(w):
    h, dd, _ = w.shape
    eye = jnp.eye(h, dtype=w.dtype)
    return (eye[:, None, :, None] * w[:, :, None, :]).reshape(h * dd, h * dd)


def _rope_tables():
    rows = N_LAT // GRID_W
    row = jnp.repeat(jnp.arange(rows, dtype=F32), GRID_W)
    col = jnp.tile(jnp.arange(GRID_W, dtype=F32), rows)
    half = MLA_ROPE // 2
    inv = ROPE_BASE ** (-jnp.arange(0, half, 2, dtype=F32) / half)
    ang = jnp.concatenate([row[:, None] * inv, col[:, None] * inv], axis=-1)
    cos, sin = jnp.cos(ang), jnp.sin(ang)
    z = lambda n: jnp.zeros((N_LAT, n), F32)
    o = lambda n: jnp.ones((N_LAT, n), F32)
    cos_t = jnp.concatenate([o(MLA_NOPE), cos, cos, o(HB - MLA_NOPE - MLA_ROPE)], axis=1)
    sa_t = jnp.concatenate([z(MLA_NOPE), -sin, z(half), z(HB - MLA_NOPE - MLA_ROPE)], axis=1)
    sb_t = jnp.concatenate([z(MLA_NOPE), z(half), sin, z(HB - MLA_NOPE - MLA_ROPE)], axis=1)
    cos_t = jnp.concatenate([cos_t, jnp.ones((N_CTX, HB), F32)], axis=0)
    sa_t = jnp.concatenate([sa_t, jnp.zeros((N_CTX, HB), F32)], axis=0)
    sb_t = jnp.concatenate([sb_t, jnp.zeros((N_CTX, HB), F32)], axis=0)
    return cos_t, sa_t, sb_t


def kernel(x, c, ctx, c_ctx, ada_w, ada_b, norm1_w, norm2_w, w_in, w_out, lru_conv_w, lru_conv_b, lru_w_r, lru_b_r, lru_w_i, lru_b_i, lru_lam, hgrn_lb_logits, hgrn_norm_w, ssd_conv_w, ssd_conv_b, ssd_a_log, ssd_dt_bias, ssd_d_skip, ssd_norm_w, mla_q_a_norm, mla_w_q_up, mla_kv_a_norm, mla_w_kv_up, mla_q_norm, mla_k_norm, moe_router, moe_w_gate, moe_w_up, moe_w_down):
    nb = x.shape[0]
    assert x.shape == (nb, N_LAT, D) and ctx.shape == (nb, N_CTX, D)
    assert nb + 1 <= ADA_ROWS
    xs = jnp.concatenate([x, ctx], axis=1)
    cc = jnp.concatenate([c, c_ctx[None, :], jnp.zeros((ADA_ROWS - nb - 1, D), F32)], axis=0)
    mod_all = _ada_call(cc, ada_w, ada_b)[:, :nb + 1].reshape(DEPTH, nb + 1, N_MOD, D)
    cos_t, sa_t, sb_t = _rope_tables()
    hg_tables = _hgrn_tables()
    perm = _rope_perm()
    lb_w = jax.nn.softmax(hgrn_lb_logits.astype(F32), axis=0)
    lb_all = jnp.cumsum(lb_w, axis=0) - lb_w[0]
    rep = lambda v, n: jnp.repeat(v, n, axis=-1)
    moe_prev = None
    for l in range(DEPTH):
        need_ctx = l < DEPTH - 1
        modt = mod_all[l]
        res = _in_call(xs, moe_prev, modt, norm1_w[l][None, :], _prep_in_weights(w_in[l]))
        if moe_prev is not None:
            xs, ua, ub, us, um = res
        else:
            ua, ub, us, um = res
        wri = jnp.concatenate([_block_diag(lru_w_r[l, 0]), _block_diag(lru_w_i[l, 0]),
                               _block_diag(lru_w_r[l, 1]), _block_diag(lru_w_i[l, 1])], axis=1).astype(BF16)
        bri = jnp.concatenate([lru_b_r[l, 0], lru_b_i[l, 0], lru_b_r[l, 1], lru_b_i[l, 1]])[None, :]
        ya = _lru_call(ua, lru_conv_w[l], lru_conv_b[l][None, :], wri, bri, lru_lam[l])
        yb = _hgrn_call(ub, lb_all[l][None, :], hgrn_norm_w[l][None, :], hg_tables)
        ex = jnp.zeros((2, LANES, GW), F32)
        for dd in range(2):
            for h in range(SSD_HEADS):
                ex = ex.at[dd, dd * SSD_HEADS + h, h * SSD_HD:(h + 1) * SSD_HD].set(1.0)
        ysd = _ssd_call(us, ssd_conv_w[l], ssd_conv_b[l][None, :], ex, rep(ssd_a_log[l], SSD_HD),
                        rep(ssd_dt_bias[l], SSD_HD), rep(ssd_d_skip[l], SSD_HD)[None, :], ssd_norm_w[l][None, :])
        wq = mla_w_q_up[l].reshape(MLA_QR, MLA_HEADS, MLA_NOPE + MLA_ROPE)
        wq = _head_block(wq[..., :MLA_NOPE], wq[..., MLA_NOPE:][..., perm])
        wq = jnp.pad(wq, ((0, 256 - MLA_QR), (0, 0))).astype(BF16)
        qn = _head_block(jnp.broadcast_to(mla_q_norm[l][:MLA_NOPE], (MLA_HEADS, MLA_NOPE)),
                         jnp.broadcast_to(mla_q_norm[l][MLA_NOPE:][perm], (MLA_HEADS, MLA_ROPE)))[None, :]
        wkv = mla_w_kv_up[l].reshape(MLA_KVR, MLA_HEADS, MLA_NOPE + MLA_V)
        wk = _head_block(wkv[..., :MLA_NOPE], jnp.zeros((MLA_KVR, MLA_HEADS, MLA_ROPE), F32))
        wv = wkv[..., MLA_NOPE:].reshape(MLA_KVR, GW)
        wkv_p = jnp.concatenate([wk, wv], axis=1).astype(BF16)
        kn = _head_block(jnp.broadcast_to(mla_k_norm[l][:MLA_NOPE], (MLA_HEADS, MLA_NOPE)),
                         jnp.zeros((MLA_HEADS, MLA_ROPE), F32))[None, :]
        krw = jnp.concatenate([jnp.zeros((MLA_NOPE,), F32), mla_k_norm[l][MLA_NOPE:][perm],
                               jnp.zeros((HB - MLA_NOPE - MLA_ROPE,), F32)])[None, :]
        qa = jnp.pad(mla_q_a_norm[l], (0, 256 - MLA_QR))[None, :]
        ym = _mla_call(um, need_ctx, qa, wq, qn, mla_kv_a_norm[l][None, :], wkv_p, kn, krw, cos_t, sa_t, sb_t)
        n_tiles = N_TILES if need_ctx else LAT_TILES
        x_mid, h2, aff = _out_call(xs, (ya, yb, ysd, ym), w_out[l].reshape(4, GW, D).astype(BF16), modt,
                                   norm2_w[l][None, :], moe_router[l].T, n_tiles)
        wg, wu, wd = moe_w_gate[l].astype(BF16), moe_w_up[l].astype(BF16), moe_w_down[l].astype(BF16)
        moe = _moe_call(aff, h2, modt, wg, wu, wd, need_ctx)
        if need_ctx:
            moe_prev = moe
            xs = x_mid
        else:
            return _add_call(x_mid, moe)
```

```python
import functools
import math

import jax
import jax.numpy as jnp
import numpy as np
from jax import lax
from jax.experimental import pallas as pl
from jax.experimental.pallas import tpu as pltpu

F32 = jnp.float32
BF16 = jnp.bfloat16
HI = lax.Precision.HIGHEST

D = 1024
DEPTH = 2
N_LAT = 2048
N_CTX = 256
NT = N_LAT + N_CTX
GRID_W = 64
GW = 256
CONV_W = 4
LRU_HEADS, LRU_HD, LRU_C = 4, 64, 8.0
HG_HEADS, HG_HD = 4, 64
SSD_HEADS, SSD_HD, SSD_GROUPS, SSD_STATE = 4, 64, 2, 64
SSD_XBC = GW + 2 * SSD_GROUPS * SSD_STATE
MLA_HEADS, MLA_QR, MLA_KVR, MLA_NOPE, MLA_ROPE = 4, 192, 128, 64, 32
MLA_V = GW // MLA_HEADS
ROPE_BASE = 10000.0
N_EXP, FF, EC_FACTOR = 16, 512, 2
N_MOD = 6
EPS = 1e-6
LRU_COLS = 2 * GW
HG_COLS = 5 * GW
SSD_COLS = GW + SSD_XBC + 2 * SSD_HEADS
MLA_COLS = MLA_QR + MLA_KVR + MLA_ROPE

LANES = 128
SUBLANES = 8
ROW_TILE = 256
N_TILES = NT // ROW_TILE
LAT_TILES = N_LAT // ROW_TILE
UA_W, UB_W, US_W, UM_W = 512, 1280, 896, 512
U_OFFS = (0, UA_W, UA_W + UB_W, UA_W + UB_W + US_W, UA_W + UB_W + US_W + UM_W)
SSD_CH = 128
HG_CH = 64
VMEM_LIMIT = 52 * 1024 * 1024
HEAD_SHIFT = 6
HB_SHIFT = 7
assert LRU_HD == HG_HD == SSD_HD == MLA_V == 1 << HEAD_SHIFT and LANES == 1 << HB_SHIFT


def _cparams(sem):
    return pltpu.CompilerParams(dimension_semantics=sem, vmem_limit_bytes=VMEM_LIMIT)


def _sigmoid(x):
    return 1.0 / (1.0 + jnp.exp(-x))


def _silu(x):
    return x * _sigmoid(x)


def _softplus(x):
    return jnp.maximum(x, 0.0) + jnp.log(1.0 + jnp.exp(-jnp.abs(x)))


def _gelu_tanh(x):
    return 0.5 * x * (1.0 + jnp.tanh(math.sqrt(2.0 / math.pi) * (x + 0.044715 * (x * x * x))))


def _dot(a, b, **kw):
    return jnp.dot(a, b, preferred_element_type=F32, **kw)


def _dot_nt(a, b, **kw):
    return lax.dot_general(a, b, (((1,), (1,)), ((), ())), preferred_element_type=F32, **kw)


def _dot_tn(a, b, **kw):
    return lax.dot_general(a, b, (((0,), (0,)), ((), ())), preferred_element_type=F32, **kw)


def _split_bf16(x, n):
    parts, rest = [], x
    for _ in range(n):
        p = rest.astype(BF16)
        parts.append(p)
        rest = rest - p.astype(F32)
    return parts


ADA_ROWS = 24
ADA_TN = 512


def _ada_kernel(c_ref, w_ref, b_ref, o_ref):
    s = _silu(c_ref[...])
    o_ref[0] = _dot(s, w_ref[0], precision=HI) + b_ref[0]


def _ada_call(cc, ada_w, ada_b):
    return pl.pallas_call(
        _ada_kernel,
        grid=(DEPTH, N_MOD * D // ADA_TN),
        in_specs=[
            pl.BlockSpec((ADA_ROWS, D), lambda l, j: (0, 0)),
            pl.BlockSpec((1, D, ADA_TN), lambda l, j: (l, 0, j)),
            pl.BlockSpec((1, 1, ADA_TN), lambda l, j: (l, 0, j)),
        ],
        out_specs=pl.BlockSpec((1, ADA_ROWS, ADA_TN), lambda l, j: (l, 0, j)),
        out_shape=jax.ShapeDtypeStruct((DEPTH, ADA_ROWS, N_MOD * D), F32),
        compiler_params=_cparams(("parallel", "parallel")),
        name="ada",
    )(cc, ada_w, ada_b.reshape(DEPTH, 1, N_MOD * D))


def _norm_mod(x, nw, shift, scale):
    ms = jnp.mean(x * x, axis=-1, keepdims=True)
    return (x * lax.rsqrt(ms + EPS) * nw) * (1.0 + scale) + shift


def _in_kernel(split, *refs):
    if split:
        x_ref, c_ref, mod_ref, nw_ref, w_ref, ua_ref, ub_ref, us_ref, um_ref = refs
        x = jnp.where(pl.program_id(1) < LAT_TILES, x_ref[0], c_ref[0])
    else:
        x_ref, mod_ref, nw_ref, w_ref, ua_ref, ub_ref, us_ref, um_ref = refs
        x = x_ref[0]
    h = _norm_mod(x, nw_ref[...], mod_ref[0, 0:1, :], mod_ref[0, 1:2, :]).astype(BF16)
    for k, ref in enumerate((ua_ref, ub_ref, us_ref, um_ref)):
        ref[0] = _dot(h, w_ref[:, U_OFFS[k]:U_OFFS[k + 1]])


def _mod_index(b, t, nb):
    return (jnp.where(t >= LAT_TILES, nb, b), 0, 0)


def _token_specs(ctx):
    if ctx is None:
        return [pl.BlockSpec((1, ROW_TILE, D), lambda b, t: (b, t, 0))]
    return [pl.BlockSpec((1, ROW_TILE, D), lambda b, t: (b, jnp.minimum(t, LAT_TILES - 1), 0)),
            pl.BlockSpec((1, N_CTX, D), lambda b, t: (b, 0, 0))]


def _in_call(x, ctx, modt, nw, w_all):
    nb = x.shape[0]
    in_specs = _token_specs(ctx) + [
        pl.BlockSpec((1, N_MOD, D), lambda b, t: _mod_index(b, t, nb)),
        pl.BlockSpec((1, D), lambda b, t: (0, 0)),
        pl.BlockSpec((D, U_OFFS[-1]), lambda b, t: (0, 0)),
    ]
    widths = (UA_W, UB_W, US_W, UM_W)
    out_specs = [pl.BlockSpec((1, ROW_TILE, w), lambda b, t: (b, t, 0)) for w in widths]
    out_shape = [jax.ShapeDtypeStruct((nb, NT, w), F32) for w in widths]
    args = ((x,) if ctx is None else (x, ctx)) + (modt, nw, w_all)
    return pl.pallas_call(
        functools.partial(_in_kernel, ctx is not None),
        grid=(nb, N_TILES),
        in_specs=in_specs,
        out_specs=out_specs,
        out_shape=out_shape,
        compiler_params=_cparams(("parallel", "parallel")),
        name="in_proj",
    )(*args)


def _conv_chunk(u_ref, r0, rows, c0, c1, cw_ref, cb_ref, at_seg_start, at_seg_end):
    x = u_ref[0, pl.ds(r0, rows), c0:c1]
    rp = pl.multiple_of(jnp.maximum(r0 - SUBLANES, 0), SUBLANES)
    rn = pl.multiple_of(jnp.minimum(r0 + rows, NT - SUBLANES), SUBLANES)
    xp = u_ref[0, pl.ds(rp, SUBLANES), c0:c1] * jnp.where(at_seg_start, 0.0, 1.0)
    xn = u_ref[0, pl.ds(rn, SUBLANES), c0:c1] * jnp.where(at_seg_end, 0.0, 1.0)
    xe = jnp.concatenate([xp, x, xn], axis=0)
    tot = rows + 2 * SUBLANES
    lo, hi = SUBLANES, SUBLANES + rows
    xm2 = pltpu.roll(xe, 2, 0)[lo:hi]
    xm1 = pltpu.roll(xe, 1, 0)[lo:hi]
    xp1 = pltpu.roll(xe, tot - 1, 0)[lo:hi]
    return cw_ref[0:1, :] * xm2 + cw_ref[1:2, :] * xm1 + cw_ref[2:3, :] * x + cw_ref[3:4, :] * xp1 + cb_ref[...]


LRU_CH = 256
LRU_NCH = NT // LRU_CH
LRU_LAT_CH = N_LAT // LRU_CH


def _scan_chunk(a, b, row_in_tile, hprev, reverse):
    n = a.shape[0]
    for s in (1, 2, 4):
        if reverse:
            a_s = pltpu.roll(a, n - s, 0)
            b_s = pltpu.roll(b, n - s, 0)
            valid = row_in_tile < SUBLANES - s
        else:
            a_s = pltpu.roll(a, s, 0)
            b_s = pltpu.roll(b, s, 0)
            valid = row_in_tile >= s
        b = b + a * jnp.where(valid, b_s, 0.0)
        a = a * jnp.where(valid, a_s, 1.0)
    tiles = n // SUBLANES
    hs = [None] * tiles
    for j in (reversed(range(tiles)) if reverse else range(tiles)):
        lo = j * SUBLANES
        hj = b[lo:lo + SUBLANES] + a[lo:lo + SUBLANES] * hprev
        hprev = hj[0:1] if reverse else hj[SUBLANES - 1:SUBLANES]
        hs[j] = hj
    return jnp.concatenate(hs, axis=0), hprev


def _lru_kernel(u_ref, cw_ref, cb_ref, wri_ref, bri_ref, lam_ref, y_ref, ab_s, h_s):
    row_in_tile = lax.broadcasted_iota(jnp.int32, (LRU_CH, GW), 0) & (SUBLANES - 1)
    sp = _softplus(-lam_ref[...])

    def prepare(c, carry):
        r0 = pl.multiple_of(c * LRU_CH, LRU_CH)
        is_ctx = c == LRU_LAT_CH
        xc = _conv_chunk(u_ref, r0, LRU_CH, 0, GW, cw_ref, cb_ref, (c == 0) | is_ctx, (c == LRU_LAT_CH - 1) | is_ctx)
        g = _dot(xc.astype(BF16), wri_ref[...]) + bri_ref[...]
        for d in (0, 1):
            r = _sigmoid(g[:, 2 * d * GW:(2 * d + 1) * GW])
            ig = _sigmoid(g[:, (2 * d + 1) * GW:(2 * d + 2) * GW])
            a = jnp.exp(-LRU_C * r * sp[d:d + 1, :])
            ab_s[2 * d, pl.ds(r0, LRU_CH), :] = a
            ab_s[2 * d + 1, pl.ds(r0, LRU_CH), :] = jnp.sqrt(1.0 - a * a) * (ig * xc)
        return carry

    lax.fori_loop(0, LRU_NCH, prepare, 0)

    def body(i, carry):
        hf, hb = carry
        rf = pl.multiple_of(((i + LRU_LAT_CH) % LRU_NCH) * LRU_CH, LRU_CH)
        rb = pl.multiple_of((LRU_LAT_CH - i) * LRU_CH, LRU_CH)
        h, hf = _scan_chunk(ab_s[0, pl.ds(rf, LRU_CH), :], ab_s[1, pl.ds(rf, LRU_CH), :], row_in_tile, hf, False)
        h_s[0, pl.ds(rf, LRU_CH), :] = h
        h, hb = _scan_chunk(ab_s[2, pl.ds(rb, LRU_CH), :], ab_s[3, pl.ds(rb, LRU_CH), :], row_in_tile, hb, True)
        h_s[1, pl.ds(rb, LRU_CH), :] = h
        return hf, hb

    zero = jnp.zeros((1, GW), F32)
    lax.fori_loop(0, LRU_NCH, body, (zero, zero))

    def finish(c, carry):
        r0 = pl.multiple_of(c * LRU_CH, LRU_CH)
        gate = u_ref[0, pl.ds(r0, LRU_CH), GW:2 * GW]
        hsum = h_s[0, pl.ds(r0, LRU_CH), :] + h_s[1, pl.ds(r0, LRU_CH), :]
        y_ref[0, pl.ds(r0, LRU_CH), :] = (hsum * _gelu_tanh(gate)).astype(BF16)
        return carry

    lax.fori_loop(0, LRU_NCH, finish, 0)


def _lru_call(ua, cw, cb, wri, bri, lam):
    nb = ua.shape[0]
    full = lambda shape: pl.BlockSpec(shape, lambda b: (0,) * len(shape))
    return pl.pallas_call(
        _lru_kernel,
        grid=(nb,),
        in_specs=[
            pl.BlockSpec((1, NT, UA_W), lambda b: (b, 0, 0)),
            full((CONV_W, GW)), full((1, GW)), full((GW, 4 * GW)), full((1, 4 * GW)), full((2, GW)),
        ],
        out_specs=pl.BlockSpec((1, NT, GW), lambda b: (b, 0, 0)),
        out_shape=jax.ShapeDtypeStruct((nb, NT, GW), BF16),
        scratch_shapes=[pltpu.VMEM((4, NT, GW), F32), pltpu.VMEM((2, NT, GW), F32)],
        compiler_params=_cparams(("parallel",)),
        name="rglru",
    )(ua, cw, cb, wri, bri, lam)


HG_NCH = NT // HG_CH
HG_LAT_CH = N_LAT // HG_CH
HG_LEVELS = HG_CH.bit_length() - 1
HG_EXP_BLOCKS = 2 + 2 * HG_LEVELS
HG_SPLIT = 3


def _hgrn_tables():
    t_ = HG_CH
    dm = np.zeros((2, HG_EXP_BLOCKS, t_, t_), np.float32)
    lm = np.zeros((2, HG_LEVELS + 1, t_, t_), np.float32)
    for d in (0, 1):
        for t in range(t_):
            if d == 0:
                dm[d, 0, t, :t + 1] = 1.0
                dm[d, 1, t, t + 1:] = 1.0
            else:
                dm[d, 0, t, t:] = 1.0
                dm[d, 1, t, :t] = 1.0
            lm[d, 0, t, t] = 1.0
        for lev in range(1, HG_LEVELS + 1):
            m = 1 << (lev - 1)
            for t in range(t_):
                start = (t // (2 * m)) * 2 * m
                mid = start + m
                upper = t >= mid
                if d == 0:
                    if upper:
                        dm[d, 2 * lev, t, mid:t + 1] = 1.0
                        lm[d, lev, t, start:mid] = 1.0
                    else:
                        dm[d, 2 * lev + 1, t, t + 1:mid] = 1.0
                else:
                    if not upper:
                        dm[d, 2 * lev, t, t:mid] = 1.0
                        lm[d, lev, t, mid:start + 2 * m] = 1.0
                    else:
                        dm[d, 2 * lev + 1, t, mid:t] = 1.0
    dm = dm.reshape(2, HG_EXP_BLOCKS * t_, t_)
    d3 = np.concatenate([dm] * HG_SPLIT, axis=-1)
    lm = np.tile(lm, (1, 1, 1, HG_HEADS))
    heads = np.arange(GW) // HG_HD
    bm = (heads[:, None] == heads[None, :]).astype(np.float32)
    return jnp.asarray(d3, BF16), jnp.asarray(lm, F32), jnp.asarray(bm, F32), jnp.asarray(bm, BF16)


def _hgrn_kernel(u_ref, lb_ref, nw_ref, d3_ref, lm_ref, bm_ref, bmb_ref, y_ref, o_s, st_s, stb_s):
    t = HG_CH
    lb = lb_ref[...]
    st_s[...] = jnp.zeros(st_s.shape, F32)
    stb_s[...] = jnp.zeros(stb_s.shape, BF16)

    def by_head(xb):
        return jnp.concatenate([xb] * HG_HEADS, axis=0) * bmb_ref[...]

    def chunk(d, c):
        r0 = pl.multiple_of(c * t, t)
        q = _silu(u_ref[0, pl.ds(r0, t), 0:GW])
        fr = u_ref[0, pl.ds(r0, t), (1 + d) * GW:(2 + d) * GW]
        v = u_ref[0, pl.ds(r0, t), 3 * GW:4 * GW]
        f = lb + (1.0 - lb) * _sigmoid(fr)
        k = 1.0 - f
        ex = _dot(d3_ref[d], jnp.concatenate(_split_bf16(jnp.log2(f), HG_SPLIT), axis=0))
        cum = ex[0:t]
        to_end = ex[t:2 * t]
        a = _dot_nt(q.astype(BF16), by_head(k.astype(BF16))) * lm_ref[d, 0]
        for lev in range(1, HG_LEVELS + 1):
            eq = ex[2 * lev * t:(2 * lev + 1) * t]
            ek = ex[(2 * lev + 1) * t:(2 * lev + 2) * t]
            qt = (q * jnp.exp2(eq)).astype(BF16)
            kt = (k * jnp.exp2(ek)).astype(BF16)
            a = a + _dot_nt(qt, by_head(kt)) * lm_ref[d, lev]
        vb = v.astype(BF16)
        o = _dot(a.astype(BF16), by_head(vb)) + _dot_nt((q * jnp.exp2(cum)).astype(BF16), stb_s[d])
        o_s[d, pl.ds(r0, t), :] = o
        cend = cum[t - 1:t, :] if d == 0 else cum[0:1, :]
        kend = (k * jnp.exp2(to_end)).astype(BF16)
        st = st_s[d] * jnp.exp2(cend) + _dot_tn(vb, kend) * bm_ref[...]
        st_s[d] = st
        stb_s[d] = st.astype(BF16)

    def body(i, carry):
        chunk(0, (i + HG_LAT_CH) % HG_NCH)
        chunk(1, HG_NCH - 1 - i)
        return carry

    lax.fori_loop(0, HG_NCH, body, 0)

    def finish(c, carry):
        r0 = pl.multiple_of(c * t, t)
        osum = o_s[0, pl.ds(r0, t), :] + o_s[1, pl.ds(r0, t), :]
        sq = _split_bf16(osum * osum, 2)
        ms = _dot(jnp.concatenate(sq, axis=1), jnp.concatenate([bmb_ref[...]] * 2, axis=0)) * (1.0 / HG_HD)
        g = u_ref[0, pl.ds(r0, t), 4 * GW:5 * GW]
        y_ref[0, pl.ds(r0, t), :] = (osum * lax.rsqrt(ms + EPS) * nw_ref[...] * _silu(g)).astype(BF16)
        return carry

    lax.fori_loop(0, HG_NCH, finish, 0)


def _hgrn_call(ub, lb, nw, tables):
    nb = ub.shape[0]
    d3, lm, bm, bmb = tables
    full = lambda shape: pl.BlockSpec(shape, lambda b: (0,) * len(shape))
    return pl.pallas_call(
        _hgrn_kernel,
        grid=(nb,),
        in_specs=[pl.BlockSpec((1, NT, UB_W), lambda b: (b, 0, 0)), full((1, GW)), full((1, GW)),
                  full(d3.shape), full(lm.shape), full(bm.shape), full(bmb.shape)],
        out_specs=pl.BlockSpec((1, NT, GW), lambda b: (b, 0, 0)),
        out_shape=jax.ShapeDtypeStruct((nb, NT, GW), BF16),
        scratch_shapes=[pltpu.VMEM((2, NT, GW), F32), pltpu.VMEM((2, GW, GW), F32), pltpu.VMEM((2, GW, GW), BF16)],
        compiler_params=_cparams(("parallel",)),
        name="hgrn2",
    )(ub, lb, nw, d3, lm, bm, bmb)


SSD_NCH = NT // SSD_CH
SSD_LAT_CH = N_LAT // SSD_CH
SSD_DT0 = GW + SSD_XBC
GRP_W = SSD_HEADS // SSD_GROUPS * SSD_HD


SSD_BC_W = 2 * SSD_GROUPS * SSD_STATE
SSD_CUM_SPLIT = 3
SSD_GAIN_SPLIT = 2


def _ssd_tables():
    t = SSD_CH
    tri = np.stack([np.tril(np.ones((t, t), np.float32)), np.triu(np.ones((t, t), np.float32))])
    col = np.zeros((2, LANES, SSD_HEADS * LANES), np.float32)
    wide = np.zeros((2, LANES, GW), np.float32)
    for d in range(2):
        for h in range(SSD_HEADS):
            col[d, d * SSD_HEADS + h, h * LANES:(h + 1) * LANES] = 1.0
            wide[d, d * SSD_HEADS + h, h * SSD_HD:(h + 1) * SSD_HD] = 1.0
    hm = (np.arange(GW)[None, :] // SSD_HD == np.arange(SSD_HEADS)[:, None]).astype(np.float32)
    gm = (np.arange(LANES)[None, :] // SSD_STATE == np.arange(SSD_GROUPS)[:, None]).astype(np.float32)
    return (jnp.asarray(np.concatenate([tri] * SSD_CUM_SPLIT, axis=2), BF16),
            jnp.asarray(np.concatenate([col] * SSD_CUM_SPLIT, axis=1), BF16),
            jnp.asarray(np.concatenate([wide] * SSD_GAIN_SPLIT, axis=1), BF16),
            jnp.asarray(hm[:, None, :], BF16), jnp.asarray(gm[:, None, :], BF16))


def _ssd_kernel(u_ref, cw_ref, cb_ref, a8_ref, b8_ref, dsk_ref, nw_ref, tri_ref, col_ref, wide_ref, hm_ref, gm_ref,
                y_ref, xs_s, bc_s, xm_s, dt_s, y_s, st_s):
    t = SSD_CH
    rr = lax.broadcasted_iota(jnp.int32, (t, t), 0)
    cc = lax.broadcasted_iota(jnp.int32, (t, t), 1)
    keeps = (rr >= cc, rr <= cc)
    a8 = -jnp.exp(a8_ref[...])

    def prepare(c, carry):
        r0 = pl.multiple_of(c * t, t)
        seg_start = (c == 0) | (c == SSD_LAT_CH)
        seg_end = (c == SSD_LAT_CH - 1) | (c == SSD_NCH - 1)
        xbc = _silu(_conv_chunk(u_ref, r0, t, GW, GW + SSD_XBC, cw_ref, cb_ref, seg_start, seg_end))
        xs = xbc[:, 0:GW]
        xs_s[pl.ds(r0, t), :] = xs
        bc_s[pl.ds(r0, t), :] = xbc[:, GW:GW + SSD_BC_W].astype(BF16)
        xb = xs.astype(BF16)
        for h in range(SSD_HEADS):
            xm_s[h, pl.ds(r0, t), :] = xb * hm_ref[h]
        dt_s[pl.ds(r0, t), :] = _softplus(u_ref[0, pl.ds(r0, t), SSD_DT0:SSD_DT0 + LANES] + b8_ref[...])
        return carry

    lax.fori_loop(0, SSD_NCH, prepare, 0)
    st_s[...] = jnp.zeros(st_s.shape, F32)

    def chunk(d, c):
        r0 = pl.multiple_of(c * t, t)
        dtn = dt_s[pl.ds(r0, t), :]
        cumn = _dot(tri_ref[d], jnp.concatenate(_split_bf16(dtn * a8, SSD_CUM_SPLIT), axis=0))
        cend = cumn[t - 1:t, :] if d == 0 else cumn[0:1, :]
        cum_t = cumn.T
        dt_t = dtn.T
        colb = _dot(jnp.concatenate(_split_bf16(cumn, SSD_CUM_SPLIT), axis=1), col_ref[d])
        gains = jnp.concatenate([jnp.exp(cumn), dtn * jnp.exp(cend - cumn)], axis=0)
        gw = _dot(jnp.concatenate(_split_bf16(gains, SSD_GAIN_SPLIT), axis=1), wide_ref[d])
        ecum, wend = gw[0:t], gw[t:2 * t]
        bc = bc_s[pl.ds(r0, t), :]
        bmat, cmat = bc[:, 0:LANES], bc[:, LANES:2 * LANES]
        st = st_s[d]
        y = jnp.zeros((t, GW), F32)
        y_in = []
        for g in range(SSD_GROUPS):
            cg = cmat * gm_ref[g]
            scores = _dot_nt(cg, bmat)
            sg = st[:, g * GRP_W:(g + 1) * GRP_W].astype(BF16)
            y_in.append(_dot(cg, jnp.concatenate([sg] * SSD_GROUPS, axis=0)))
            for hh in range(SSD_HEADS // SSD_GROUPS):
                h = g * (SSD_HEADS // SSD_GROUPS) + hh
                row = d * SSD_HEADS + h
                seg = colb[:, h * LANES:(h + 1) * LANES] - cum_t[row:row + 1, :]
                m = jnp.where(keeps[d], jnp.exp(seg), 0.0) * scores * dt_t[row:row + 1, :]
                y = y + _dot(m.astype(BF16), xm_s[h, pl.ds(r0, t), :])
        y_s[d, pl.ds(r0, t), :] = y + jnp.concatenate(y_in, axis=1) * ecum
        xw = (xs_s[pl.ds(r0, t), :] * wend).astype(BF16)
        upd = [_dot_tn(bmat, xw[:, g * GRP_W:(g + 1) * GRP_W])[g * SSD_STATE:(g + 1) * SSD_STATE]
               for g in range(SSD_GROUPS)]
        eend = ecum[t - 1:t, :] if d == 0 else ecum[0:1, :]
        st_s[d] = st * eend + jnp.concatenate(upd, axis=1)

    def body(i, carry):
        chunk(0, (i + SSD_LAT_CH) % SSD_NCH)
        chunk(1, SSD_NCH - 1 - i)
        return carry

    lax.fori_loop(0, SSD_NCH, body, 0)

    def finish(c, carry):
        r0 = pl.multiple_of(c * t, t)
        z = u_ref[0, pl.ds(r0, t), 0:GW]
        yy = (y_s[0, pl.ds(r0, t), :] + y_s[1, pl.ds(r0, t), :] + dsk_ref[...] * xs_s[pl.ds(r0, t), :]) * _silu(z)
        ms = jnp.mean(yy * yy, axis=-1, keepdims=True)
        y_ref[0, pl.ds(r0, t), :] = (yy * lax.rsqrt(ms + EPS) * nw_ref[...]).astype(BF16)
        return carry

    lax.fori_loop(0, SSD_NCH, finish, 0)


def _ssd_call(us, cw, cb, a8, b8, dsk, nw, tables):
    nb = us.shape[0]
    full = lambda shape: pl.BlockSpec(shape, lambda b: (0,) * len(shape))
    return pl.pallas_call(
        _ssd_kernel,
        grid=(nb,),
        in_specs=[
            pl.BlockSpec((1, NT, US_W), lambda b: (b, 0, 0)),
            full((CONV_W, SSD_XBC)), full((1, SSD_XBC)), full((1, LANES)), full((1, LANES)),
            full((1, GW)), full((1, GW)),
        ] + [full(tb.shape) for tb in tables],
        out_specs=pl.BlockSpec((1, NT, GW), lambda b: (b, 0, 0)),
        out_shape=jax.ShapeDtypeStruct((nb, NT, GW), BF16),
        scratch_shapes=[pltpu.VMEM((NT, GW), F32), pltpu.VMEM((NT, SSD_BC_W), BF16),
                        pltpu.VMEM((SSD_HEADS, NT, GW), BF16), pltpu.VMEM((NT, LANES), F32),
                        pltpu.VMEM((2, NT, GW), F32), pltpu.VMEM((2, SSD_STATE, GW), F32)],
        compiler_params=_cparams(("parallel",)),
        name="ssd",
    )(us, cw, cb, a8, b8, dsk, nw, *tables)


HB = LANES
QW = MLA_HEADS * HB
MLA_QB = 256
MLA_AB = 256
MLA_SCALE = (MLA_NOPE + MLA_ROPE) ** -0.5


def _mla_kernel(need_ctx, u_ref, qa_ref, wq_ref, qn_ref, kva_ref, wkv_ref, kn_ref, krw_ref,
                cos_ref, sa_ref, sb_ref, y_ref, q_s, k_s, v_s):
    lane = lax.broadcasted_iota(jnp.int32, (QW, QW), 1)
    row = lax.broadcasted_iota(jnp.int32, (QW, QW), 0)

    def grp(i):
        within = i & (HB - 1)
        return (i >> HB_SHIFT) * 3 + jnp.where(within < MLA_NOPE, 0, jnp.where(within < MLA_NOPE + MLA_ROPE, 1, 2))

    gmat = jnp.where(grp(row) == grp(lane), 1.0, 0.0).astype(BF16)
    l1 = lax.broadcasted_iota(jnp.int32, (1, QW), 1) & (HB - 1)
    inv_size = jnp.where(l1 < MLA_NOPE, 1.0 / MLA_NOPE, 1.0 / MLA_ROPE)
    vlane_head = lax.broadcasted_iota(jnp.int32, (MLA_QB, GW), 1) >> HEAD_SHIFT

    def rope(x, r0, reps):
        cosr = cos_ref[pl.ds(r0, MLA_QB), :]
        sar = sa_ref[pl.ds(r0, MLA_QB), :]
        sbr = sb_ref[pl.ds(r0, MLA_QB), :]
        if reps > 1:
            cosr, sar, sbr = (jnp.concatenate([tbl] * reps, axis=1) for tbl in (cosr, sar, sbr))
        w = x.shape[1]
        return x * cosr + pltpu.roll(x, w - MLA_ROPE // 2, 1) * sar + pltpu.roll(x, MLA_ROPE // 2, 1) * sbr

    def project(ci, carry):
        r0 = pl.multiple_of(ci * MLA_QB, MLA_QB)
        cq = u_ref[0, pl.ds(r0, MLA_QB), 0:256]
        ms = jnp.sum(cq * cq, axis=-1, keepdims=True) * (1.0 / MLA_QR)
        qraw = _dot((cq * lax.rsqrt(ms + EPS) * qa_ref[...]).astype(BF16), wq_ref[...])
        ss = _dot((qraw * qraw).astype(BF16), gmat) * inv_size
        q = rope(qraw * lax.rsqrt(ss + EPS) * qn_ref[...], r0, MLA_HEADS) * MLA_SCALE
        q_s[pl.ds(r0, MLA_QB), :] = q.astype(BF16)
        ckv = u_ref[0, pl.ds(r0, MLA_QB), 256:256 + MLA_KVR]
        ms = jnp.mean(ckv * ckv, axis=-1, keepdims=True)
        kv = _dot((ckv * lax.rsqrt(ms + EPS) * kva_ref[...]).astype(BF16), wkv_ref[...])
        kraw = kv[:, 0:QW]
        ss = _dot((kraw * kraw).astype(BF16), gmat) * inv_size
        knope = kraw * lax.rsqrt(ss + EPS) * kn_ref[...]
        kr = u_ref[0, pl.ds(r0, MLA_QB), 384:512]
        ms = jnp.sum(kr * kr, axis=-1, keepdims=True) * (1.0 / MLA_ROPE)
        krope = rope(kr * lax.rsqrt(ms + EPS) * krw_ref[...], r0, 1)
        k_s[pl.ds(r0, MLA_QB), :] = (knope + jnp.concatenate([krope] * MLA_HEADS, axis=1)).astype(BF16)
        vv = kv[:, QW:QW + GW]
        for h in range(MLA_HEADS):
            v_s[h, pl.ds(r0, MLA_QB), :] = jnp.where(vlane_head == h, vv, 0.0).astype(BF16)
        return carry

    lax.fori_loop(0, NT // MLA_QB, project, 0)

    def attend(r0, rows, k0, klen):
        o = jnp.zeros((rows, GW), F32)
        for h in range(MLA_HEADS):
            qh = q_s[pl.ds(r0, rows), h * HB:(h + 1) * HB]
            s = _dot_nt(qh, k_s[k0:k0 + klen, h * HB:(h + 1) * HB])
            p = jnp.exp(s - jnp.max(s, axis=-1, keepdims=True))
            l = jnp.sum(p, axis=-1, keepdims=True)
            o = o + _dot(p.astype(BF16), v_s[h, k0:k0 + klen, :]) * (1.0 / l)
        y_ref[0, pl.ds(r0, rows), :] = o.astype(BF16)

    def lat_block(qi, carry):
        attend(pl.multiple_of(qi * MLA_AB, MLA_AB), MLA_AB, 0, NT)
        return carry

    lax.fori_loop(0, N_LAT // MLA_AB, lat_block, 0)
    if need_ctx:
        attend(N_LAT, N_CTX, N_LAT, N_CTX)
    else:
        y_ref[0, N_LAT:NT, :] = jnp.zeros((N_CTX, GW), BF16)


def _mla_call(um, need_ctx, qa, wq, qn, kva, wkv, kn, krw, cos_t, sa_t, sb_t):
    nb = um.shape[0]
    full = lambda shape: pl.BlockSpec(shape, lambda b: (0,) * len(shape))
    return pl.pallas_call(
        functools.partial(_mla_kernel, need_ctx),
        grid=(nb,),
        in_specs=[
            pl.BlockSpec((1, NT, UM_W), lambda b: (b, 0, 0)),
            full((1, 256)), full((256, QW)), full((1, QW)),
            full((1, MLA_KVR)), full((MLA_KVR, QW + GW)), full((1, QW)), full((1, HB)),
            full((NT, HB)), full((NT, HB)), full((NT, HB)),
        ],
        out_specs=pl.BlockSpec((1, NT, GW), lambda b: (b, 0, 0)),
        out_shape=jax.ShapeDtypeStruct((nb, NT, GW), BF16),
        scratch_shapes=[pltpu.VMEM((NT, QW), BF16), pltpu.VMEM((NT, QW), BF16),
                        pltpu.VMEM((MLA_HEADS, NT, GW), BF16)],
        compiler_params=_cparams(("parallel",)),
        name="mla",
    )(um, qa, wq, qn, kva, wkv, kn, krw, cos_t, sa_t, sb_t)


def _out_kernel(split, *refs):
    if split:
        x_ref, c_ref, ya_ref, yb_ref, ys_ref, ym_ref, wo_ref, mod_ref, nw_ref, wr_ref, xo_ref, h2_ref, aff_ref = refs
        x_in = jnp.where(pl.program_id(1) < LAT_TILES, x_ref[0], c_ref[0])
    else:
        x_ref, ya_ref, yb_ref, ys_ref, ym_ref, wo_ref, mod_ref, nw_ref, wr_ref, xo_ref, h2_ref, aff_ref = refs
        x_in = x_ref[0]
    y = jnp.concatenate([ya_ref[0], yb_ref[0], ys_ref[0], ym_ref[0]], axis=1)
    x = x_in + mod_ref[0, 2:3, :] * _dot(y, wo_ref[...])
    xo_ref[0] = x
    h2 = _norm_mod(x, nw_ref[...], mod_ref[0, 3:4, :], mod_ref[0, 4:5, :])
    hi = h2.astype(BF16)
    h2_ref[0] = hi
    lo = (h2 - hi.astype(F32)).astype(BF16)
    wr = wr_ref[...]
    w_hi = wr.astype(BF16)
    w_lo = (wr - w_hi.astype(F32)).astype(BF16)
    logit = _dot_nt(jnp.concatenate([w_hi, w_hi, w_lo], axis=1), jnp.concatenate([hi, lo, hi], axis=1))
    e = jnp.exp(logit - jnp.max(logit, axis=0, keepdims=True))
    aff_ref[0] = e / jnp.sum(e, axis=0, keepdims=True)


def _out_call(x, ctx, ys, wo, modt, nw, wr_t, n_tiles):
    nb = x.shape[0]
    tok = lambda w: pl.BlockSpec((1, ROW_TILE, w), lambda b, t: (b, t, 0))
    x_args = (x,) if ctx is None else (x, ctx)
    return pl.pallas_call(
        functools.partial(_out_kernel, ctx is not None),
        grid=(nb, n_tiles),
        in_specs=_token_specs(ctx) + [tok(GW), tok(GW), tok(GW), tok(GW),
                  pl.BlockSpec((D, D), lambda b, t: (0, 0)),
                  pl.BlockSpec((1, N_MOD, D), lambda b, t: _mod_index(b, t, nb)),
                  pl.BlockSpec((1, D), lambda b, t: (0, 0)),
                  pl.BlockSpec((N_EXP, D), lambda b, t: (0, 0))],
        out_specs=[tok(D), tok(D), pl.BlockSpec((1, N_EXP, ROW_TILE), lambda b, t: (b, 0, t))],
        out_shape=[jax.ShapeDtypeStruct((nb, NT, D), F32),
                   jax.ShapeDtypeStruct((nb, NT, D), BF16),
                   jax.ShapeDtypeStruct((nb, N_EXP, NT), F32)],
        compiler_params=_cparams(("parallel", "parallel")),
        name="out_proj",
    )(*x_args, *ys, wo, modt, nw, wr_t)


PRE_W = 256
ROUTE_MAX_IT = 160


def _prefix_count(m, tri_bf):
    n = m.shape[1]
    off = jnp.zeros((m.shape[0], 1), F32)
    outs = []
    for j in range(n // PRE_W):
        blk = m[:, j * PRE_W:(j + 1) * PRE_W]
        outs.append(_dot(blk.astype(BF16), tri_bf) + off)
        off = off + jnp.sum(blk, axis=1, keepdims=True)
    return outs[0] if len(outs) == 1 else jnp.concatenate(outs, axis=1)


def _route(aff, cap, tri):
    n_lo0 = jnp.full((N_EXP, 1), float(aff.shape[1]), F32)

    def cond(s):
        it, _, _, n_lo, n_hi = s
        return jnp.logical_and(it < ROUTE_MAX_IT, jnp.max(n_lo - n_hi) > 1.0)

    def step(s):
        it, lo, hi, n_lo, n_hi = s
        mid = 0.5 * (lo + hi)
        cnt = jnp.sum(jnp.where(aff > mid, 1.0, 0.0), axis=1, keepdims=True)
        below = jnp.max(jnp.where(aff <= mid, aff, -1.0), axis=1, keepdims=True)
        up = cnt >= cap
        return (it + 1, jnp.where(up, mid, lo), jnp.where(up, hi, below),
                jnp.where(up, cnt, n_lo), jnp.where(up, n_hi, cnt))

    init = (jnp.int32(0), jnp.full((N_EXP, 1), -1.0, F32), jnp.max(aff, axis=1, keepdims=True),
            n_lo0, jnp.zeros((N_EXP, 1), F32))
    _, _, thr, _, n_gt = lax.while_loop(cond, step, init)
    gt = jnp.where(aff > thr, 1.0, 0.0)
    eq = jnp.where(aff == thr, 1.0, 0.0)
    sel = gt + eq * jnp.where(_prefix_count(eq, tri) < cap - n_gt, 1.0, 0.0)
    return jnp.where(sel > 0.0, _prefix_count(sel, tri), -1.0)


def _moe_kernel(segs, x_ref, aff_ref, h_ref, gl_ref, gc_ref, wg_ref, wu_ref, wd_ref, o_ref, pos_ref):
    e = pl.program_id(1)

    @pl.when(e == 0)
    def _init():
        tri = jnp.where(lax.broadcasted_iota(jnp.int32, (PRE_W, PRE_W), 0)
                        < lax.broadcasted_iota(jnp.int32, (PRE_W, PRE_W), 1), 1.0, 0.0).astype(BF16)
        for r0, n, cap in segs:
            pos_ref[:, r0:r0 + n] = _route(aff_ref[0, :, r0:r0 + n], cap, tri)
        o_ref[0] = x_ref[0]

    onehots, gates, xs = [], [], []
    for (r0, n, cap), g_ref in zip(segs, (gl_ref, gc_ref)):
        posrow = pos_ref[pl.ds(e, 1), r0:r0 + n]
        affrow = aff_ref[0, pl.ds(e, 1), r0:r0 + n]
        hit = lax.broadcasted_iota(jnp.int32, (cap, n), 0).astype(F32) == posrow
        onehot = jnp.where(hit, 1.0, 0.0).astype(BF16)
        onehots.append(onehot)
        gates.append(jnp.sum(jnp.where(hit, affrow, 0.0), axis=1, keepdims=True) * g_ref[0, N_MOD - 1:N_MOD, :])
        xs.append(_dot(onehot, h_ref[0, r0:r0 + n, :]).astype(BF16))
    xs = xs[0] if len(xs) == 1 else jnp.concatenate(xs, axis=0)
    act = (_silu(_dot(xs, wg_ref[0])) * _dot(xs, wu_ref[0])).astype(BF16)
    ys = _dot(act, wd_ref[0])
    s0 = 0
    for (r0, n, cap), onehot, gate in zip(segs, onehots, gates):
        o_ref[0, r0:r0 + n, :] += _dot_tn(onehot, (ys[s0:s0 + cap] * gate).astype(BF16))
        s0 += cap


def _moe_call(x_mid, aff, h2, modt, wg, wu, wd, with_ctx):
    nb = h2.shape[0]
    segs = ((0, N_LAT, EC_FACTOR * N_LAT // N_EXP),)
    if with_ctx:
        segs += ((N_LAT, N_CTX, EC_FACTOR * N_CTX // N_EXP),)
    rows = NT if with_ctx else N_LAT
    return pl.pallas_call(
        functools.partial(_moe_kernel, segs),
        grid=(nb, N_EXP),
        in_specs=[
            pl.BlockSpec((1, rows, D), lambda b, e: (b, 0, 0), pipeline_mode=pl.Buffered(1)),
            pl.BlockSpec((1, N_EXP, rows), lambda b, e: (b, 0, 0)),
            pl.BlockSpec((1, rows, D), lambda b, e: (b, 0, 0)),
            pl.BlockSpec((1, N_MOD, D), lambda b, e: (b, 0, 0)),
            pl.BlockSpec((1, N_MOD, D), lambda b, e: (nb, 0, 0)),
            pl.BlockSpec((1, D, FF), lambda b, e: (e, 0, 0)),
            pl.BlockSpec((1, D, FF), lambda b, e: (e, 0, 0)),
            pl.BlockSpec((1, FF, D), lambda b, e: (e, 0, 0)),
        ],
        out_specs=pl.BlockSpec((1, rows, D), lambda b, e: (b, 0, 0)),
        out_shape=jax.ShapeDtypeStruct((nb, rows, D), F32),
        scratch_shapes=[pltpu.VMEM((N_EXP, rows), F32)],
        compiler_params=_cparams(("parallel", "arbitrary")),
        name="moe",
    )(x_mid, aff, h2, modt, modt, wg, wu, wd)


def _pad_cols(w, width):
    return jnp.pad(w, ((0, 0), (0, width - w.shape[1])))


def _rope_perm():
    half = MLA_ROPE // 2
    return jnp.concatenate([jnp.arange(half) * 2, jnp.arange(half) * 2 + 1])


def _head_block(nope, rope):
    pad = jnp.zeros(nope.shape[:-1] + (HB - MLA_NOPE - MLA_ROPE,), nope.dtype)
    blk = jnp.concatenate([nope, rope, pad], axis=-1)
    return blk.reshape(blk.shape[:-2] + (MLA_HEADS * HB,))


def _prep_in_weights(w_in):
    o1, o2, o3 = LRU_COLS, LRU_COLS + HG_COLS, LRU_COLS + HG_COLS + SSD_COLS
    wa, wb, ws, wm = w_in[:, :o1], w_in[:, o1:o2], w_in[:, o2:o3], w_in[:, o3:]
    perm = _rope_perm()
    cq, ckv, kr = wm[:, :MLA_QR], wm[:, MLA_QR:MLA_QR + MLA_KVR], wm[:, MLA_QR + MLA_KVR:]
    zeros = lambda n: jnp.zeros((D, n), w_in.dtype)
    wm_p = jnp.concatenate([cq, zeros(256 - MLA_QR), ckv, zeros(MLA_NOPE), kr[:, perm],
                            zeros(HB - MLA_NOPE - MLA_ROPE)], axis=1)
    return jnp.concatenate([wa, wb, _pad_cols(ws, US_W), wm_p], axis=1).astype(BF16)


def _block_diag(w):
    h, dd, _ = w.shape
    eye = jnp.eye(h, dtype=w.dtype)
    return (eye[:, None, :, None] * w[:, :, None, :]).reshape(h * dd, h * dd)


def _rope_tables():
    rows = N_LAT // GRID_W
    row = jnp.repeat(jnp.arange(rows, dtype=F32), GRID_W)
    col = jnp.tile(jnp.arange(GRID_W, dtype=F32), rows)
    half = MLA_ROPE // 2
    inv = ROPE_BASE ** (-jnp.arange(0, half, 2, dtype=F32) / half)
    ang = jnp.concatenate([row[:, None] * inv, col[:, None] * inv], axis=-1)
    cos, sin = jnp.cos(ang), jnp.sin(ang)
    z = lambda n: jnp.zeros((N_LAT, n), F32)
    o = lambda n: jnp.ones((N_LAT, n), F32)
    cos_t = jnp.concatenate([o(MLA_NOPE), cos, cos, o(HB - MLA_NOPE - MLA_ROPE)], axis=1)
    sa_t = jnp.concatenate([z(MLA_NOPE), -sin, z(half), z(HB - MLA_NOPE - MLA_ROPE)], axis=1)
    sb_t = jnp.concatenate([z(MLA_NOPE), z(half), sin, z(HB - MLA_NOPE - MLA_ROPE)], axis=1)
    cos_t = jnp.concatenate([cos_t, jnp.ones((N_CTX, HB), F32)], axis=0)
    sa_t = jnp.concatenate([sa_t, jnp.zeros((N_CTX, HB), F32)], axis=0)
    sb_t = jnp.concatenate([sb_t, jnp.zeros((N_CTX, HB), F32)], axis=0)
    return cos_t, sa_t, sb_t


def kernel(x, c, ctx, c_ctx, ada_w, ada_b, norm1_w, norm2_w, w_in, w_out, lru_conv_w, lru_conv_b, lru_w_r, lru_b_r, lru_w_i, lru_b_i, lru_lam, hgrn_lb_logits, hgrn_norm_w, ssd_conv_w, ssd_conv_b, ssd_a_log, ssd_dt_bias, ssd_d_skip, ssd_norm_w, mla_q_a_norm, mla_w_q_up, mla_kv_a_norm, mla_w_kv_up, mla_q_norm, mla_k_norm, moe_router, moe_w_gate, moe_w_up, moe_w_down):
    nb = x.shape[0]
    assert x.shape == (nb, N_LAT, D) and ctx.shape == (nb, N_CTX, D)
    assert nb + 1 <= ADA_ROWS
    cc = jnp.concatenate([c, c_ctx[None, :], jnp.zeros((ADA_ROWS - nb - 1, D), F32)], axis=0)
    mod_all = _ada_call(cc, ada_w, ada_b)[:, :nb + 1].reshape(DEPTH, nb + 1, N_MOD, D)
    cos_t, sa_t, sb_t = _rope_tables()
    hg_tables = _hgrn_tables()
    ssd_tables = _ssd_tables()
    perm = _rope_perm()
    lb_w = jax.nn.softmax(hgrn_lb_logits.astype(F32), axis=0)
    lb_all = jnp.cumsum(lb_w, axis=0) - lb_w[0]
    rep = lambda v, n: jnp.repeat(v, n, axis=-1)
    stream = (x, ctx)
    for l in range(DEPTH):
        need_ctx = l < DEPTH - 1
        modt = mod_all[l]
        ua, ub, us, um = _in_call(*stream, modt, norm1_w[l][None, :], _prep_in_weights(w_in[l]))
        wri = jnp.concatenate([_block_diag(lru_w_r[l, 0]), _block_diag(lru_w_i[l, 0]),
                               _block_diag(lru_w_r[l, 1]), _block_diag(lru_w_i[l, 1])], axis=1).astype(BF16)
        bri = jnp.concatenate([lru_b_r[l, 0], lru_b_i[l, 0], lru_b_r[l, 1], lru_b_i[l, 1]])[None, :]
        ya = _lru_call(ua, lru_conv_w[l], lru_conv_b[l][None, :], wri, bri, lru_lam[l])
        yb = _hgrn_call(ub, lb_all[l][None, :], hgrn_norm_w[l][None, :], hg_tables)
        narrow = lambda v: jnp.pad(v.reshape(-1), (0, LANES - 2 * SSD_HEADS))[None, :]
        ysd = _ssd_call(us, ssd_conv_w[l], ssd_conv_b[l][None, :], narrow(ssd_a_log[l]), narrow(ssd_dt_bias[l]),
                        rep(ssd_d_skip[l], SSD_HD)[None, :], ssd_norm_w[l][None, :], ssd_tables)
        wq = mla_w_q_up[l].reshape(MLA_QR, MLA_HEADS, MLA_NOPE + MLA_ROPE)
        wq = _head_block(wq[..., :MLA_NOPE], wq[..., MLA_NOPE:][..., perm])
        wq = jnp.pad(wq, ((0, 256 - MLA_QR), (0, 0))).astype(BF16)
        qn = _head_block(jnp.broadcast_to(mla_q_norm[l][:MLA_NOPE], (MLA_HEADS, MLA_NOPE)),
                         jnp.broadcast_to(mla_q_norm[l][MLA_NOPE:][perm], (MLA_HEADS, MLA_ROPE)))[None, :]
        wkv = mla_w_kv_up[l].reshape(MLA_KVR, MLA_HEADS, MLA_NOPE + MLA_V)
        wk = _head_block(wkv[..., :MLA_NOPE], jnp.zeros((MLA_KVR, MLA_HEADS, MLA_ROPE), F32))
        wv = wkv[..., MLA_NOPE:].reshape(MLA_KVR, GW)
        wkv_p = jnp.concatenate([wk, wv], axis=1).astype(BF16)
        kn = _head_block(jnp.broadcast_to(mla_k_norm[l][:MLA_NOPE], (MLA_HEADS, MLA_NOPE)),
                         jnp.zeros((MLA_HEADS, MLA_ROPE), F32))[None, :]
        krw = jnp.concatenate([jnp.zeros((MLA_NOPE,), F32), mla_k_norm[l][MLA_NOPE:][perm],
                               jnp.zeros((HB - MLA_NOPE - MLA_ROPE,), F32)])[None, :]
        qa = jnp.pad(mla_q_a_norm[l], (0, 256 - MLA_QR))[None, :]
        ym = _mla_call(um, need_ctx, qa, wq, qn, mla_kv_a_norm[l][None, :], wkv_p, kn, krw, cos_t, sa_t, sb_t)
        n_tiles = N_TILES if need_ctx else LAT_TILES
        x_mid, h2, aff = _out_call(*stream, (ya, yb, ysd, ym), w_out[l].astype(BF16), modt, norm2_w[l][None, :],
                                   moe_router[l].T, n_tiles)
        wg, wu, wd = moe_w_gate[l].astype(BF16), moe_w_up[l].astype(BF16), moe_w_down[l].astype(BF16)
        stream = (_moe_call(x_mid, aff, h2, modt, wg, wu, wd, need_ctx), None)
    return stream[0]
```

```python
import functools
import math

import jax
import jax.numpy as jnp
import numpy as np
from jax import lax
from jax.experimental import pallas as pl
from jax.experimental.pallas import tpu as pltpu

F32 = jnp.float32
BF16 = jnp.bfloat16
HI = lax.Precision.HIGHEST

D = 1024
DEPTH = 2
N_LAT = 2048
N_CTX = 256
NT = N_LAT + N_CTX
GRID_W = 64
GW = 256
CONV_W = 4
LRU_HEADS, LRU_HD, LRU_C = 4, 64, 8.0
HG_HEADS, HG_HD = 4, 64
SSD_HEADS, SSD_HD, SSD_GROUPS, SSD_STATE = 4, 64, 2, 64
SSD_XBC = GW + 2 * SSD_GROUPS * SSD_STATE
MLA_HEADS, MLA_QR, MLA_KVR, MLA_NOPE, MLA_ROPE = 4, 192, 128, 64, 32
MLA_V = GW // MLA_HEADS
ROPE_BASE = 10000.0
N_EXP, FF, EC_FACTOR = 16, 512, 2
N_MOD = 6
EPS = 1e-6
LRU_COLS = 2 * GW
HG_COLS = 5 * GW
SSD_COLS = GW + SSD_XBC + 2 * SSD_HEADS
MLA_COLS = MLA_QR + MLA_KVR + MLA_ROPE

LANES = 128
SUBLANES = 8
ROW_TILE = 256
N_TILES = NT // ROW_TILE
LAT_TILES = N_LAT // ROW_TILE
UA_W, UB_W, US_W, UM_W = 512, 1280, 896, 512
U_OFFS = (0, UA_W, UA_W + UB_W, UA_W + UB_W + US_W, UA_W + UB_W + US_W + UM_W)
SSD_CH = 128
HG_CH = 64
VMEM_LIMIT = 52 * 1024 * 1024
HEAD_SHIFT = 6
HB_SHIFT = 7
assert LRU_HD == HG_HD == SSD_HD == MLA_V == 1 << HEAD_SHIFT and LANES == 1 << HB_SHIFT


def _cparams(sem):
    return pltpu.CompilerParams(dimension_semantics=sem, vmem_limit_bytes=VMEM_LIMIT)


def _sigmoid(x):
    return 1.0 / (1.0 + jnp.exp(-x))


def _silu(x):
    return x * _sigmoid(x)


def _softplus(x):
    return jnp.maximum(x, 0.0) + jnp.log(1.0 + jnp.exp(-jnp.abs(x)))


def _gelu_tanh(x):
    return 0.5 * x * (1.0 + jnp.tanh(math.sqrt(2.0 / math.pi) * (x + 0.044715 * (x * x * x))))


def _dot(a, b, **kw):
    return jnp.dot(a, b, preferred_element_type=F32, **kw)


def _dot_nt(a, b, **kw):
    return lax.dot_general(a, b, (((1,), (1,)), ((), ())), preferred_element_type=F32, **kw)


def _dot_tn(a, b, **kw):
    return lax.dot_general(a, b, (((0,), (0,)), ((), ())), preferred_element_type=F32, **kw)


def _split_bf16(x, n):
    parts, rest = [], x
    for _ in range(n):
        p = rest.astype(BF16)
        parts.append(p)
        rest = rest - p.astype(F32)
    return parts


ADA_ROWS = 24
ADA_TN = 512


def _ada_kernel(c_ref, w_ref, b_ref, o_ref):
    s = _silu(c_ref[...])
    o_ref[0] = _dot(s, w_ref[0], precision=HI) + b_ref[0]


def _ada_call(cc, ada_w, ada_b):
    return pl.pallas_call(
        _ada_kernel,
        grid=(DEPTH, N_MOD * D // ADA_TN),
        in_specs=[
            pl.BlockSpec((ADA_ROWS, D), lambda l, j: (0, 0)),
            pl.BlockSpec((1, D, ADA_TN), lambda l, j: (l, 0, j)),
            pl.BlockSpec((1, 1, ADA_TN), lambda l, j: (l, 0, j)),
        ],
        out_specs=pl.BlockSpec((1, ADA_ROWS, ADA_TN), lambda l, j: (l, 0, j)),
        out_shape=jax.ShapeDtypeStruct((DEPTH, ADA_ROWS, N_MOD * D), F32),
        compiler_params=_cparams(("parallel", "parallel")),
        name="ada",
    )(cc, ada_w, ada_b.reshape(DEPTH, 1, N_MOD * D))


def _norm_mod(x, nw, shift, scale):
    ms = jnp.mean(x * x, axis=-1, keepdims=True)
    return (x * lax.rsqrt(ms + EPS) * nw) * (1.0 + scale) + shift


def _in_kernel(split, *refs):
    if split:
        x_ref, c_ref, mod_ref, nw_ref, w_ref, ua_ref, ub_ref, us_ref, um_ref = refs
        x = jnp.where(pl.program_id(1) < LAT_TILES, x_ref[0], c_ref[0])
    else:
        x_ref, mod_ref, nw_ref, w_ref, ua_ref, ub_ref, us_ref, um_ref = refs
        x = x_ref[0]
    h = _norm_mod(x, nw_ref[...], mod_ref[0, 0:1, :], mod_ref[0, 1:2, :]).astype(BF16)
    for k, ref in enumerate((ua_ref, ub_ref, us_ref, um_ref)):
        ref[0] = _dot(h, w_ref[:, U_OFFS[k]:U_OFFS[k + 1]])


def _mod_index(b, t, nb):
    return (jnp.where(t >= LAT_TILES, nb, b), 0, 0)


def _token_specs(ctx):
    if ctx is None:
        return [pl.BlockSpec((1, ROW_TILE, D), lambda b, t: (b, t, 0))]
    return [pl.BlockSpec((1, ROW_TILE, D), lambda b, t: (b, jnp.minimum(t, LAT_TILES - 1), 0)),
            pl.BlockSpec((1, N_CTX, D), lambda b, t: (b, 0, 0))]


def _in_call(x, ctx, modt, nw, w_all):
    nb = x.shape[0]
    in_specs = _token_specs(ctx) + [
        pl.BlockSpec((1, N_MOD, D), lambda b, t: _mod_index(b, t, nb)),
        pl.BlockSpec((1, D), lambda b, t: (0, 0)),
        pl.BlockSpec((D, U_OFFS[-1]), lambda b, t: (0, 0)),
    ]
    widths = (UA_W, UB_W, US_W, UM_W)
    out_specs = [pl.BlockSpec((1, ROW_TILE, w), lambda b, t: (b, t, 0)) for w in widths]
    out_shape = [jax.ShapeDtypeStruct((nb, NT, w), F32) for w in widths]
    args = ((x,) if ctx is None else (x, ctx)) + (modt, nw, w_all)
    return pl.pallas_call(
        functools.partial(_in_kernel, ctx is not None),
        grid=(nb, N_TILES),
        in_specs=in_specs,
        out_specs=out_specs,
        out_shape=out_shape,
        compiler_params=_cparams(("parallel", "parallel")),
        name="in_proj",
    )(*args)


def _conv_chunk(u_ref, r0, rows, c0, c1, cw_ref, cb_ref, at_seg_start, at_seg_end):
    x = u_ref[0, pl.ds(r0, rows), c0:c1]
    rp = pl.multiple_of(jnp.maximum(r0 - SUBLANES, 0), SUBLANES)
    rn = pl.multiple_of(jnp.minimum(r0 + rows, NT - SUBLANES), SUBLANES)
    xp = u_ref[0, pl.ds(rp, SUBLANES), c0:c1] * jnp.where(at_seg_start, 0.0, 1.0)
    xn = u_ref[0, pl.ds(rn, SUBLANES), c0:c1] * jnp.where(at_seg_end, 0.0, 1.0)
    xe = jnp.concatenate([xp, x, xn], axis=0)
    tot = rows + 2 * SUBLANES
    lo, hi = SUBLANES, SUBLANES + rows
    xm2 = pltpu.roll(xe, 2, 0)[lo:hi]
    xm1 = pltpu.roll(xe, 1, 0)[lo:hi]
    xp1 = pltpu.roll(xe, tot - 1, 0)[lo:hi]
    return cw_ref[0:1, :] * xm2 + cw_ref[1:2, :] * xm1 + cw_ref[2:3, :] * x + cw_ref[3:4, :] * xp1 + cb_ref[...]


LRU_CH = 256
LRU_NCH = NT // LRU_CH
LRU_LAT_CH = N_LAT // LRU_CH


def _scan_chunk(a, b, row_in_tile, hprev, reverse):
    n = a.shape[0]
    for s in (1, 2, 4):
        if reverse:
            a_s = pltpu.roll(a, n - s, 0)
            b_s = pltpu.roll(b, n - s, 0)
            valid = row_in_tile < SUBLANES - s
        else:
            a_s = pltpu.roll(a, s, 0)
            b_s = pltpu.roll(b, s, 0)
            valid = row_in_tile >= s
        b = b + a * jnp.where(valid, b_s, 0.0)
        a = a * jnp.where(valid, a_s, 1.0)
    tiles = n // SUBLANES
    hs = [None] * tiles
    for j in (reversed(range(tiles)) if reverse else range(tiles)):
        lo = j * SUBLANES
        hj = b[lo:lo + SUBLANES] + a[lo:lo + SUBLANES] * hprev
        hprev = hj[0:1] if reverse else hj[SUBLANES - 1:SUBLANES]
        hs[j] = hj
    return jnp.concatenate(hs, axis=0), hprev


def _lru_kernel(u_ref, cw_ref, cb_ref, wri_ref, bri_ref, lam_ref, y_ref, ab_s, h_s):
    row_in_tile = lax.broadcasted_iota(jnp.int32, (LRU_CH, GW), 0) & (SUBLANES - 1)
    sp = _softplus(-lam_ref[...])

    def prepare(c, carry):
        r0 = pl.multiple_of(c * LRU_CH, LRU_CH)
        is_ctx = c == LRU_LAT_CH
        xc = _conv_chunk(u_ref, r0, LRU_CH, 0, GW, cw_ref, cb_ref, (c == 0) | is_ctx, (c == LRU_LAT_CH - 1) | is_ctx)
        g = _dot(xc.astype(BF16), wri_ref[...]) + bri_ref[...]
        for d in (0, 1):
            r = _sigmoid(g[:, 2 * d * GW:(2 * d + 1) * GW])
            ig = _sigmoid(g[:, (2 * d + 1) * GW:(2 * d + 2) * GW])
            a = jnp.exp(-LRU_C * r * sp[d:d + 1, :])
            ab_s[2 * d, pl.ds(r0, LRU_CH), :] = a
            ab_s[2 * d + 1, pl.ds(r0, LRU_CH), :] = jnp.sqrt(1.0 - a * a) * (ig * xc)
        return carry

    lax.fori_loop(0, LRU_NCH, prepare, 0)

    def body(i, carry):
        hf, hb = carry
        rf = pl.multiple_of(((i + LRU_LAT_CH) % LRU_NCH) * LRU_CH, LRU_CH)
        rb = pl.multiple_of((LRU_LAT_CH - i) * LRU_CH, LRU_CH)
        h, hf = _scan_chunk(ab_s[0, pl.ds(rf, LRU_CH), :], ab_s[1, pl.ds(rf, LRU_CH), :], row_in_tile, hf, False)
        h_s[0, pl.ds(rf, LRU_CH), :] = h
        h, hb = _scan_chunk(ab_s[2, pl.ds(rb, LRU_CH), :], ab_s[3, pl.ds(rb, LRU_CH), :], row_in_tile, hb, True)
        h_s[1, pl.ds(rb, LRU_CH), :] = h
        return hf, hb

    zero = jnp.zeros((1, GW), F32)
    lax.fori_loop(0, LRU_NCH, body, (zero, zero))

    def finish(c, carry):
        r0 = pl.multiple_of(c * LRU_CH, LRU_CH)
        gate = u_ref[0, pl.ds(r0, LRU_CH), GW:2 * GW]
        hsum = h_s[0, pl.ds(r0, LRU_CH), :] + h_s[1, pl.ds(r0, LRU_CH), :]
        y_ref[0, pl.ds(r0, LRU_CH), :] = (hsum * _gelu_tanh(gate)).astype(BF16)
        return carry

    lax.fori_loop(0, LRU_NCH, finish, 0)


def _lru_call(ua, cw, cb, wri, bri, lam):
    nb = ua.shape[0]
    full = lambda shape: pl.BlockSpec(shape, lambda b: (0,) * len(shape))
    return pl.pallas_call(
        _lru_kernel,
        grid=(nb,),
        in_specs=[
            pl.BlockSpec((1, NT, UA_W), lambda b: (b, 0, 0)),
            full((CONV_W, GW)), full((1, GW)), full((GW, 4 * GW)), full((1, 4 * GW)), full((2, GW)),
        ],
        out_specs=pl.BlockSpec((1, NT, GW), lambda b: (b, 0, 0)),
        out_shape=jax.ShapeDtypeStruct((nb, NT, GW), BF16),
        scratch_shapes=[pltpu.VMEM((4, NT, GW), F32), pltpu.VMEM((2, NT, GW), F32)],
        compiler_params=_cparams(("parallel",)),
        name="rglru",
    )(ua, cw, cb, wri, bri, lam)


HG_NCH = NT // HG_CH
HG_LAT_CH = N_LAT // HG_CH
HG_LEVELS = HG_CH.bit_length() - 1
HG_EXP_BLOCKS = 1 + HG_LEVELS
HG_SPLIT = 3


def _hgrn_tables():
    t_ = HG_CH
    dm = np.zeros((2, HG_EXP_BLOCKS, t_, t_), np.float32)
    lm = np.zeros((2, HG_LEVELS + 1, t_, t_), np.float32)
    for d in (0, 1):
        for t in range(t_):
            if d == 0:
                dm[d, 0, t, :t + 1] = 1.0
            else:
                dm[d, 0, t, t:] = 1.0
            lm[d, 0, t, t] = 1.0
        for lev in range(1, HG_LEVELS + 1):
            m = 1 << (lev - 1)
            for t in range(t_):
                start = (t // (2 * m)) * 2 * m
                mid = start + m
                upper = t >= mid
                if d == 0:
                    if upper:
                        dm[d, lev, t, mid:t + 1] = 1.0
                        lm[d, lev, t, start:mid] = 1.0
                    else:
                        dm[d, lev, t, t + 1:mid] = 1.0
                else:
                    if not upper:
                        dm[d, lev, t, t:mid] = 1.0
                        lm[d, lev, t, mid:start + 2 * m] = 1.0
                    else:
                        dm[d, lev, t, mid:t] = 1.0
    dm = dm.reshape(2, HG_EXP_BLOCKS * t_, t_)
    d3 = np.concatenate([dm] * HG_SPLIT, axis=-1)
    lm = np.tile(lm, (1, 1, 1, HG_HEADS))
    heads = np.arange(GW) // HG_HD
    bm = (heads[:, None] == heads[None, :]).astype(np.float32)
    return jnp.asarray(d3, BF16), jnp.asarray(lm, F32), jnp.asarray(bm, F32), jnp.asarray(bm, BF16)


def _hgrn_kernel(u_ref, lb_ref, nw_ref, d3_ref, lm_ref, bm_ref, bmb_ref, y_ref, o_s, st_s, stb_s):
    t = HG_CH
    lb = lb_ref[...]
    st_s[...] = jnp.zeros(st_s.shape, F32)
    stb_s[...] = jnp.zeros(stb_s.shape, BF16)

    def by_head(xb):
        return jnp.concatenate([xb] * HG_HEADS, axis=0) * bmb_ref[...]

    def chunk(d, c):
        r0 = pl.multiple_of(c * t, t)
        q = _silu(u_ref[0, pl.ds(r0, t), 0:GW])
        fr = u_ref[0, pl.ds(r0, t), (1 + d) * GW:(2 + d) * GW]
        v = u_ref[0, pl.ds(r0, t), 3 * GW:4 * GW]
        f = lb + (1.0 - lb) * _sigmoid(fr)
        k = 1.0 - f
        ex = _dot(d3_ref[d], jnp.concatenate(_split_bf16(jnp.log2(f), HG_SPLIT), axis=0))
        cum = ex[0:t]
        a = _dot_nt(q.astype(BF16), by_head(k.astype(BF16))) * lm_ref[d, 0]
        for lev in range(1, HG_LEVELS + 1):
            fac = jnp.exp2(ex[lev * t:(lev + 1) * t])
            a = a + _dot_nt((q * fac).astype(BF16), by_head((k * fac).astype(BF16))) * lm_ref[d, lev]
        vb = v.astype(BF16)
        o = _dot(a.astype(BF16), by_head(vb)) + _dot_nt((q * jnp.exp2(cum)).astype(BF16), stb_s[d])
        o_s[d, pl.ds(r0, t), :] = o
        cend = cum[t - 1:t, :] if d == 0 else cum[0:1, :]
        kend = (k * jnp.exp2(cend - cum)).astype(BF16)
        st = st_s[d] * jnp.exp2(cend) + _dot_tn(vb, kend) * bm_ref[...]
        st_s[d] = st
        stb_s[d] = st.astype(BF16)

    def body(i, carry):
        chunk(0, (i + HG_LAT_CH) % HG_NCH)
        chunk(1, HG_NCH - 1 - i)
        return carry

    lax.fori_loop(0, HG_NCH, body, 0)

    def finish(c, carry):
        r0 = pl.multiple_of(c * t, t)
        osum = o_s[0, pl.ds(r0, t), :] + o_s[1, pl.ds(r0, t), :]
        sq = _split_bf16(osum * osum, 2)
        ms = _dot(jnp.concatenate(sq, axis=1), jnp.concatenate([bmb_ref[...]] * 2, axis=0)) * (1.0 / HG_HD)
        g = u_ref[0, pl.ds(r0, t), 4 * GW:5 * GW]
        y_ref[0, pl.ds(r0, t), :] = (osum * lax.rsqrt(ms + EPS) * nw_ref[...] * _silu(g)).astype(BF16)
        return carry

    lax.fori_loop(0, HG_NCH, finish, 0)


def _hgrn_call(ub, lb, nw, tables):
    nb = ub.shape[0]
    d3, lm, bm, bmb = tables
    full = lambda shape: pl.BlockSpec(shape, lambda b: (0,) * len(shape))
    return pl.pallas_call(
        _hgrn_kernel,
        grid=(nb,),
        in_specs=[pl.BlockSpec((1, NT, UB_W), lambda b: (b, 0, 0)), full((1, GW)), full((1, GW)),
                  full(d3.shape), full(lm.shape), full(bm.shape), full(bmb.shape)],
        out_specs=pl.BlockSpec((1, NT, GW), lambda b: (b, 0, 0)),
        out_shape=jax.ShapeDtypeStruct((nb, NT, GW), BF16),
        scratch_shapes=[pltpu.VMEM((2, NT, GW), F32), pltpu.VMEM((2, GW, GW), F32), pltpu.VMEM((2, GW, GW), BF16)],
        compiler_params=_cparams(("parallel",)),
        name="hgrn2",
    )(ub, lb, nw, d3, lm, bm, bmb)


SSD_NCH = NT // SSD_CH
SSD_LAT_CH = N_LAT // SSD_CH
SSD_DT0 = GW + SSD_XBC
GRP_W = SSD_HEADS // SSD_GROUPS * SSD_HD


SSD_BC_W = 2 * SSD_GROUPS * SSD_STATE
SSD_CUM_SPLIT = 3
SSD_GAIN_SPLIT = 2


def _ssd_tables():
    t = SSD_CH
    tri = np.stack([np.tril(np.ones((t, t), np.float32)), np.triu(np.ones((t, t), np.float32))])
    col = np.zeros((2, LANES, SSD_HEADS * LANES), np.float32)
    wide = np.zeros((2, LANES, GW), np.float32)
    for d in range(2):
        for h in range(SSD_HEADS):
            col[d, d * SSD_HEADS + h, h * LANES:(h + 1) * LANES] = 1.0
            wide[d, d * SSD_HEADS + h, h * SSD_HD:(h + 1) * SSD_HD] = 1.0
    hm = (np.arange(GW)[None, :] // SSD_HD == np.arange(SSD_HEADS)[:, None]).astype(np.float32)
    gm = (np.arange(LANES)[None, :] // SSD_STATE == np.arange(SSD_GROUPS)[:, None]).astype(np.float32)
    return (jnp.asarray(np.concatenate([tri] * SSD_CUM_SPLIT, axis=2), BF16),
            jnp.asarray(np.concatenate([col] * SSD_CUM_SPLIT, axis=1), BF16),
            jnp.asarray(np.concatenate([wide] * SSD_GAIN_SPLIT, axis=1), BF16),
            jnp.asarray(hm[:, None, :], BF16), jnp.asarray(gm[:, None, :], BF16))


def _ssd_kernel(u_ref, cw_ref, cb_ref, a8_ref, b8_ref, dsk_ref, nw_ref, tri_ref, col_ref, wide_ref, hm_ref, gm_ref,
                y_ref, xs_s, bc_s, xm_s, dt_s, y_s, st_s):
    t = SSD_CH
    rr = lax.broadcasted_iota(jnp.int32, (t, t), 0)
    cc = lax.broadcasted_iota(jnp.int32, (t, t), 1)
    keeps = (rr >= cc, rr <= cc)
    a8 = -jnp.exp(a8_ref[...])

    def prepare(c, carry):
        r0 = pl.multiple_of(c * t, t)
        seg_start = (c == 0) | (c == SSD_LAT_CH)
        seg_end = (c == SSD_LAT_CH - 1) | (c == SSD_NCH - 1)
        xbc = _silu(_conv_chunk(u_ref, r0, t, GW, GW + SSD_XBC, cw_ref, cb_ref, seg_start, seg_end))
        xs = xbc[:, 0:GW]
        xs_s[pl.ds(r0, t), :] = xs
        bc_s[pl.ds(r0, t), :] = xbc[:, GW:GW + SSD_BC_W].astype(BF16)
        xb = xs.astype(BF16)
        for h in range(SSD_HEADS):
            xm_s[h, pl.ds(r0, t), :] = xb * hm_ref[h]
        dt_s[pl.ds(r0, t), :] = _softplus(u_ref[0, pl.ds(r0, t), SSD_DT0:SSD_DT0 + LANES] + b8_ref[...])
        return carry

    lax.fori_loop(0, SSD_NCH, prepare, 0)
    st_s[...] = jnp.zeros(st_s.shape, F32)

    def chunk(d, c):
        r0 = pl.multiple_of(c * t, t)
        dtn = dt_s[pl.ds(r0, t), :]
        cumn = _dot(tri_ref[d], jnp.concatenate(_split_bf16(dtn * a8, SSD_CUM_SPLIT), axis=0))
        cend = cumn[t - 1:t, :] if d == 0 else cumn[0:1, :]
        cum_t = cumn.T
        dt_t = dtn.T
        colb = _dot(jnp.concatenate(_split_bf16(cumn, SSD_CUM_SPLIT), axis=1), col_ref[d])
        gains = jnp.concatenate([jnp.exp(cumn), dtn * jnp.exp(cend - cumn)], axis=0)
        gw = _dot(jnp.concatenate(_split_bf16(gains, SSD_GAIN_SPLIT), axis=1), wide_ref[d])
        ecum, wend = gw[0:t], gw[t:2 * t]
        bc = bc_s[pl.ds(r0, t), :]
        bmat, cmat = bc[:, 0:LANES], bc[:, LANES:2 * LANES]
        st = st_s[d]
        y = jnp.zeros((t, GW), F32)
        y_in = []
        for g in range(SSD_GROUPS):
            cg = cmat * gm_ref[g]
            scores = _dot_nt(cg, bmat)
            sg = st[:, g * GRP_W:(g + 1) * GRP_W].astype(BF16)
            y_in.append(_dot(cg, jnp.concatenate([sg] * SSD_GROUPS, axis=0)))
            for hh in range(SSD_HEADS // SSD_GROUPS):
                h = g * (SSD_HEADS // SSD_GROUPS) + hh
                row = d * SSD_HEADS + h
                seg = colb[:, h * LANES:(h + 1) * LANES] - cum_t[row:row + 1, :]
                m = jnp.where(keeps[d], jnp.exp(seg), 0.0) * scores * dt_t[row:row + 1, :]
                y = y + _dot(m.astype(BF16), xm_s[h, pl.ds(r0, t), :])
        y_s[d, pl.ds(r0, t), :] = y + jnp.concatenate(y_in, axis=1) * ecum
        xw = (xs_s[pl.ds(r0, t), :] * wend).astype(BF16)
        upd = [_dot_tn(bmat, xw[:, g * GRP_W:(g + 1) * GRP_W])[g * SSD_STATE:(g + 1) * SSD_STATE]
               for g in range(SSD_GROUPS)]
        eend = ecum[t - 1:t, :] if d == 0 else ecum[0:1, :]
        st_s[d] = st * eend + jnp.concatenate(upd, axis=1)

    def body(i, carry):
        chunk(0, (i + SSD_LAT_CH) % SSD_NCH)
        chunk(1, SSD_NCH - 1 - i)
        return carry

    lax.fori_loop(0, SSD_NCH, body, 0)

    def finish(c, carry):
        r0 = pl.multiple_of(c * t, t)
        z = u_ref[0, pl.ds(r0, t), 0:GW]
        yy = (y_s[0, pl.ds(r0, t), :] + y_s[1, pl.ds(r0, t), :] + dsk_ref[...] * xs_s[pl.ds(r0, t), :]) * _silu(z)
        ms = jnp.mean(yy * yy, axis=-1, keepdims=True)
        y_ref[0, pl.ds(r0, t), :] = (yy * lax.rsqrt(ms + EPS) * nw_ref[...]).astype(BF16)
        return carry

    lax.fori_loop(0, SSD_NCH, finish, 0)


def _ssd_call(us, cw, cb, a8, b8, dsk, nw, tables):
    nb = us.shape[0]
    full = lambda shape: pl.BlockSpec(shape, lambda b: (0,) * len(shape))
    return pl.pallas_call(
        _ssd_kernel,
        grid=(nb,),
        in_specs=[
            pl.BlockSpec((1, NT, US_W), lambda b: (b, 0, 0)),
            full((CONV_W, SSD_XBC)), full((1, SSD_XBC)), full((1, LANES)), full((1, LANES)),
            full((1, GW)), full((1, GW)),
        ] + [full(tb.shape) for tb in tables],
        out_specs=pl.BlockSpec((1, NT, GW), lambda b: (b, 0, 0)),
        out_shape=jax.ShapeDtypeStruct((nb, NT, GW), BF16),
        scratch_shapes=[pltpu.VMEM((NT, GW), F32), pltpu.VMEM((NT, SSD_BC_W), BF16),
                        pltpu.VMEM((SSD_HEADS, NT, GW), BF16), pltpu.VMEM((NT, LANES), F32),
                        pltpu.VMEM((2, NT, GW), F32), pltpu.VMEM((2, SSD_STATE, GW), F32)],
        compiler_params=_cparams(("parallel",)),
        name="ssd",
    )(us, cw, cb, a8, b8, dsk, nw, *tables)


HB = LANES
QW = MLA_HEADS * HB
MLA_QB = 256
MLA_AB = 256
MLA_SCALE = (MLA_NOPE + MLA_ROPE) ** -0.5


def _mla_kernel(need_ctx, u_ref, qa_ref, wq_ref, qn_ref, kva_ref, wkv_ref, kn_ref, krw_ref,
                cos_ref, sa_ref, sb_ref, y_ref, q_s, k_s, v_s):
    lane = lax.broadcasted_iota(jnp.int32, (QW, QW), 1)
    row = lax.broadcasted_iota(jnp.int32, (QW, QW), 0)

    def grp(i):
        within = i & (HB - 1)
        return (i >> HB_SHIFT) * 3 + jnp.where(within < MLA_NOPE, 0, jnp.where(within < MLA_NOPE + MLA_ROPE, 1, 2))

    gmat = jnp.where(grp(row) == grp(lane), 1.0, 0.0).astype(BF16)
    l1 = lax.broadcasted_iota(jnp.int32, (1, QW), 1) & (HB - 1)
    inv_size = jnp.where(l1 < MLA_NOPE, 1.0 / MLA_NOPE, 1.0 / MLA_ROPE)
    vlane_head = lax.broadcasted_iota(jnp.int32, (MLA_QB, GW), 1) >> HEAD_SHIFT

    def rope(x, r0, reps):
        cosr = cos_ref[pl.ds(r0, MLA_QB), :]
        sar = sa_ref[pl.ds(r0, MLA_QB), :]
        sbr = sb_ref[pl.ds(r0, MLA_QB), :]
        if reps > 1:
            cosr, sar, sbr = (jnp.concatenate([tbl] * reps, axis=1) for tbl in (cosr, sar, sbr))
        w = x.shape[1]
        return x * cosr + pltpu.roll(x, w - MLA_ROPE // 2, 1) * sar + pltpu.roll(x, MLA_ROPE // 2, 1) * sbr

    def project(ci, carry):
        r0 = pl.multiple_of(ci * MLA_QB, MLA_QB)
        cq = u_ref[0, pl.ds(r0, MLA_QB), 0:256]
        ms = jnp.sum(cq * cq, axis=-1, keepdims=True) * (1.0 / MLA_QR)
        qraw = _dot((cq * lax.rsqrt(ms + EPS) * qa_ref[...]).astype(BF16), wq_ref[...])
        ss = _dot((qraw * qraw).astype(BF16), gmat) * inv_size
        q = rope(qraw * lax.rsqrt(ss + EPS) * qn_ref[...], r0, MLA_HEADS) * MLA_SCALE
        q_s[pl.ds(r0, MLA_QB), :] = q.astype(BF16)
        ckv = u_ref[0, pl.ds(r0, MLA_QB), 256:256 + MLA_KVR]
        ms = jnp.mean(ckv * ckv, axis=-1, keepdims=True)
        kv = _dot((ckv * lax.rsqrt(ms + EPS) * kva_ref[...]).astype(BF16), wkv_ref[...])
        kraw = kv[:, 0:QW]
        ss = _dot((kraw * kraw).astype(BF16), gmat) * inv_size
        knope = kraw * lax.rsqrt(ss + EPS) * kn_ref[...]
        kr = u_ref[0, pl.ds(r0, MLA_QB), 384:512]
        ms = jnp.sum(kr * kr, axis=-1, keepdims=True) * (1.0 / MLA_ROPE)
        krope = rope(kr * lax.rsqrt(ms + EPS) * krw_ref[...], r0, 1)
        k_s[pl.ds(r0, MLA_QB), :] = (knope + jnp.concatenate([krope] * MLA_HEADS, axis=1)).astype(BF16)
        vv = kv[:, QW:QW + GW]
        for h in range(MLA_HEADS):
            v_s[h, pl.ds(r0, MLA_QB), :] = jnp.where(vlane_head == h, vv, 0.0).astype(BF16)
        return carry

    lax.fori_loop(0, NT // MLA_QB, project, 0)

    def attend(r0, rows, k0, klen):
        o = jnp.zeros((rows, GW), F32)
        for h in range(MLA_HEADS):
            qh = q_s[pl.ds(r0, rows), h * HB:(h + 1) * HB]
            s = _dot_nt(qh, k_s[k0:k0 + klen, h * HB:(h + 1) * HB])
            p = jnp.exp(s - jnp.max(s, axis=-1, keepdims=True))
            l = jnp.sum(p, axis=-1, keepdims=True)
            o = o + _dot(p.astype(BF16), v_s[h, k0:k0 + klen, :]) * (1.0 / l)
        y_ref[0, pl.ds(r0, rows), :] = o.astype(BF16)

    def lat_block(qi, carry):
        attend(pl.multiple_of(qi * MLA_AB, MLA_AB), MLA_AB, 0, NT)
        return carry

    lax.fori_loop(0, N_LAT // MLA_AB, lat_block, 0)
    if need_ctx:
        attend(N_LAT, N_CTX, N_LAT, N_CTX)
    else:
        y_ref[0, N_LAT:NT, :] = jnp.zeros((N_CTX, GW), BF16)


def _mla_call(um, need_ctx, qa, wq, qn, kva, wkv, kn, krw, cos_t, sa_t, sb_t):
    nb = um.shape[0]
    full = lambda shape: pl.BlockSpec(shape, lambda b: (0,) * len(shape))
    return pl.pallas_call(
        functools.partial(_mla_kernel, need_ctx),
        grid=(nb,),
        in_specs=[
            pl.BlockSpec((1, NT, UM_W), lambda b: (b, 0, 0)),
            full((1, 256)), full((256, QW)), full((1, QW)),
            full((1, MLA_KVR)), full((MLA_KVR, QW + GW)), full((1, QW)), full((1, HB)),
            full((NT, HB)), full((NT, HB)), full((NT, HB)),
        ],
        out_specs=pl.BlockSpec((1, NT, GW), lambda b: (b, 0, 0)),
        out_shape=jax.ShapeDtypeStruct((nb, NT, GW), BF16),
        scratch_shapes=[pltpu.VMEM((NT, QW), BF16), pltpu.VMEM((NT, QW), BF16),
                        pltpu.VMEM((MLA_HEADS, NT, GW), BF16)],
        compiler_params=_cparams(("parallel",)),
        name="mla",
    )(um, qa, wq, qn, kva, wkv, kn, krw, cos_t, sa_t, sb_t)


def _out_kernel(split, *refs):
    if split:
        x_ref, c_ref, ya_ref, yb_ref, ys_ref, ym_ref, wo_ref, mod_ref, nw_ref, wr_ref, xo_ref, h2_ref, aff_ref = refs
        x_in = jnp.where(pl.program_id(1) < LAT_TILES, x_ref[0], c_ref[0])
    else:
        x_ref, ya_ref, yb_ref, ys_ref, ym_ref, wo_ref, mod_ref, nw_ref, wr_ref, xo_ref, h2_ref, aff_ref = refs
        x_in = x_ref[0]
    y = jnp.concatenate([ya_ref[0], yb_ref[0], ys_ref[0], ym_ref[0]], axis=1)
    x = x_in + mod_ref[0, 2:3, :] * _dot(y, wo_ref[...])
    xo_ref[0] = x
    h2 = _norm_mod(x, nw_ref[...], mod_ref[0, 3:4, :], mod_ref[0, 4:5, :])
    hi = h2.astype(BF16)
    h2_ref[0] = hi
    lo = (h2 - hi.astype(F32)).astype(BF16)
    wr = wr_ref[...]
    w_hi = wr.astype(BF16)
    w_lo = (wr - w_hi.astype(F32)).astype(BF16)
    logit = _dot_nt(jnp.concatenate([w_hi, w_hi, w_lo], axis=1), jnp.concatenate([hi, lo, hi], axis=1))
    e = jnp.exp(logit - jnp.max(logit, axis=0, keepdims=True))
    aff_ref[0] = e / jnp.sum(e, axis=0, keepdims=True)


def _out_call(x, ctx, ys, wo, modt, nw, wr_t, n_tiles):
    nb = x.shape[0]
    tok = lambda w: pl.BlockSpec((1, ROW_TILE, w), lambda b, t: (b, t, 0))
    x_args = (x,) if ctx is None else (x, ctx)
    return pl.pallas_call(
        functools.partial(_out_kernel, ctx is not None),
        grid=(nb, n_tiles),
        in_specs=_token_specs(ctx) + [tok(GW), tok(GW), tok(GW), tok(GW),
                  pl.BlockSpec((D, D), lambda b, t: (0, 0)),
                  pl.BlockSpec((1, N_MOD, D), lambda b, t: _mod_index(b, t, nb)),
                  pl.BlockSpec((1, D), lambda b, t: (0, 0)),
                  pl.BlockSpec((N_EXP, D), lambda b, t: (0, 0))],
        out_specs=[tok(D), tok(D), pl.BlockSpec((1, N_EXP, ROW_TILE), lambda b, t: (b, 0, t))],
        out_shape=[jax.ShapeDtypeStruct((nb, n_tiles * ROW_TILE, D), F32),
                   jax.ShapeDtypeStruct((nb, n_tiles * ROW_TILE, D), BF16),
                   jax.ShapeDtypeStruct((nb, N_EXP, n_tiles * ROW_TILE), F32)],
        compiler_params=_cparams(("parallel", "parallel")),
        name="out_proj",
    )(*x_args, *ys, wo, modt, nw, wr_t)


PRE_W = 256
ROUTE_MAX_IT = 160


def _prefix_count(m, tri_bf):
    n = m.shape[1]
    off = jnp.zeros((m.shape[0], 1), F32)
    outs = []
    for j in range(n // PRE_W):
        blk = m[:, j * PRE_W:(j + 1) * PRE_W]
        outs.append(_dot(blk.astype(BF16), tri_bf) + off)
        off = off + jnp.sum(blk, axis=1, keepdims=True)
    return outs[0] if len(outs) == 1 else jnp.concatenate(outs, axis=1)


def _route(aff, cap, tri):
    n_lo0 = jnp.full((N_EXP, 1), float(aff.shape[1]), F32)

    def cond(s):
        it, _, _, n_lo, n_hi = s
        return jnp.logical_and(it < ROUTE_MAX_IT, jnp.max(n_lo - n_hi) > 1.0)

    def step(s):
        it, lo, hi, n_lo, n_hi = s
        mid = 0.5 * (lo + hi)
        cnt = jnp.sum(jnp.where(aff > mid, 1.0, 0.0), axis=1, keepdims=True)
        below = jnp.max(jnp.where(aff <= mid, aff, -1.0), axis=1, keepdims=True)
        up = cnt >= cap
        return (it + 1, jnp.where(up, mid, lo), jnp.where(up, hi, below),
                jnp.where(up, cnt, n_lo), jnp.where(up, n_hi, cnt))

    init = (jnp.int32(0), jnp.full((N_EXP, 1), -1.0, F32), jnp.max(aff, axis=1, keepdims=True),
            n_lo0, jnp.zeros((N_EXP, 1), F32))
    _, _, thr, _, n_gt = lax.while_loop(cond, step, init)
    gt = jnp.where(aff > thr, 1.0, 0.0)
    eq = jnp.where(aff == thr, 1.0, 0.0)
    sel = gt + eq * jnp.where(_prefix_count(eq, tri) < cap - n_gt, 1.0, 0.0)
    return jnp.where(sel > 0.0, _prefix_count(sel, tri), -1.0)


MOE_EPS = 2


def _moe_kernel(segs, x_hbm, aff_ref, h_ref, gl_ref, gc_ref, wg_ref, wu_ref, wd_ref, o_ref, pos_ref, sem):
    b = pl.program_id(0)
    step = pl.program_id(1)
    rows = o_ref.shape[1]

    def residual_copy():
        return pltpu.make_async_copy(x_hbm.at[b, pl.ds(0, rows), :], o_ref.at[0], sem)

    @pl.when(step == 0)
    def _init():
        residual_copy().start()
        tri = jnp.where(lax.broadcasted_iota(jnp.int32, (PRE_W, PRE_W), 0)
                        < lax.broadcasted_iota(jnp.int32, (PRE_W, PRE_W), 1), 1.0, 0.0).astype(BF16)
        for r0, n, cap in segs:
            pos_ref[:, r0:r0 + n] = _route(aff_ref[0, :, r0:r0 + n], cap, tri)
        residual_copy().wait()

    onehots, gates, xs = [], [], []
    for (r0, n, cap), g_ref in zip(segs, (gl_ref, gc_ref)):
        slot = lax.broadcasted_iota(jnp.int32, (cap, n), 0).astype(F32)
        oh, gt = [], []
        for j in range(MOE_EPS):
            e = step * MOE_EPS + j
            hit = slot == pos_ref[pl.ds(e, 1), r0:r0 + n]
            oh.append(jnp.where(hit, 1.0, 0.0).astype(BF16))
            gsel = jnp.sum(jnp.where(hit, aff_ref[0, pl.ds(e, 1), r0:r0 + n], 0.0), axis=1, keepdims=True)
            gt.append(gsel * g_ref[0, N_MOD - 1:N_MOD, :])
        onehot = jnp.concatenate(oh, axis=0)
        onehots.append(onehot)
        gates.append(gt)
        xs.append(_dot(onehot, h_ref[0, r0:r0 + n, :]).astype(BF16))
    ys = []
    for j in range(MOE_EPS):
        xj = [x[j * cap:(j + 1) * cap] for x, (_, _, cap) in zip(xs, segs)]
        xj = xj[0] if len(xj) == 1 else jnp.concatenate(xj, axis=0)
        act = (_silu(_dot(xj, wg_ref[j])) * _dot(xj, wu_ref[j])).astype(BF16)
        ys.append(_dot(act, wd_ref[j]))
    s0 = 0
    for (r0, n, cap), onehot, gt in zip(segs, onehots, gates):
        ysg = jnp.concatenate([(ys[j][s0:s0 + cap] * gt[j]).astype(BF16) for j in range(MOE_EPS)], axis=0)
        o_ref[0, r0:r0 + n, :] += _dot_tn(onehot, ysg)
        s0 += cap


def _moe_call(x_mid, aff, h2, modt, wg, wu, wd, with_ctx):
    nb = h2.shape[0]
    segs = ((0, N_LAT, EC_FACTOR * N_LAT // N_EXP),)
    if with_ctx:
        segs += ((N_LAT, N_CTX, EC_FACTOR * N_CTX // N_EXP),)
    rows = NT if with_ctx else N_LAT
    return pl.pallas_call(
        functools.partial(_moe_kernel, segs),
        grid=(nb, N_EXP // MOE_EPS),
        in_specs=[
            pl.BlockSpec(memory_space=pl.ANY),
            pl.BlockSpec((1, N_EXP, rows), lambda b, e: (b, 0, 0)),
            pl.BlockSpec((1, rows, D), lambda b, e: (b, 0, 0)),
            pl.BlockSpec((1, N_MOD, D), lambda b, e: (b, 0, 0)),
            pl.BlockSpec((1, N_MOD, D), lambda b, e: (nb, 0, 0)),
            pl.BlockSpec((MOE_EPS, D, FF), lambda b, e: (e, 0, 0)),
            pl.BlockSpec((MOE_EPS, D, FF), lambda b, e: (e, 0, 0)),
            pl.BlockSpec((MOE_EPS, FF, D), lambda b, e: (e, 0, 0)),
        ],
        out_specs=pl.BlockSpec((1, rows, D), lambda b, e: (b, 0, 0)),
        out_shape=jax.ShapeDtypeStruct((nb, rows, D), F32),
        scratch_shapes=[pltpu.VMEM((N_EXP, rows), F32), pltpu.SemaphoreType.DMA(())],
        compiler_params=_cparams(("parallel", "arbitrary")),
        name="moe",
    )(x_mid, aff, h2, modt, modt, wg, wu, wd)


def _pad_cols(w, width):
    return jnp.pad(w, ((0, 0), (0, width - w.shape[1])))


def _rope_perm():
    half = MLA_ROPE // 2
    return jnp.concatenate([jnp.arange(half) * 2, jnp.arange(half) * 2 + 1])


def _head_block(nope, rope):
    pad = jnp.zeros(nope.shape[:-1] + (HB - MLA_NOPE - MLA_ROPE,), nope.dtype)
    blk = jnp.concatenate([nope, rope, pad], axis=-1)
    return blk.reshape(blk.shape[:-2] + (MLA_HEADS * HB,))


def _prep_in_weights(w_in):
    o1, o2, o3 = LRU_COLS, LRU_COLS + HG_COLS, LRU_COLS + HG_COLS + SSD_COLS
    wa, wb, ws, wm = w_in[:, :o1], w_in[:, o1:o2], w_in[:, o2:o3], w_in[:, o3:]
    perm = _rope_perm()
    cq, ckv, kr = wm[:, :MLA_QR], wm[:, MLA_QR:MLA_QR + MLA_KVR], wm[:, MLA_QR + MLA_KVR:]
    zeros = lambda n: jnp.zeros((D, n), w_in.dtype)
    wm_p = jnp.concatenate([cq, zeros(256 - MLA_QR), ckv, zeros(MLA_NOPE), kr[:, perm],
                            zeros(HB - MLA_NOPE - MLA_ROPE)], axis=1)
    return jnp.concatenate([wa, wb, _pad_cols(ws, US_W), wm_p], axis=1).astype(BF16)


def _block_diag(w):
    h, dd, _ = w.shape
    eye = jnp.eye(h, dtype=w.dtype)
    return (eye[:, None, :, None] * w[:, :, None, :]).reshape(h * dd, h * dd)


def _rope_tables():
    rows = N_LAT // GRID_W
    row = jnp.repeat(jnp.arange(rows, dtype=F32), GRID_W)
    col = jnp.tile(jnp.arange(GRID_W, dtype=F32), rows)
    half = MLA_ROPE // 2
    inv = ROPE_BASE ** (-jnp.arange(0, half, 2, dtype=F32) / half)
    ang = jnp.concatenate([row[:, None] * inv, col[:, None] * inv], axis=-1)
    cos, sin = jnp.cos(ang), jnp.sin(ang)
    z = lambda n: jnp.zeros((N_LAT, n), F32)
    o = lambda n: jnp.ones((N_LAT, n), F32)
    cos_t = jnp.concatenate([o(MLA_NOPE), cos, cos, o(HB - MLA_NOPE - MLA_ROPE)], axis=1)
    sa_t = jnp.concatenate([z(MLA_NOPE), -sin, z(half), z(HB - MLA_NOPE - MLA_ROPE)], axis=1)
    sb_t = jnp.concatenate([z(MLA_NOPE), z(half), sin, z(HB - MLA_NOPE - MLA_ROPE)], axis=1)
    cos_t = jnp.concatenate([cos_t, jnp.ones((N_CTX, HB), F32)], axis=0)
    sa_t = jnp.concatenate([sa_t, jnp.zeros((N_CTX, HB), F32)], axis=0)
    sb_t = jnp.concatenate([sb_t, jnp.zeros((N_CTX, HB), F32)], axis=0)
    return cos_t, sa_t, sb_t


def kernel(x, c, ctx, c_ctx, ada_w, ada_b, norm1_w, norm2_w, w_in, w_out, lru_conv_w, lru_conv_b, lru_w_r, lru_b_r, lru_w_i, lru_b_i, lru_lam, hgrn_lb_logits, hgrn_norm_w, ssd_conv_w, ssd_conv_b, ssd_a_log, ssd_dt_bias, ssd_d_skip, ssd_norm_w, mla_q_a_norm, mla_w_q_up, mla_kv_a_norm, mla_w_kv_up, mla_q_norm, mla_k_norm, moe_router, moe_w_gate, moe_w_up, moe_w_down):
    nb = x.shape[0]
    assert x.shape == (nb, N_LAT, D) and ctx.shape == (nb, N_CTX, D)
    assert nb + 1 <= ADA_ROWS
    cc = jnp.concatenate([c, c_ctx[None, :], jnp.zeros((ADA_ROWS - nb - 1, D), F32)], axis=0)
    mod_all = _ada_call(cc, ada_w, ada_b)[:, :nb + 1].reshape(DEPTH, nb + 1, N_MOD, D)
    cos_t, sa_t, sb_t = _rope_tables()
    hg_tables = _hgrn_tables()
    ssd_tables = _ssd_tables()
    perm = _rope_perm()
    lb_w = jax.nn.softmax(hgrn_lb_logits.astype(F32), axis=0)
    lb_all = jnp.cumsum(lb_w, axis=0) - lb_w[0]
    rep = lambda v, n: jnp.repeat(v, n, axis=-1)
    stream = (x, ctx)
    for l in range(DEPTH):
        need_ctx = l < DEPTH - 1
        modt = mod_all[l]
        ua, ub, us, um = _in_call(*stream, modt, norm1_w[l][None, :], _prep_in_weights(w_in[l]))
        wri = jnp.concatenate([_block_diag(lru_w_r[l, 0]), _block_diag(lru_w_i[l, 0]),
                               _block_diag(lru_w_r[l, 1]), _block_diag(lru_w_i[l, 1])], axis=1).astype(BF16)
        bri = jnp.concatenate([lru_b_r[l, 0], lru_b_i[l, 0], lru_b_r[l, 1], lru_b_i[l, 1]])[None, :]
        ya = _lru_call(ua, lru_conv_w[l], lru_conv_b[l][None, :], wri, bri, lru_lam[l])
        yb = _hgrn_call(ub, lb_all[l][None, :], hgrn_norm_w[l][None, :], hg_tables)
        narrow = lambda v: jnp.pad(v.reshape(-1), (0, LANES - 2 * SSD_HEADS))[None, :]
        ysd = _ssd_call(us, ssd_conv_w[l], ssd_conv_b[l][None, :], narrow(ssd_a_log[l]), narrow(ssd_dt_bias[l]),
                        rep(ssd_d_skip[l], SSD_HD)[None, :], ssd_norm_w[l][None, :], ssd_tables)
        wq = mla_w_q_up[l].reshape(MLA_QR, MLA_HEADS, MLA_NOPE + MLA_ROPE)
        wq = _head_block(wq[..., :MLA_NOPE], wq[..., MLA_NOPE:][..., perm])
        wq = jnp.pad(wq, ((0, 256 - MLA_QR), (0, 0))).astype(BF16)
        qn = _head_block(jnp.broadcast_to(mla_q_norm[l][:MLA_NOPE], (MLA_HEADS, MLA_NOPE)),
                         jnp.broadcast_to(mla_q_norm[l][MLA_NOPE:][perm], (MLA_HEADS, MLA_ROPE)))[None, :]
        wkv = mla_w_kv_up[l].reshape(MLA_KVR, MLA_HEADS, MLA_NOPE + MLA_V)
        wk = _head_block(wkv[..., :MLA_NOPE], jnp.zeros((MLA_KVR, MLA_HEADS, MLA_ROPE), F32))
        wv = wkv[..., MLA_NOPE:].reshape(MLA_KVR, GW)
        wkv_p = jnp.concatenate([wk, wv], axis=1).astype(BF16)
        kn = _head_block(jnp.broadcast_to(mla_k_norm[l][:MLA_NOPE], (MLA_HEADS, MLA_NOPE)),
                         jnp.zeros((MLA_HEADS, MLA_ROPE), F32))[None, :]
        krw = jnp.concatenate([jnp.zeros((MLA_NOPE,), F32), mla_k_norm[l][MLA_NOPE:][perm],
                               jnp.zeros((HB - MLA_NOPE - MLA_ROPE,), F32)])[None, :]
        qa = jnp.pad(mla_q_a_norm[l], (0, 256 - MLA_QR))[None, :]
        ym = _mla_call(um, need_ctx, qa, wq, qn, mla_kv_a_norm[l][None, :], wkv_p, kn, krw, cos_t, sa_t, sb_t)
        n_tiles = N_TILES if need_ctx else LAT_TILES
        x_mid, h2, aff = _out_call(*stream, (ya, yb, ysd, ym), w_out[l].astype(BF16), modt, norm2_w[l][None, :],
                                   moe_router[l].T, n_tiles)
        wg, wu, wd = moe_w_gate[l].astype(BF16), moe_w_up[l].astype(BF16), moe_w_down[l].astype(BF16)
        stream = (_moe_call(x_mid, aff, h2, modt, wg, wu, wd, need_ctx), None)
    return stream[0]
```

```python
import functools
import math

import jax
import jax.numpy as jnp
import numpy as np
from jax import lax
from jax.experimental import pallas as pl
from jax.experimental.pallas import tpu as pltpu

F32 = jnp.float32
BF16 = jnp.bfloat16
HI = lax.Precision.HIGHEST

D = 1024
DEPTH = 2
N_LAT = 2048
N_CTX = 256
NT = N_LAT + N_CTX
GRID_W = 64
GW = 256
CONV_W = 4
LRU_HEADS, LRU_HD, LRU_C = 4, 64, 8.0
HG_HEADS, HG_HD = 4, 64
SSD_HEADS, SSD_HD, SSD_GROUPS, SSD_STATE = 4, 64, 2, 64
SSD_XBC = GW + 2 * SSD_GROUPS * SSD_STATE
MLA_HEADS, MLA_QR, MLA_KVR, MLA_NOPE, MLA_ROPE = 4, 192, 128, 64, 32
MLA_V = GW // MLA_HEADS
ROPE_BASE = 10000.0
N_EXP, FF, EC_FACTOR = 16, 512, 2
N_MOD = 6
EPS = 1e-6
LRU_COLS = 2 * GW
HG_COLS = 5 * GW
SSD_COLS = GW + SSD_XBC + 2 * SSD_HEADS
MLA_COLS = MLA_QR + MLA_KVR + MLA_ROPE

LANES = 128
SUBLANES = 8
ROW_TILE = 256
N_TILES = NT // ROW_TILE
LAT_TILES = N_LAT // ROW_TILE
UA_W, UB_W, US_W, UM_W = 512, 1280, 896, 512
U_OFFS = (0, UA_W, UA_W + UB_W, UA_W + UB_W + US_W, UA_W + UB_W + US_W + UM_W)
SSD_CH = 128
HG_CH = 64
VMEM_LIMIT = 52 * 1024 * 1024
HEAD_SHIFT = 6
HB_SHIFT = 7
assert LRU_HD == HG_HD == SSD_HD == MLA_V == 1 << HEAD_SHIFT and LANES == 1 << HB_SHIFT


def _cparams(sem):
    return pltpu.CompilerParams(dimension_semantics=sem, vmem_limit_bytes=VMEM_LIMIT)


def _sigmoid(x):
    return 1.0 / (1.0 + jnp.exp(-x))


def _silu(x):
    return x * _sigmoid(x)


def _softplus(x):
    return jnp.maximum(x, 0.0) + jnp.log(1.0 + jnp.exp(-jnp.abs(x)))


def _gelu_tanh(x):
    return 0.5 * x * (1.0 + jnp.tanh(math.sqrt(2.0 / math.pi) * (x + 0.044715 * (x * x * x))))


def _dot(a, b, **kw):
    return jnp.dot(a, b, preferred_element_type=F32, **kw)


def _dot_nt(a, b, **kw):
    return lax.dot_general(a, b, (((1,), (1,)), ((), ())), preferred_element_type=F32, **kw)


def _dot_tn(a, b, **kw):
    return lax.dot_general(a, b, (((0,), (0,)), ((), ())), preferred_element_type=F32, **kw)


def _split_bf16(x, n):
    parts, rest = [], x
    for _ in range(n):
        p = rest.astype(BF16)
        parts.append(p)
        rest = rest - p.astype(F32)
    return parts


ADA_ROWS = 24
ADA_TN = 512


def _ada_kernel(c_ref, w_ref, b_ref, o_ref):
    s = _silu(c_ref[...])
    o_ref[0] = _dot(s, w_ref[0], precision=HI) + b_ref[0]


def _ada_call(cc, ada_w, ada_b):
    return pl.pallas_call(
        _ada_kernel,
        grid=(DEPTH, N_MOD * D // ADA_TN),
        in_specs=[
            pl.BlockSpec((ADA_ROWS, D), lambda l, j: (0, 0)),
            pl.BlockSpec((1, D, ADA_TN), lambda l, j: (l, 0, j)),
            pl.BlockSpec((1, 1, ADA_TN), lambda l, j: (l, 0, j)),
        ],
        out_specs=pl.BlockSpec((1, ADA_ROWS, ADA_TN), lambda l, j: (l, 0, j)),
        out_shape=jax.ShapeDtypeStruct((DEPTH, ADA_ROWS, N_MOD * D), F32),
        compiler_params=_cparams(("parallel", "parallel")),
        name="ada",
    )(cc, ada_w, ada_b.reshape(DEPTH, 1, N_MOD * D))


def _norm_mod(x, nw, shift, scale):
    ms = jnp.mean(x * x, axis=-1, keepdims=True)
    return (x * lax.rsqrt(ms + EPS) * nw) * (1.0 + scale) + shift


def _in_kernel(split, *refs):
    if split:
        x_ref, c_ref, mod_ref, nw_ref, w_ref, ua_ref, ub_ref, us_ref, um_ref = refs
        x = jnp.where(pl.program_id(1) < LAT_TILES, x_ref[0], c_ref[0])
    else:
        x_ref, mod_ref, nw_ref, w_ref, ua_ref, ub_ref, us_ref, um_ref = refs
        x = x_ref[0]
    h = _norm_mod(x, nw_ref[...], mod_ref[0, 0:1, :], mod_ref[0, 1:2, :]).astype(BF16)
    for k, ref in enumerate((ua_ref, ub_ref, us_ref, um_ref)):
        ref[0] = _dot(h, w_ref[:, U_OFFS[k]:U_OFFS[k + 1]])


def _mod_index(b, t, nb):
    return (jnp.where(t >= LAT_TILES, nb, b), 0, 0)


def _token_specs(ctx):
    if ctx is None:
        return [pl.BlockSpec((1, ROW_TILE, D), lambda b, t: (b, t, 0))]
    return [pl.BlockSpec((1, ROW_TILE, D), lambda b, t: (b, jnp.minimum(t, LAT_TILES - 1), 0)),
            pl.BlockSpec((1, N_CTX, D), lambda b, t: (b, 0, 0))]


def _in_call(x, ctx, modt, nw, w_all):
    nb = x.shape[0]
    in_specs = _token_specs(ctx) + [
        pl.BlockSpec((1, N_MOD, D), lambda b, t: _mod_index(b, t, nb)),
        pl.BlockSpec((1, D), lambda b, t: (0, 0)),
        pl.BlockSpec((D, U_OFFS[-1]), lambda b, t: (0, 0)),
    ]
    widths = (UA_W, UB_W, US_W, UM_W)
    out_specs = [pl.BlockSpec((1, ROW_TILE, w), lambda b, t: (b, t, 0)) for w in widths]
    out_shape = [jax.ShapeDtypeStruct((nb, NT, w), F32) for w in widths]
    args = ((x,) if ctx is None else (x, ctx)) + (modt, nw, w_all)
    return pl.pallas_call(
        functools.partial(_in_kernel, ctx is not None),
        grid=(nb, N_TILES),
        in_specs=in_specs,
        out_specs=out_specs,
        out_shape=out_shape,
        compiler_params=_cparams(("parallel", "parallel")),
        name="in_proj",
    )(*args)


def _conv_chunk(u_ref, r0, rows, c0, c1, cw_ref, cb_ref, at_seg_start, at_seg_end):
    x = u_ref[0, pl.ds(r0, rows), c0:c1]
    rp = pl.multiple_of(jnp.maximum(r0 - SUBLANES, 0), SUBLANES)
    rn = pl.multiple_of(jnp.minimum(r0 + rows, NT - SUBLANES), SUBLANES)
    xp = u_ref[0, pl.ds(rp, SUBLANES), c0:c1] * jnp.where(at_seg_start, 0.0, 1.0)
    xn = u_ref[0, pl.ds(rn, SUBLANES), c0:c1] * jnp.where(at_seg_end, 0.0, 1.0)
    xe = jnp.concatenate([xp, x, xn], axis=0)
    tot = rows + 2 * SUBLANES
    lo, hi = SUBLANES, SUBLANES + rows
    xm2 = pltpu.roll(xe, 2, 0)[lo:hi]
    xm1 = pltpu.roll(xe, 1, 0)[lo:hi]
    xp1 = pltpu.roll(xe, tot - 1, 0)[lo:hi]
    return cw_ref[0:1, :] * xm2 + cw_ref[1:2, :] * xm1 + cw_ref[2:3, :] * x + cw_ref[3:4, :] * xp1 + cb_ref[...]


LRU_CH = 256
LRU_NCH = NT // LRU_CH
LRU_LAT_CH = N_LAT // LRU_CH


def _scan_chunk(a, b, row_in_tile, hprev, reverse):
    n = a.shape[0]
    for s in (1, 2, 4):
        if reverse:
            a_s = pltpu.roll(a, n - s, 0)
            b_s = pltpu.roll(b, n - s, 0)
            valid = row_in_tile < SUBLANES - s
        else:
            a_s = pltpu.roll(a, s, 0)
            b_s = pltpu.roll(b, s, 0)
            valid = row_in_tile >= s
        b = b + a * jnp.where(valid, b_s, 0.0)
        a = a * jnp.where(valid, a_s, 1.0)
    tiles = n // SUBLANES
    hs = [None] * tiles
    for j in (reversed(range(tiles)) if reverse else range(tiles)):
        lo = j * SUBLANES
        hj = b[lo:lo + SUBLANES] + a[lo:lo + SUBLANES] * hprev
        hprev = hj[0:1] if reverse else hj[SUBLANES - 1:SUBLANES]
        hs[j] = hj
    return jnp.concatenate(hs, axis=0), hprev


def _lru_kernel(u_ref, cw_ref, cb_ref, wri_ref, bri_ref, lam_ref, y_ref, ab_s, h_s):
    row_in_tile = lax.broadcasted_iota(jnp.int32, (LRU_CH, GW), 0) & (SUBLANES - 1)
    sp = _softplus(-lam_ref[...])

    def prepare(c, carry):
        r0 = pl.multiple_of(c * LRU_CH, LRU_CH)
        is_ctx = c == LRU_LAT_CH
        xc = _conv_chunk(u_ref, r0, LRU_CH, 0, GW, cw_ref, cb_ref, (c == 0) | is_ctx, (c == LRU_LAT_CH - 1) | is_ctx)
        g = _dot(xc.astype(BF16), wri_ref[...]) + bri_ref[...]
        for d in (0, 1):
            r = _sigmoid(g[:, 2 * d * GW:(2 * d + 1) * GW])
            ig = _sigmoid(g[:, (2 * d + 1) * GW:(2 * d + 2) * GW])
            a = jnp.exp(-LRU_C * r * sp[d:d + 1, :])
            ab_s[2 * d, pl.ds(r0, LRU_CH), :] = a
            ab_s[2 * d + 1, pl.ds(r0, LRU_CH), :] = jnp.sqrt(1.0 - a * a) * (ig * xc)
        return carry

    lax.fori_loop(0, LRU_NCH, prepare, 0)

    def body(i, carry):
        hf, hb = carry
        rf = pl.multiple_of(((i + LRU_LAT_CH) % LRU_NCH) * LRU_CH, LRU_CH)
        rb = pl.multiple_of((LRU_LAT_CH - i) * LRU_CH, LRU_CH)
        h, hf = _scan_chunk(ab_s[0, pl.ds(rf, LRU_CH), :], ab_s[1, pl.ds(rf, LRU_CH), :], row_in_tile, hf, False)
        h_s[0, pl.ds(rf, LRU_CH), :] = h
        h, hb = _scan_chunk(ab_s[2, pl.ds(rb, LRU_CH), :], ab_s[3, pl.ds(rb, LRU_CH), :], row_in_tile, hb, True)
        h_s[1, pl.ds(rb, LRU_CH), :] = h
        return hf, hb

    zero = jnp.zeros((1, GW), F32)
    lax.fori_loop(0, LRU_NCH, body, (zero, zero))

    def finish(c, carry):
        r0 = pl.multiple_of(c * LRU_CH, LRU_CH)
        gate = u_ref[0, pl.ds(r0, LRU_CH), GW:2 * GW]
        hsum = h_s[0, pl.ds(r0, LRU_CH), :] + h_s[1, pl.ds(r0, LRU_CH), :]
        y_ref[0, pl.ds(r0, LRU_CH), :] = (hsum * _gelu_tanh(gate)).astype(BF16)
        return carry

    lax.fori_loop(0, LRU_NCH, finish, 0)


def _lru_call(ua, cw, cb, wri, bri, lam):
    nb = ua.shape[0]
    full = lambda shape: pl.BlockSpec(shape, lambda b: (0,) * len(shape))
    return pl.pallas_call(
        _lru_kernel,
        grid=(nb,),
        in_specs=[
            pl.BlockSpec((1, NT, UA_W), lambda b: (b, 0, 0)),
            full((CONV_W, GW)), full((1, GW)), full((GW, 4 * GW)), full((1, 4 * GW)), full((2, GW)),
        ],
        out_specs=pl.BlockSpec((1, NT, GW), lambda b: (b, 0, 0)),
        out_shape=jax.ShapeDtypeStruct((nb, NT, GW), BF16),
        scratch_shapes=[pltpu.VMEM((4, NT, GW), F32), pltpu.VMEM((2, NT, GW), F32)],
        compiler_params=_cparams(("parallel",)),
        name="rglru",
    )(ua, cw, cb, wri, bri, lam)


HG_NCH = NT // HG_CH
HG_LAT_CH = N_LAT // HG_CH
HG_LEVELS = HG_CH.bit_length() - 1
HG_EXP_BLOCKS = 1 + HG_LEVELS
HG_SPLIT = 3


def _hgrn_tables():
    t_ = HG_CH
    dm = np.zeros((2, HG_EXP_BLOCKS, t_, t_), np.float32)
    lm = np.zeros((2, HG_LEVELS + 1, t_, t_), np.float32)
    for d in (0, 1):
        for t in range(t_):
            if d == 0:
                dm[d, 0, t, :t + 1] = 1.0
            else:
                dm[d, 0, t, t:] = 1.0
            lm[d, 0, t, t] = 1.0
        for lev in range(1, HG_LEVELS + 1):
            m = 1 << (lev - 1)
            for t in range(t_):
                start = (t // (2 * m)) * 2 * m
                mid = start + m
                upper = t >= mid
                if d == 0:
                    if upper:
                        dm[d, lev, t, mid:t + 1] = 1.0
                        lm[d, lev, t, start:mid] = 1.0
                    else:
                        dm[d, lev, t, t + 1:mid] = 1.0
                else:
                    if not upper:
                        dm[d, lev, t, t:mid] = 1.0
                        lm[d, lev, t, mid:start + 2 * m] = 1.0
                    else:
                        dm[d, lev, t, mid:t] = 1.0
    dm = dm.reshape(2, HG_EXP_BLOCKS * t_, t_)
    d3 = np.concatenate([dm] * HG_SPLIT, axis=-1)
    lm = np.tile(lm, (1, 1, 1, HG_HEADS))
    heads = np.arange(GW) // HG_HD
    bm = (heads[:, None] == heads[None, :]).astype(np.float32)
    return jnp.asarray(d3, BF16), jnp.asarray(lm, F32), jnp.asarray(bm, F32), jnp.asarray(bm, BF16)


def _hgrn_kernel(u_ref, lb_ref, nw_ref, d3_ref, lm_ref, bm_ref, bmb_ref, y_ref, o_s, st_s, stb_s):
    t = HG_CH
    lb = lb_ref[...]
    st_s[...] = jnp.zeros(st_s.shape, F32)
    stb_s[...] = jnp.zeros(stb_s.shape, BF16)

    def by_head(xb):
        return jnp.concatenate([xb] * HG_HEADS, axis=0) * bmb_ref[...]

    def chunk(d, c):
        r0 = pl.multiple_of(c * t, t)
        q = _silu(u_ref[0, pl.ds(r0, t), 0:GW])
        fr = u_ref[0, pl.ds(r0, t), (1 + d) * GW:(2 + d) * GW]
        v = u_ref[0, pl.ds(r0, t), 3 * GW:4 * GW]
        f = lb + (1.0 - lb) * _sigmoid(fr)
        k = 1.0 - f
        ex = _dot(d3_ref[d], jnp.concatenate(_split_bf16(jnp.log2(f), HG_SPLIT), axis=0))
        cum = ex[0:t]
        a = _dot_nt(q.astype(BF16), by_head(k.astype(BF16))) * lm_ref[d, 0]
        for lev in range(1, HG_LEVELS + 1):
            fac = jnp.exp2(ex[lev * t:(lev + 1) * t])
            a = a + _dot_nt((q * fac).astype(BF16), by_head((k * fac).astype(BF16))) * lm_ref[d, lev]
        vb = v.astype(BF16)
        o = _dot(a.astype(BF16), by_head(vb)) + _dot_nt((q * jnp.exp2(cum)).astype(BF16), stb_s[d])
        o_s[d, pl.ds(r0, t), :] = o
        cend = cum[t - 1:t, :] if d == 0 else cum[0:1, :]
        kend = (k * jnp.exp2(cend - cum)).astype(BF16)
        st = st_s[d] * jnp.exp2(cend) + _dot_tn(vb, kend) * bm_ref[...]
        st_s[d] = st
        stb_s[d] = st.astype(BF16)

    def body(i, carry):
        chunk(0, (i + HG_LAT_CH) % HG_NCH)
        chunk(1, HG_NCH - 1 - i)
        return carry

    lax.fori_loop(0, HG_NCH, body, 0, unroll=2)

    def finish(c, carry):
        r0 = pl.multiple_of(c * t, t)
        osum = o_s[0, pl.ds(r0, t), :] + o_s[1, pl.ds(r0, t), :]
        sq = _split_bf16(osum * osum, 2)
        ms = _dot(jnp.concatenate(sq, axis=1), jnp.concatenate([bmb_ref[...]] * 2, axis=0)) * (1.0 / HG_HD)
        g = u_ref[0, pl.ds(r0, t), 4 * GW:5 * GW]
        y_ref[0, pl.ds(r0, t), :] = (osum * lax.rsqrt(ms + EPS) * nw_ref[...] * _silu(g)).astype(BF16)
        return carry

    lax.fori_loop(0, HG_NCH, finish, 0)


def _hgrn_call(ub, lb, nw, tables):
    nb = ub.shape[0]
    d3, lm, bm, bmb = tables
    full = lambda shape: pl.BlockSpec(shape, lambda b: (0,) * len(shape))
    return pl.pallas_call(
        _hgrn_kernel,
        grid=(nb,),
        in_specs=[pl.BlockSpec((1, NT, UB_W), lambda b: (b, 0, 0)), full((1, GW)), full((1, GW)),
                  full(d3.shape), full(lm.shape), full(bm.shape), full(bmb.shape)],
        out_specs=pl.BlockSpec((1, NT, GW), lambda b: (b, 0, 0)),
        out_shape=jax.ShapeDtypeStruct((nb, NT, GW), BF16),
        scratch_shapes=[pltpu.VMEM((2, NT, GW), F32), pltpu.VMEM((2, GW, GW), F32), pltpu.VMEM((2, GW, GW), BF16)],
        compiler_params=_cparams(("parallel",)),
        name="hgrn2",
    )(ub, lb, nw, d3, lm, bm, bmb)


SSD_NCH = NT // SSD_CH
SSD_LAT_CH = N_LAT // SSD_CH
SSD_DT0 = GW + SSD_XBC
GRP_W = SSD_HEADS // SSD_GROUPS * SSD_HD


SSD_BC_W = 2 * SSD_GROUPS * SSD_STATE
SSD_CUM_SPLIT = 3
SSD_GAIN_SPLIT = 2


def _ssd_tables():
    t = SSD_CH
    tri = np.stack([np.tril(np.ones((t, t), np.float32)), np.triu(np.ones((t, t), np.float32))])
    col = np.zeros((2, LANES, SSD_HEADS * LANES), np.float32)
    wide = np.zeros((2, LANES, GW), np.float32)
    for d in range(2):
        for h in range(SSD_HEADS):
            col[d, d * SSD_HEADS + h, h * LANES:(h + 1) * LANES] = 1.0
            wide[d, d * SSD_HEADS + h, h * SSD_HD:(h + 1) * SSD_HD] = 1.0
    hm = (np.arange(GW)[None, :] // SSD_HD == np.arange(SSD_HEADS)[:, None]).astype(np.float32)
    gm = (np.arange(LANES)[None, :] // SSD_STATE == np.arange(SSD_GROUPS)[:, None]).astype(np.float32)
    return (jnp.asarray(np.concatenate([tri] * SSD_CUM_SPLIT, axis=2), BF16),
            jnp.asarray(np.concatenate([col] * SSD_CUM_SPLIT, axis=1), BF16),
            jnp.asarray(np.concatenate([wide] * SSD_GAIN_SPLIT, axis=1), BF16),
            jnp.asarray(hm[:, None, :], BF16), jnp.asarray(gm[:, None, :], BF16))


def _ssd_kernel(u_ref, cw_ref, cb_ref, a8_ref, b8_ref, dsk_ref, nw_ref, tri_ref, col_ref, wide_ref, hm_ref, gm_ref,
                y_ref, xs_s, bc_s, xm_s, dt_s, y_s, st_s):
    t = SSD_CH
    rr = lax.broadcasted_iota(jnp.int32, (t, t), 0)
    cc = lax.broadcasted_iota(jnp.int32, (t, t), 1)
    keeps = (rr >= cc, rr <= cc)
    a8 = -jnp.exp(a8_ref[...])

    def prepare(c, carry):
        r0 = pl.multiple_of(c * t, t)
        seg_start = (c == 0) | (c == SSD_LAT_CH)
        seg_end = (c == SSD_LAT_CH - 1) | (c == SSD_NCH - 1)
        xbc = _silu(_conv_chunk(u_ref, r0, t, GW, GW + SSD_XBC, cw_ref, cb_ref, seg_start, seg_end))
        xs = xbc[:, 0:GW]
        xs_s[pl.ds(r0, t), :] = xs
        bc_s[pl.ds(r0, t), :] = xbc[:, GW:GW + SSD_BC_W].astype(BF16)
        xb = xs.astype(BF16)
        for h in range(SSD_HEADS):
            xm_s[h, pl.ds(r0, t), :] = xb * hm_ref[h]
        dt_s[pl.ds(r0, t), :] = _softplus(u_ref[0, pl.ds(r0, t), SSD_DT0:SSD_DT0 + LANES] + b8_ref[...])
        return carry

    lax.fori_loop(0, SSD_NCH, prepare, 0)
    st_s[...] = jnp.zeros(st_s.shape, F32)

    def chunk(d, c):
        r0 = pl.multiple_of(c * t, t)
        dtn = dt_s[pl.ds(r0, t), :]
        cumn = _dot(tri_ref[d], jnp.concatenate(_split_bf16(dtn * a8, SSD_CUM_SPLIT), axis=0))
        cend = cumn[t - 1:t, :] if d == 0 else cumn[0:1, :]
        cum_t = cumn.T
        dt_t = dtn.T
        colb = _dot(jnp.concatenate(_split_bf16(cumn, SSD_CUM_SPLIT), axis=1), col_ref[d])
        gains = jnp.concatenate([jnp.exp(cumn), dtn * jnp.exp(cend - cumn)], axis=0)
        gw = _dot(jnp.concatenate(_split_bf16(gains, SSD_GAIN_SPLIT), axis=1), wide_ref[d])
        ecum, wend = gw[0:t], gw[t:2 * t]
        bc = bc_s[pl.ds(r0, t), :]
        bmat, cmat = bc[:, 0:LANES], bc[:, LANES:2 * LANES]
        st = st_s[d]
        y = jnp.zeros((t, GW), F32)
        y_in = []
        for g in range(SSD_GROUPS):
            cg = cmat * gm_ref[g]
            scores = _dot_nt(cg, bmat)
            sg = st[:, g * GRP_W:(g + 1) * GRP_W].astype(BF16)
            y_in.append(_dot(cg, jnp.concatenate([sg] * SSD_GROUPS, axis=0)))
            for hh in range(SSD_HEADS // SSD_GROUPS):
                h = g * (SSD_HEADS // SSD_GROUPS) + hh
                row = d * SSD_HEADS + h
                seg = colb[:, h * LANES:(h + 1) * LANES] - cum_t[row:row + 1, :]
                m = jnp.where(keeps[d], jnp.exp(seg), 0.0) * scores * dt_t[row:row + 1, :]
                y = y + _dot(m.astype(BF16), xm_s[h, pl.ds(r0, t), :])
        y_s[d, pl.ds(r0, t), :] = y + jnp.concatenate(y_in, axis=1) * ecum
        xw = (xs_s[pl.ds(r0, t), :] * wend).astype(BF16)
        upd = [_dot_tn(bmat, xw[:, g * GRP_W:(g + 1) * GRP_W])[g * SSD_STATE:(g + 1) * SSD_STATE]
               for g in range(SSD_GROUPS)]
        eend = ecum[t - 1:t, :] if d == 0 else ecum[0:1, :]
        st_s[d] = st * eend + jnp.concatenate(upd, axis=1)

    def body(i, carry):
        chunk(0, (i + SSD_LAT_CH) % SSD_NCH)
        chunk(1, SSD_NCH - 1 - i)
        return carry

    lax.fori_loop(0, SSD_NCH, body, 0, unroll=2)

    def finish(c, carry):
        r0 = pl.multiple_of(c * t, t)
        z = u_ref[0, pl.ds(r0, t), 0:GW]
        yy = (y_s[0, pl.ds(r0, t), :] + y_s[1, pl.ds(r0, t), :] + dsk_ref[...] * xs_s[pl.ds(r0, t), :]) * _silu(z)
        ms = jnp.mean(yy * yy, axis=-1, keepdims=True)
        y_ref[0, pl.ds(r0, t), :] = (yy * lax.rsqrt(ms + EPS) * nw_ref[...]).astype(BF16)
        return carry

    lax.fori_loop(0, SSD_NCH, finish, 0)


def _ssd_call(us, cw, cb, a8, b8, dsk, nw, tables):
    nb = us.shape[0]
    full = lambda shape: pl.BlockSpec(shape, lambda b: (0,) * len(shape))
    return pl.pallas_call(
        _ssd_kernel,
        grid=(nb,),
        in_specs=[
            pl.BlockSpec((1, NT, US_W), lambda b: (b, 0, 0)),
            full((CONV_W, SSD_XBC)), full((1, SSD_XBC)), full((1, LANES)), full((1, LANES)),
            full((1, GW)), full((1, GW)),
        ] + [full(tb.shape) for tb in tables],
        out_specs=pl.BlockSpec((1, NT, GW), lambda b: (b, 0, 0)),
        out_shape=jax.ShapeDtypeStruct((nb, NT, GW), BF16),
        scratch_shapes=[pltpu.VMEM((NT, GW), F32), pltpu.VMEM((NT, SSD_BC_W), BF16),
                        pltpu.VMEM((SSD_HEADS, NT, GW), BF16), pltpu.VMEM((NT, LANES), F32),
                        pltpu.VMEM((2, NT, GW), F32), pltpu.VMEM((2, SSD_STATE, GW), F32)],
        compiler_params=_cparams(("parallel",)),
        name="ssd",
    )(us, cw, cb, a8, b8, dsk, nw, *tables)


HB = LANES
QW = MLA_HEADS * HB
MLA_QB = 256
MLA_AB = 256
MLA_SCALE = (MLA_NOPE + MLA_ROPE) ** -0.5


def _mla_kernel(need_ctx, u_ref, qa_ref, wq_ref, qn_ref, kva_ref, wkv_ref, kn_ref, krw_ref,
                cos_ref, sa_ref, sb_ref, y_ref, q_s, k_s, v_s):
    lane = lax.broadcasted_iota(jnp.int32, (QW, QW), 1)
    row = lax.broadcasted_iota(jnp.int32, (QW, QW), 0)

    def grp(i):
        within = i & (HB - 1)
        return (i >> HB_SHIFT) * 3 + jnp.where(within < MLA_NOPE, 0, jnp.where(within < MLA_NOPE + MLA_ROPE, 1, 2))

    gmat = jnp.where(grp(row) == grp(lane), 1.0, 0.0).astype(BF16)
    l1 = lax.broadcasted_iota(jnp.int32, (1, QW), 1) & (HB - 1)
    inv_size = jnp.where(l1 < MLA_NOPE, 1.0 / MLA_NOPE, 1.0 / MLA_ROPE)
    vlane_head = lax.broadcasted_iota(jnp.int32, (MLA_QB, GW), 1) >> HEAD_SHIFT

    def rope(x, r0, reps):
        cosr = cos_ref[pl.ds(r0, MLA_QB), :]
        sar = sa_ref[pl.ds(r0, MLA_QB), :]
        sbr = sb_ref[pl.ds(r0, MLA_QB), :]
        if reps > 1:
            cosr, sar, sbr = (jnp.concatenate([tbl] * reps, axis=1) for tbl in (cosr, sar, sbr))
        w = x.shape[1]
        return x * cosr + pltpu.roll(x, w - MLA_ROPE // 2, 1) * sar + pltpu.roll(x, MLA_ROPE // 2, 1) * sbr

    def project(ci, carry):
        r0 = pl.multiple_of(ci * MLA_QB, MLA_QB)
        cq = u_ref[0, pl.ds(r0, MLA_QB), 0:256]
        ms = jnp.sum(cq * cq, axis=-1, keepdims=True) * (1.0 / MLA_QR)
        qraw = _dot((cq * lax.rsqrt(ms + EPS) * qa_ref[...]).astype(BF16), wq_ref[...])
        ss = _dot((qraw * qraw).astype(BF16), gmat) * inv_size
        q = rope(qraw * lax.rsqrt(ss + EPS) * qn_ref[...], r0, MLA_HEADS) * MLA_SCALE
        q_s[pl.ds(r0, MLA_QB), :] = q.astype(BF16)
        ckv = u_ref[0, pl.ds(r0, MLA_QB), 256:256 + MLA_KVR]
        ms = jnp.mean(ckv * ckv, axis=-1, keepdims=True)
        kv = _dot((ckv * lax.rsqrt(ms + EPS) * kva_ref[...]).astype(BF16), wkv_ref[...])
        kraw = kv[:, 0:QW]
        ss = _dot((kraw * kraw).astype(BF16), gmat) * inv_size
        knope = kraw * lax.rsqrt(ss + EPS) * kn_ref[...]
        kr = u_ref[0, pl.ds(r0, MLA_QB), 384:512]
        ms = jnp.sum(kr * kr, axis=-1, keepdims=True) * (1.0 / MLA_ROPE)
        krope = rope(kr * lax.rsqrt(ms + EPS) * krw_ref[...], r0, 1)
        k_s[pl.ds(r0, MLA_QB), :] = (knope + jnp.concatenate([krope] * MLA_HEADS, axis=1)).astype(BF16)
        vv = kv[:, QW:QW + GW]
        for h in range(MLA_HEADS):
            v_s[h, pl.ds(r0, MLA_QB), :] = jnp.where(vlane_head == h, vv, 0.0).astype(BF16)
        return carry

    lax.fori_loop(0, NT // MLA_QB, project, 0)

    def attend(r0, rows, k0, klen):
        o = jnp.zeros((rows, GW), F32)
        for h in range(MLA_HEADS):
            qh = q_s[pl.ds(r0, rows), h * HB:(h + 1) * HB]
            s = _dot_nt(qh, k_s[k0:k0 + klen, h * HB:(h + 1) * HB])
            p = jnp.exp(s - jnp.max(s, axis=-1, keepdims=True))
            l = jnp.sum(p, axis=-1, keepdims=True)
            o = o + _dot(p.astype(BF16), v_s[h, k0:k0 + klen, :]) * (1.0 / l)
        y_ref[0, pl.ds(r0, rows), :] = o.astype(BF16)

    def lat_block(qi, carry):
        attend(pl.multiple_of(qi * MLA_AB, MLA_AB), MLA_AB, 0, NT)
        return carry

    lax.fori_loop(0, N_LAT // MLA_AB, lat_block, 0)
    if need_ctx:
        attend(N_LAT, N_CTX, N_LAT, N_CTX)
    else:
        y_ref[0, N_LAT:NT, :] = jnp.zeros((N_CTX, GW), BF16)


def _mla_call(um, need_ctx, qa, wq, qn, kva, wkv, kn, krw, cos_t, sa_t, sb_t):
    nb = um.shape[0]
    full = lambda shape: pl.BlockSpec(shape, lambda b: (0,) * len(shape))
    return pl.pallas_call(
        functools.partial(_mla_kernel, need_ctx),
        grid=(nb,),
        in_specs=[
            pl.BlockSpec((1, NT, UM_W), lambda b: (b, 0, 0)),
            full((1, 256)), full((256, QW)), full((1, QW)),
            full((1, MLA_KVR)), full((MLA_KVR, QW + GW)), full((1, QW)), full((1, HB)),
            full((NT, HB)), full((NT, HB)), full((NT, HB)),
        ],
        out_specs=pl.BlockSpec((1, NT, GW), lambda b: (b, 0, 0)),
        out_shape=jax.ShapeDtypeStruct((nb, NT, GW), BF16),
        scratch_shapes=[pltpu.VMEM((NT, QW), BF16), pltpu.VMEM((NT, QW), BF16),
                        pltpu.VMEM((MLA_HEADS, NT, GW), BF16)],
        compiler_params=_cparams(("parallel",)),
        name="mla",
    )(um, qa, wq, qn, kva, wkv, kn, krw, cos_t, sa_t, sb_t)


def _out_kernel(split, *refs):
    if split:
        x_ref, c_ref, ya_ref, yb_ref, ys_ref, ym_ref, wo_ref, mod_ref, nw_ref, wr_ref, xo_ref, h2_ref, aff_ref = refs
        x_in = jnp.where(pl.program_id(1) < LAT_TILES, x_ref[0], c_ref[0])
    else:
        x_ref, ya_ref, yb_ref, ys_ref, ym_ref, wo_ref, mod_ref, nw_ref, wr_ref, xo_ref, h2_ref, aff_ref = refs
        x_in = x_ref[0]
    y = jnp.concatenate([ya_ref[0], yb_ref[0], ys_ref[0], ym_ref[0]], axis=1)
    x = x_in + mod_ref[0, 2:3, :] * _dot(y, wo_ref[...])
    xo_ref[0] = x
    h2 = _norm_mod(x, nw_ref[...], mod_ref[0, 3:4, :], mod_ref[0, 4:5, :])
    hi = h2.astype(BF16)
    h2_ref[0] = hi
    lo = (h2 - hi.astype(F32)).astype(BF16)
    wr = wr_ref[...]
    w_hi = wr.astype(BF16)
    w_lo = (wr - w_hi.astype(F32)).astype(BF16)
    logit = _dot_nt(jnp.concatenate([w_hi, w_hi, w_lo], axis=1), jnp.concatenate([hi, lo, hi], axis=1))
    e = jnp.exp(logit - jnp.max(logit, axis=0, keepdims=True))
    aff_ref[0] = e / jnp.sum(e, axis=0, keepdims=True)


def _out_call(x, ctx, ys, wo, modt, nw, wr_t, n_tiles):
    nb = x.shape[0]
    tok = lambda w: pl.BlockSpec((1, ROW_TILE, w), lambda b, t: (b, t, 0))
    x_args = (x,) if ctx is None else (x, ctx)
    return pl.pallas_call(
        functools.partial(_out_kernel, ctx is not None),
        grid=(nb, n_tiles),
        in_specs=_token_specs(ctx) + [tok(GW), tok(GW), tok(GW), tok(GW),
                  pl.BlockSpec((D, D), lambda b, t: (0, 0)),
                  pl.BlockSpec((1, N_MOD, D), lambda b, t: _mod_index(b, t, nb)),
                  pl.BlockSpec((1, D), lambda b, t: (0, 0)),
                  pl.BlockSpec((N_EXP, D), lambda b, t: (0, 0))],
        out_specs=[tok(D), tok(D), pl.BlockSpec((1, N_EXP, ROW_TILE), lambda b, t: (b, 0, t))],
        out_shape=[jax.ShapeDtypeStruct((nb, n_tiles * ROW_TILE, D), F32),
                   jax.ShapeDtypeStruct((nb, n_tiles * ROW_TILE, D), BF16),
                   jax.ShapeDtypeStruct((nb, N_EXP, n_tiles * ROW_TILE), F32)],
        compiler_params=_cparams(("parallel", "parallel")),
        name="out_proj",
    )(*x_args, *ys, wo, modt, nw, wr_t)


PRE_W = 256
ROUTE_MAX_IT = 160


def _prefix_count(m, tri_bf):
    n = m.shape[1]
    off = jnp.zeros((m.shape[0], 1), F32)
    outs = []
    for j in range(n // PRE_W):
        blk = m[:, j * PRE_W:(j + 1) * PRE_W]
        outs.append(_dot(blk.astype(BF16), tri_bf) + off)
        off = off + jnp.sum(blk, axis=1, keepdims=True)
    return outs[0] if len(outs) == 1 else jnp.concatenate(outs, axis=1)


def _route(aff, cap, tri):
    n_lo0 = jnp.full((N_EXP, 1), float(aff.shape[1]), F32)

    def cond(s):
        it, _, _, n_lo, n_hi = s
        return jnp.logical_and(it < ROUTE_MAX_IT, jnp.max(n_lo - n_hi) > 1.0)

    def step(s):
        it, lo, hi, n_lo, n_hi = s
        width = hi - lo
        mids = [lo + frac * width for frac in (0.25, 0.5, 0.75)]
        cnts = [jnp.sum(jnp.where(aff > m, 1.0, 0.0), axis=1, keepdims=True) for m in mids]
        blws = [jnp.max(jnp.where(aff <= m, aff, -1.0), axis=1, keepdims=True) for m in mids]
        u0, u1, u2 = (c >= cap for c in cnts)
        pick = lambda a3, a2, a1, a0: jnp.where(u2, a3, jnp.where(u1, a2, jnp.where(u0, a1, a0)))
        return (it + 1, pick(mids[2], mids[1], mids[0], lo), pick(hi, blws[2], blws[1], blws[0]),
                pick(cnts[2], cnts[1], cnts[0], n_lo), pick(n_hi, cnts[2], cnts[1], cnts[0]))

    init = (jnp.int32(0), jnp.full((N_EXP, 1), -1.0, F32), jnp.max(aff, axis=1, keepdims=True),
            n_lo0, jnp.zeros((N_EXP, 1), F32))
    _, _, thr, _, n_gt = lax.while_loop(cond, step, init)
    gt = jnp.where(aff > thr, 1.0, 0.0)
    eq = jnp.where(aff == thr, 1.0, 0.0)
    sel = gt + eq * jnp.where(_prefix_count(eq, tri) < cap - n_gt, 1.0, 0.0)
    return jnp.where(sel > 0.0, _prefix_count(sel, tri), -1.0)


MOE_EPS = 2


def _moe_kernel(segs, x_hbm, aff_ref, h_ref, gl_ref, gc_ref, wg_ref, wu_ref, wd_ref, o_ref, pos_ref, sem):
    b = pl.program_id(0)
    step = pl.program_id(1)
    rows = o_ref.shape[1]

    def residual_copy():
        return pltpu.make_async_copy(x_hbm.at[b, pl.ds(0, rows), :], o_ref.at[0], sem)

    @pl.when(step == 0)
    def _init():
        residual_copy().start()
        tri = jnp.where(lax.broadcasted_iota(jnp.int32, (PRE_W, PRE_W), 0)
                        < lax.broadcasted_iota(jnp.int32, (PRE_W, PRE_W), 1), 1.0, 0.0).astype(BF16)
        for r0, n, cap in segs:
            pos_ref[:, r0:r0 + n] = _route(aff_ref[0, :, r0:r0 + n], cap, tri)
        residual_copy().wait()

    onehots, gates, xs = [], [], []
    for (r0, n, cap), g_ref in zip(segs, (gl_ref, gc_ref)):
        slot = lax.broadcasted_iota(jnp.int32, (cap, n), 0).astype(F32)
        oh, gt = [], []
        for j in range(MOE_EPS):
            e = step * MOE_EPS + j
            hit = slot == pos_ref[pl.ds(e, 1), r0:r0 + n]
            oh.append(jnp.where(hit, 1.0, 0.0).astype(BF16))
            gsel = jnp.sum(jnp.where(hit, aff_ref[0, pl.ds(e, 1), r0:r0 + n], 0.0), axis=1, keepdims=True)
            gt.append(gsel * g_ref[0, N_MOD - 1:N_MOD, :])
        onehot = jnp.concatenate(oh, axis=0)
        onehots.append(onehot)
        gates.append(gt)
        xs.append(_dot(onehot, h_ref[0, r0:r0 + n, :]).astype(BF16))
    ys = []
    for j in range(MOE_EPS):
        xj = [x[j * cap:(j + 1) * cap] for x, (_, _, cap) in zip(xs, segs)]
        xj = xj[0] if len(xj) == 1 else jnp.concatenate(xj, axis=0)
        act = (_silu(_dot(xj, wg_ref[j])) * _dot(xj, wu_ref[j])).astype(BF16)
        ys.append(_dot(act, wd_ref[j]))
    s0 = 0
    for (r0, n, cap), onehot, gt in zip(segs, onehots, gates):
        ysg = jnp.concatenate([(ys[j][s0:s0 + cap] * gt[j]).astype(BF16) for j in range(MOE_EPS)], axis=0)
        o_ref[0, r0:r0 + n, :] += _dot_tn(onehot, ysg)
        s0 += cap


def _moe_call(x_mid, aff, h2, modt, wg, wu, wd, with_ctx):
    nb = h2.shape[0]
    segs = ((0, N_LAT, EC_FACTOR * N_LAT // N_EXP),)
    if with_ctx:
        segs += ((N_LAT, N_CTX, EC_FACTOR * N_CTX // N_EXP),)
    rows = NT if with_ctx else N_LAT
    return pl.pallas_call(
        functools.partial(_moe_kernel, segs),
        grid=(nb, N_EXP // MOE_EPS),
        in_specs=[
            pl.BlockSpec(memory_space=pl.ANY),
            pl.BlockSpec((1, N_EXP, rows), lambda b, e: (b, 0, 0)),
            pl.BlockSpec((1, rows, D), lambda b, e: (b, 0, 0)),
            pl.BlockSpec((1, N_MOD, D), lambda b, e: (b, 0, 0)),
            pl.BlockSpec((1, N_MOD, D), lambda b, e: (nb, 0, 0)),
            pl.BlockSpec((MOE_EPS, D, FF), lambda b, e: (e, 0, 0)),
            pl.BlockSpec((MOE_EPS, D, FF), lambda b, e: (e, 0, 0)),
            pl.BlockSpec((MOE_EPS, FF, D), lambda b, e: (e, 0, 0)),
        ],
        out_specs=pl.BlockSpec((1, rows, D), lambda b, e: (b, 0, 0)),
        out_shape=jax.ShapeDtypeStruct((nb, rows, D), F32),
        scratch_shapes=[pltpu.VMEM((N_EXP, rows), F32), pltpu.SemaphoreType.DMA(())],
        compiler_params=_cparams(("parallel", "arbitrary")),
        name="moe",
    )(x_mid, aff, h2, modt, modt, wg, wu, wd)


def _pad_cols(w, width):
    return jnp.pad(w, ((0, 0), (0, width - w.shape[1])))


def _rope_perm():
    half = MLA_ROPE // 2
    return jnp.concatenate([jnp.arange(half) * 2, jnp.arange(half) * 2 + 1])


def _head_block(nope, rope):
    pad = jnp.zeros(nope.shape[:-1] + (HB - MLA_NOPE - MLA_ROPE,), nope.dtype)
    blk = jnp.concatenate([nope, rope, pad], axis=-1)
    return blk.reshape(blk.shape[:-2] + (MLA_HEADS * HB,))


def _prep_in_weights(w_in):
    o1, o2, o3 = LRU_COLS, LRU_COLS + HG_COLS, LRU_COLS + HG_COLS + SSD_COLS
    wa, wb, ws, wm = w_in[:, :o1], w_in[:, o1:o2], w_in[:, o2:o3], w_in[:, o3:]
    perm = _rope_perm()
    cq, ckv, kr = wm[:, :MLA_QR], wm[:, MLA_QR:MLA_QR + MLA_KVR], wm[:, MLA_QR + MLA_KVR:]
    zeros = lambda n: jnp.zeros((D, n), w_in.dtype)
    wm_p = jnp.concatenate([cq, zeros(256 - MLA_QR), ckv, zeros(MLA_NOPE), kr[:, perm],
                            zeros(HB - MLA_NOPE - MLA_ROPE)], axis=1)
    return jnp.concatenate([wa, wb, _pad_cols(ws, US_W), wm_p], axis=1).astype(BF16)


def _block_diag(w):
    h, dd, _ = w.shape
    eye = jnp.eye(h, dtype=w.dtype)
    return (eye[:, None, :, None] * w[:, :, None, :]).reshape(h * dd, h * dd)


def _rope_tables():
    rows = N_LAT // GRID_W
    row = jnp.repeat(jnp.arange(rows, dtype=F32), GRID_W)
    col = jnp.tile(jnp.arange(GRID_W, dtype=F32), rows)
    half = MLA_ROPE // 2
    inv = ROPE_BASE ** (-jnp.arange(0, half, 2, dtype=F32) / half)
    ang = jnp.concatenate([row[:, None] * inv, col[:, None] * inv], axis=-1)
    cos, sin = jnp.cos(ang), jnp.sin(ang)
    z = lambda n: jnp.zeros((N_LAT, n), F32)
    o = lambda n: jnp.ones((N_LAT, n), F32)
    cos_t = jnp.concatenate([o(MLA_NOPE), cos, cos, o(HB - MLA_NOPE - MLA_ROPE)], axis=1)
    sa_t = jnp.concatenate([z(MLA_NOPE), -sin, z(half), z(HB - MLA_NOPE - MLA_ROPE)], axis=1)
    sb_t = jnp.concatenate([z(MLA_NOPE), z(half), sin, z(HB - MLA_NOPE - MLA_ROPE)], axis=1)
    cos_t = jnp.concatenate([cos_t, jnp.ones((N_CTX, HB), F32)], axis=0)
    sa_t = jnp.concatenate([sa_t, jnp.zeros((N_CTX, HB), F32)], axis=0)
    sb_t = jnp.concatenate([sb_t, jnp.zeros((N_CTX, HB), F32)], axis=0)
    return cos_t, sa_t, sb_t


def kernel(x, c, ctx, c_ctx, ada_w, ada_b, norm1_w, norm2_w, w_in, w_out, lru_conv_w, lru_conv_b, lru_w_r, lru_b_r, lru_w_i, lru_b_i, lru_lam, hgrn_lb_logits, hgrn_norm_w, ssd_conv_w, ssd_conv_b, ssd_a_log, ssd_dt_bias, ssd_d_skip, ssd_norm_w, mla_q_a_norm, mla_w_q_up, mla_kv_a_norm, mla_w_kv_up, mla_q_norm, mla_k_norm, moe_router, moe_w_gate, moe_w_up, moe_w_down):
    nb = x.shape[0]
    assert x.shape == (nb, N_LAT, D) and ctx.shape == (nb, N_CTX, D)
    assert nb + 1 <= ADA_ROWS
    cc = jnp.concatenate([c, c_ctx[None, :], jnp.zeros((ADA_ROWS - nb - 1, D), F32)], axis=0)
    mod_all = _ada_call(cc, ada_w, ada_b)[:, :nb + 1].reshape(DEPTH, nb + 1, N_MOD, D)
    cos_t, sa_t, sb_t = _rope_tables()
    hg_tables = _hgrn_tables()
    ssd_tables = _ssd_tables()
    perm = _rope_perm()
    lb_w = jax.nn.softmax(hgrn_lb_logits.astype(F32), axis=0)
    lb_all = jnp.cumsum(lb_w, axis=0) - lb_w[0]
    rep = lambda v, n: jnp.repeat(v, n, axis=-1)
    stream = (x, ctx)
    for l in range(DEPTH):
        need_ctx = l < DEPTH - 1
        modt = mod_all[l]
        ua, ub, us, um = _in_call(*stream, modt, norm1_w[l][None, :], _prep_in_weights(w_in[l]))
        wri = jnp.concatenate([_block_diag(lru_w_r[l, 0]), _block_diag(lru_w_i[l, 0]),
                               _block_diag(lru_w_r[l, 1]), _block_diag(lru_w_i[l, 1])], axis=1).astype(BF16)
        bri = jnp.concatenate([lru_b_r[l, 0], lru_b_i[l, 0], lru_b_r[l, 1], lru_b_i[l, 1]])[None, :]
        ya = _lru_call(ua, lru_conv_w[l], lru_conv_b[l][None, :], wri, bri, lru_lam[l])
        yb = _hgrn_call(ub, lb_all[l][None, :], hgrn_norm_w[l][None, :], hg_tables)
        narrow = lambda v: jnp.pad(v.reshape(-1), (0, LANES - 2 * SSD_HEADS))[None, :]
        ysd = _ssd_call(us, ssd_conv_w[l], ssd_conv_b[l][None, :], narrow(ssd_a_log[l]), narrow(ssd_dt_bias[l]),
                        rep(ssd_d_skip[l], SSD_HD)[None, :], ssd_norm_w[l][None, :], ssd_tables)
        wq = mla_w_q_up[l].reshape(MLA_QR, MLA_HEADS, MLA_NOPE + MLA_ROPE)
        wq = _head_block(wq[..., :MLA_NOPE], wq[..., MLA_NOPE:][..., perm])
        wq = jnp.pad(wq, ((0, 256 - MLA_QR), (0, 0))).astype(BF16)
        qn = _head_block(jnp.broadcast_to(mla_q_norm[l][:MLA_NOPE], (MLA_HEADS, MLA_NOPE)),
                         jnp.broadcast_to(mla_q_norm[l][MLA_NOPE:][perm], (MLA_HEADS, MLA_ROPE)))[None, :]
        wkv = mla_w_kv_up[l].reshape(MLA_KVR, MLA_HEADS, MLA_NOPE + MLA_V)
        wk = _head_block(wkv[..., :MLA_NOPE], jnp.zeros((MLA_KVR, MLA_HEADS, MLA_ROPE), F32))
        wv = wkv[..., MLA_NOPE:].reshape(MLA_KVR, GW)
        wkv_p = jnp.concatenate([wk, wv], axis=1).astype(BF16)
        kn = _head_block(jnp.broadcast_to(mla_k_norm[l][:MLA_NOPE], (MLA_HEADS, MLA_NOPE)),
                         jnp.zeros((MLA_HEADS, MLA_ROPE), F32))[None, :]
        krw = jnp.concatenate([jnp.zeros((MLA_NOPE,), F32), mla_k_norm[l][MLA_NOPE:][perm],
                               jnp.zeros((HB - MLA_NOPE - MLA_ROPE,), F32)])[None, :]
        qa = jnp.pad(mla_q_a_norm[l], (0, 256 - MLA_QR))[None, :]
        ym = _mla_call(um, need_ctx, qa, wq, qn, mla_kv_a_norm[l][None, :], wkv_p, kn, krw, cos_t, sa_t, sb_t)
        n_tiles = N_TILES if need_ctx else LAT_TILES
        x_mid, h2, aff = _out_call(*stream, (ya, yb, ysd, ym), w_out[l].astype(BF16), modt, norm2_w[l][None, :],
                                   moe_router[l].T, n_tiles)
        wg, wu, wd = moe_w_gate[l].astype(BF16), moe_w_up[l].astype(BF16), moe_w_down[l].astype(BF16)
        stream = (_moe_call(x_mid, aff, h2, modt, wg, wu, wd, need_ctx), None)
    return stream[0]
```

```python
import functools
import math

import jax
import jax.numpy as jnp
import numpy as np
from jax import lax
from jax.experimental import pallas as pl
from jax.experimental.pallas import tpu as pltpu

F32 = jnp.float32
BF16 = jnp.bfloat16
HI = lax.Precision.HIGHEST

D = 1024
DEPTH = 2
N_LAT = 2048
N_CTX = 256
NT = N_LAT + N_CTX
GRID_W = 64
GW = 256
CONV_W = 4
LRU_HEADS, LRU_HD, LRU_C = 4, 64, 8.0
HG_HEADS, HG_HD = 4, 64
SSD_HEADS, SSD_HD, SSD_GROUPS, SSD_STATE = 4, 64, 2, 64
SSD_XBC = GW + 2 * SSD_GROUPS * SSD_STATE
MLA_HEADS, MLA_QR, MLA_KVR, MLA_NOPE, MLA_ROPE = 4, 192, 128, 64, 32
MLA_V = GW // MLA_HEADS
ROPE_BASE = 10000.0
N_EXP, FF, EC_FACTOR = 16, 512, 2
N_MOD = 6
EPS = 1e-6
LRU_COLS = 2 * GW
HG_COLS = 5 * GW
SSD_COLS = GW + SSD_XBC + 2 * SSD_HEADS
MLA_COLS = MLA_QR + MLA_KVR + MLA_ROPE

LANES = 128
SUBLANES = 8
ROW_TILE = 256
N_TILES = NT // ROW_TILE
LAT_TILES = N_LAT // ROW_TILE
UA_W, UB_W, US_W, UM_W = 512, 1280, 896, 512
U_OFFS = (0, UA_W, UA_W + UB_W, UA_W + UB_W + US_W, UA_W + UB_W + US_W + UM_W)
SSD_CH = 128
HG_CH = 64
VMEM_LIMIT = 52 * 1024 * 1024
HEAD_SHIFT = 6
HB_SHIFT = 7
assert LRU_HD == HG_HD == SSD_HD == MLA_V == 1 << HEAD_SHIFT and LANES == 1 << HB_SHIFT


def _cparams(sem):
    return pltpu.CompilerParams(dimension_semantics=sem, vmem_limit_bytes=VMEM_LIMIT)


def _sigmoid(x):
    return 1.0 / (1.0 + jnp.exp(-x))


def _silu(x):
    return x * _sigmoid(x)


def _softplus(x):
    return jnp.maximum(x, 0.0) + jnp.log(1.0 + jnp.exp(-jnp.abs(x)))


def _gelu_tanh(x):
    return 0.5 * x * (1.0 + jnp.tanh(math.sqrt(2.0 / math.pi) * (x + 0.044715 * (x * x * x))))


def _dot(a, b, **kw):
    return jnp.dot(a, b, preferred_element_type=F32, **kw)


def _dot_nt(a, b, **kw):
    return lax.dot_general(a, b, (((1,), (1,)), ((), ())), preferred_element_type=F32, **kw)


def _dot_tn(a, b, **kw):
    return lax.dot_general(a, b, (((0,), (0,)), ((), ())), preferred_element_type=F32, **kw)


def _split_bf16(x, n):
    parts, rest = [], x
    for _ in range(n):
        p = rest.astype(BF16)
        parts.append(p)
        rest = rest - p.astype(F32)
    return parts


ADA_ROWS = 24
ADA_TN = 512


def _ada_kernel(c_ref, w_ref, b_ref, o_ref):
    s = _silu(c_ref[...])
    o_ref[0] = _dot(s, w_ref[0], precision=HI) + b_ref[0]


def _ada_call(cc, ada_w, ada_b):
    return pl.pallas_call(
        _ada_kernel,
        grid=(DEPTH, N_MOD * D // ADA_TN),
        in_specs=[
            pl.BlockSpec((ADA_ROWS, D), lambda l, j: (0, 0)),
            pl.BlockSpec((1, D, ADA_TN), lambda l, j: (l, 0, j)),
            pl.BlockSpec((1, 1, ADA_TN), lambda l, j: (l, 0, j)),
        ],
        out_specs=pl.BlockSpec((1, ADA_ROWS, ADA_TN), lambda l, j: (l, 0, j)),
        out_shape=jax.ShapeDtypeStruct((DEPTH, ADA_ROWS, N_MOD * D), F32),
        compiler_params=_cparams(("parallel", "parallel")),
        name="ada",
    )(cc, ada_w, ada_b.reshape(DEPTH, 1, N_MOD * D))


def _norm_mod(x, nw, shift, scale):
    ms = jnp.mean(x * x, axis=-1, keepdims=True)
    return (x * lax.rsqrt(ms + EPS) * nw) * (1.0 + scale) + shift


def _in_kernel(split, *refs):
    if split:
        x_ref, c_ref, mod_ref, nw_ref, w_ref, ua_ref, ub_ref, us_ref, um_ref = refs
        x = jnp.where(pl.program_id(1) < LAT_TILES, x_ref[0], c_ref[0])
    else:
        x_ref, mod_ref, nw_ref, w_ref, ua_ref, ub_ref, us_ref, um_ref = refs
        x = x_ref[0]
    h = _norm_mod(x, nw_ref[...], mod_ref[0, 0:1, :], mod_ref[0, 1:2, :]).astype(BF16)
    for k, ref in enumerate((ua_ref, ub_ref, us_ref, um_ref)):
        ref[0] = _dot(h, w_ref[:, U_OFFS[k]:U_OFFS[k + 1]])


def _mod_index(b, t, nb):
    return (jnp.where(t >= LAT_TILES, nb, b), 0, 0)


def _token_specs(ctx, tile_of=lambda s: s):
    if ctx is None:
        return [pl.BlockSpec((1, ROW_TILE, D), lambda b, s: (b, tile_of(s), 0))]
    return [pl.BlockSpec((1, ROW_TILE, D), lambda b, s: (b, jnp.minimum(tile_of(s), LAT_TILES - 1), 0)),
            pl.BlockSpec((1, N_CTX, D), lambda b, s: (b, 0, 0))]


def _in_call(x, ctx, modt, nw, w_all):
    nb = x.shape[0]
    in_specs = _token_specs(ctx) + [
        pl.BlockSpec((1, N_MOD, D), lambda b, t: _mod_index(b, t, nb)),
        pl.BlockSpec((1, D), lambda b, t: (0, 0)),
        pl.BlockSpec((D, U_OFFS[-1]), lambda b, t: (0, 0)),
    ]
    widths = (UA_W, UB_W, US_W, UM_W)
    out_specs = [pl.BlockSpec((1, ROW_TILE, w), lambda b, t: (b, t, 0)) for w in widths]
    out_shape = [jax.ShapeDtypeStruct((nb, NT, w), F32) for w in widths]
    args = ((x,) if ctx is None else (x, ctx)) + (modt, nw, w_all)
    return pl.pallas_call(
        functools.partial(_in_kernel, ctx is not None),
        grid=(nb, N_TILES),
        in_specs=in_specs,
        out_specs=out_specs,
        out_shape=out_shape,
        compiler_params=_cparams(("parallel", "parallel")),
        name="in_proj",
    )(*args)


def _conv_chunk(u_ref, r0, rows, c0, c1, cw_ref, cb_ref, at_seg_start, at_seg_end):
    x = u_ref[0, pl.ds(r0, rows), c0:c1]
    rp = pl.multiple_of(jnp.maximum(r0 - SUBLANES, 0), SUBLANES)
    rn = pl.multiple_of(jnp.minimum(r0 + rows, NT - SUBLANES), SUBLANES)
    xp = u_ref[0, pl.ds(rp, SUBLANES), c0:c1] * jnp.where(at_seg_start, 0.0, 1.0)
    xn = u_ref[0, pl.ds(rn, SUBLANES), c0:c1] * jnp.where(at_seg_end, 0.0, 1.0)
    xe = jnp.concatenate([xp, x, xn], axis=0)
    tot = rows + 2 * SUBLANES
    lo, hi = SUBLANES, SUBLANES + rows
    xm2 = pltpu.roll(xe, 2, 0)[lo:hi]
    xm1 = pltpu.roll(xe, 1, 0)[lo:hi]
    xp1 = pltpu.roll(xe, tot - 1, 0)[lo:hi]
    return cw_ref[0:1, :] * xm2 + cw_ref[1:2, :] * xm1 + cw_ref[2:3, :] * x + cw_ref[3:4, :] * xp1 + cb_ref[...]


LRU_CH = 256
LRU_NCH = NT // LRU_CH
LRU_LAT_CH = N_LAT // LRU_CH


def _scan_chunk(a, b, row_in_tile, hprev, reverse):
    n = a.shape[0]
    for s in (1, 2, 4):
        if reverse:
            a_s = pltpu.roll(a, n - s, 0)
            b_s = pltpu.roll(b, n - s, 0)
            valid = row_in_tile < SUBLANES - s
        else:
            a_s = pltpu.roll(a, s, 0)
            b_s = pltpu.roll(b, s, 0)
            valid = row_in_tile >= s
        b = b + a * jnp.where(valid, b_s, 0.0)
        a = a * jnp.where(valid, a_s, 1.0)
    tiles = n // SUBLANES
    hs = [None] * tiles
    for j in (reversed(range(tiles)) if reverse else range(tiles)):
        lo = j * SUBLANES
        hj = b[lo:lo + SUBLANES] + a[lo:lo + SUBLANES] * hprev
        hprev = hj[0:1] if reverse else hj[SUBLANES - 1:SUBLANES]
        hs[j] = hj
    return jnp.concatenate(hs, axis=0), hprev


def _lru_kernel(u_ref, cw_ref, cb_ref, wri_ref, bri_ref, lam_ref, y_ref, ab_s, h_s):
    row_in_tile = lax.broadcasted_iota(jnp.int32, (LRU_CH, GW), 0) & (SUBLANES - 1)
    sp = _softplus(-lam_ref[...])

    def prepare(c, carry):
        r0 = pl.multiple_of(c * LRU_CH, LRU_CH)
        is_ctx = c == LRU_LAT_CH
        xc = _conv_chunk(u_ref, r0, LRU_CH, 0, GW, cw_ref, cb_ref, (c == 0) | is_ctx, (c == LRU_LAT_CH - 1) | is_ctx)
        g = _dot(xc.astype(BF16), wri_ref[...]) + bri_ref[...]
        for d in (0, 1):
            r = _sigmoid(g[:, 2 * d * GW:(2 * d + 1) * GW])
            ig = _sigmoid(g[:, (2 * d + 1) * GW:(2 * d + 2) * GW])
            a = jnp.exp(-LRU_C * r * sp[d:d + 1, :])
            ab_s[2 * d, pl.ds(r0, LRU_CH), :] = a
            ab_s[2 * d + 1, pl.ds(r0, LRU_CH), :] = jnp.sqrt(1.0 - a * a) * (ig * xc)
        return carry

    lax.fori_loop(0, LRU_NCH, prepare, 0)

    def body(i, carry):
        hf, hb = carry
        rf = pl.multiple_of(((i + LRU_LAT_CH) % LRU_NCH) * LRU_CH, LRU_CH)
        rb = pl.multiple_of((LRU_LAT_CH - i) * LRU_CH, LRU_CH)
        h, hf = _scan_chunk(ab_s[0, pl.ds(rf, LRU_CH), :], ab_s[1, pl.ds(rf, LRU_CH), :], row_in_tile, hf, False)
        h_s[0, pl.ds(rf, LRU_CH), :] = h
        h, hb = _scan_chunk(ab_s[2, pl.ds(rb, LRU_CH), :], ab_s[3, pl.ds(rb, LRU_CH), :], row_in_tile, hb, True)
        h_s[1, pl.ds(rb, LRU_CH), :] = h
        return hf, hb

    zero = jnp.zeros((1, GW), F32)
    lax.fori_loop(0, LRU_NCH, body, (zero, zero))

    def finish(c, carry):
        r0 = pl.multiple_of(c * LRU_CH, LRU_CH)
        gate = u_ref[0, pl.ds(r0, LRU_CH), GW:2 * GW]
        hsum = h_s[0, pl.ds(r0, LRU_CH), :] + h_s[1, pl.ds(r0, LRU_CH), :]
        y_ref[0, pl.ds(r0, LRU_CH), :] = (hsum * _gelu_tanh(gate)).astype(BF16)
        return carry

    lax.fori_loop(0, LRU_NCH, finish, 0)


def _lru_call(ua, cw, cb, wri, bri, lam):
    nb = ua.shape[0]
    full = lambda shape: pl.BlockSpec(shape, lambda b: (0,) * len(shape))
    return pl.pallas_call(
        _lru_kernel,
        grid=(nb,),
        in_specs=[
            pl.BlockSpec((1, NT, UA_W), lambda b: (b, 0, 0)),
            full((CONV_W, GW)), full((1, GW)), full((GW, 4 * GW)), full((1, 4 * GW)), full((2, GW)),
        ],
        out_specs=pl.BlockSpec((1, NT, GW), lambda b: (b, 0, 0)),
        out_shape=jax.ShapeDtypeStruct((nb, NT, GW), BF16),
        scratch_shapes=[pltpu.VMEM((4, NT, GW), F32), pltpu.VMEM((2, NT, GW), F32)],
        compiler_params=_cparams(("parallel",)),
        name="rglru",
    )(ua, cw, cb, wri, bri, lam)


HG_NCH = NT // HG_CH
HG_LAT_CH = N_LAT // HG_CH
HG_LEVELS = HG_CH.bit_length() - 1
HG_EXP_BLOCKS = 1 + HG_LEVELS
HG_SPLIT = 3


def _hgrn_tables():
    t_ = HG_CH
    dm = np.zeros((2, HG_EXP_BLOCKS, t_, t_), np.float32)
    lm = np.zeros((2, HG_LEVELS + 1, t_, t_), np.float32)
    for d in (0, 1):
        for t in range(t_):
            if d == 0:
                dm[d, 0, t, :t + 1] = 1.0
            else:
                dm[d, 0, t, t:] = 1.0
            lm[d, 0, t, t] = 1.0
        for lev in range(1, HG_LEVELS + 1):
            m = 1 << (lev - 1)
            for t in range(t_):
                start = (t // (2 * m)) * 2 * m
                mid = start + m
                upper = t >= mid
                if d == 0:
                    if upper:
                        dm[d, lev, t, mid:t + 1] = 1.0
                        lm[d, lev, t, start:mid] = 1.0
                    else:
                        dm[d, lev, t, t + 1:mid] = 1.0
                else:
                    if not upper:
                        dm[d, lev, t, t:mid] = 1.0
                        lm[d, lev, t, mid:start + 2 * m] = 1.0
                    else:
                        dm[d, lev, t, mid:t] = 1.0
    dm = dm.reshape(2, HG_EXP_BLOCKS * t_, t_)
    d3 = np.concatenate([dm] * HG_SPLIT, axis=-1)
    lm = np.tile(lm, (1, 1, 1, HG_HEADS))
    heads = np.arange(GW) // HG_HD
    bm = (heads[:, None] == heads[None, :]).astype(np.float32)
    return jnp.asarray(d3, BF16), jnp.asarray(lm, F32), jnp.asarray(bm, F32), jnp.asarray(bm, BF16)


def _hgrn_kernel(u_ref, lb_ref, nw_ref, d3_ref, lm_ref, bm_ref, bmb_ref, y_ref, o_s, st_s, stb_s):
    t = HG_CH
    lb = lb_ref[...]
    st_s[...] = jnp.zeros(st_s.shape, F32)
    stb_s[...] = jnp.zeros(stb_s.shape, BF16)

    def by_head(xb):
        return jnp.concatenate([xb] * HG_HEADS, axis=0) * bmb_ref[...]

    def chunk(d, c):
        r0 = pl.multiple_of(c * t, t)
        q = _silu(u_ref[0, pl.ds(r0, t), 0:GW])
        fr = u_ref[0, pl.ds(r0, t), (1 + d) * GW:(2 + d) * GW]
        v = u_ref[0, pl.ds(r0, t), 3 * GW:4 * GW]
        f = lb + (1.0 - lb) * _sigmoid(fr)
        k = 1.0 - f
        ex = _dot(d3_ref[d], jnp.concatenate(_split_bf16(jnp.log2(f), HG_SPLIT), axis=0))
        cum = ex[0:t]
        a = _dot_nt(q.astype(BF16), by_head(k.astype(BF16))) * lm_ref[d, 0]
        for lev in range(1, HG_LEVELS + 1):
            fac = jnp.exp2(ex[lev * t:(lev + 1) * t])
            a = a + _dot_nt((q * fac).astype(BF16), by_head((k * fac).astype(BF16))) * lm_ref[d, lev]
        vb = v.astype(BF16)
        o = _dot(a.astype(BF16), by_head(vb)) + _dot_nt((q * jnp.exp2(cum)).astype(BF16), stb_s[d])
        o_s[d, pl.ds(r0, t), :] = o
        cend = cum[t - 1:t, :] if d == 0 else cum[0:1, :]
        kend = (k * jnp.exp2(cend - cum)).astype(BF16)
        st = st_s[d] * jnp.exp2(cend) + _dot_tn(vb, kend) * bm_ref[...]
        st_s[d] = st
        stb_s[d] = st.astype(BF16)

    def body(i, carry):
        chunk(0, (i + HG_LAT_CH) % HG_NCH)
        chunk(1, HG_NCH - 1 - i)
        return carry

    lax.fori_loop(0, HG_NCH, body, 0, unroll=2)

    def finish(c, carry):
        r0 = pl.multiple_of(c * t, t)
        osum = o_s[0, pl.ds(r0, t), :] + o_s[1, pl.ds(r0, t), :]
        sq = _split_bf16(osum * osum, 2)
        ms = _dot(jnp.concatenate(sq, axis=1), jnp.concatenate([bmb_ref[...]] * 2, axis=0)) * (1.0 / HG_HD)
        g = u_ref[0, pl.ds(r0, t), 4 * GW:5 * GW]
        y_ref[0, pl.ds(r0, t), :] = (osum * lax.rsqrt(ms + EPS) * nw_ref[...] * _silu(g)).astype(BF16)
        return carry

    lax.fori_loop(0, HG_NCH, finish, 0)


def _hgrn_call(ub, lb, nw, tables):
    nb = ub.shape[0]
    d3, lm, bm, bmb = tables
    full = lambda shape: pl.BlockSpec(shape, lambda b: (0,) * len(shape))
    return pl.pallas_call(
        _hgrn_kernel,
        grid=(nb,),
        in_specs=[pl.BlockSpec((1, NT, UB_W), lambda b: (b, 0, 0)), full((1, GW)), full((1, GW)),
                  full(d3.shape), full(lm.shape), full(bm.shape), full(bmb.shape)],
        out_specs=pl.BlockSpec((1, NT, GW), lambda b: (b, 0, 0)),
        out_shape=jax.ShapeDtypeStruct((nb, NT, GW), BF16),
        scratch_shapes=[pltpu.VMEM((2, NT, GW), F32), pltpu.VMEM((2, GW, GW), F32), pltpu.VMEM((2, GW, GW), BF16)],
        compiler_params=_cparams(("parallel",)),
        name="hgrn2",
    )(ub, lb, nw, d3, lm, bm, bmb)


SSD_NCH = NT // SSD_CH
SSD_LAT_CH = N_LAT // SSD_CH
SSD_DT0 = GW + SSD_XBC
GRP_W = SSD_HEADS // SSD_GROUPS * SSD_HD


SSD_BC_W = 2 * SSD_GROUPS * SSD_STATE
SSD_CUM_SPLIT = 3
SSD_GAIN_SPLIT = 2


def _ssd_tables():
    t = SSD_CH
    tri = np.stack([np.tril(np.ones((t, t), np.float32)), np.triu(np.ones((t, t), np.float32))])
    col = np.zeros((2, LANES, SSD_HEADS * LANES), np.float32)
    wide = np.zeros((2, LANES, GW), np.float32)
    for d in range(2):
        for h in range(SSD_HEADS):
            col[d, d * SSD_HEADS + h, h * LANES:(h + 1) * LANES] = 1.0
            wide[d, d * SSD_HEADS + h, h * SSD_HD:(h + 1) * SSD_HD] = 1.0
    hm = (np.arange(GW)[None, :] // SSD_HD == np.arange(SSD_HEADS)[:, None]).astype(np.float32)
    gm = (np.arange(LANES)[None, :] // SSD_STATE == np.arange(SSD_GROUPS)[:, None]).astype(np.float32)
    return (jnp.asarray(np.concatenate([tri] * SSD_CUM_SPLIT, axis=2), BF16),
            jnp.asarray(np.concatenate([col] * SSD_CUM_SPLIT, axis=1), BF16),
            jnp.asarray(np.concatenate([wide] * SSD_GAIN_SPLIT, axis=1), BF16),
            jnp.asarray(hm[:, None, :], BF16), jnp.asarray(gm[:, None, :], BF16))


def _ssd_kernel(u_ref, cw_ref, cb_ref, a8_ref, b8_ref, dsk_ref, nw_ref, tri_ref, col_ref, wide_ref, hm_ref, gm_ref,
                y_ref, xs_s, bc_s, xm_s, dt_s, y_s, st_s):
    t = SSD_CH
    rr = lax.broadcasted_iota(jnp.int32, (t, t), 0)
    cc = lax.broadcasted_iota(jnp.int32, (t, t), 1)
    keeps = (rr >= cc, rr <= cc)
    a8 = -jnp.exp(a8_ref[...])

    def prepare(c, carry):
        r0 = pl.multiple_of(c * t, t)
        seg_start = (c == 0) | (c == SSD_LAT_CH)
        seg_end = (c == SSD_LAT_CH - 1) | (c == SSD_NCH - 1)
        xbc = _silu(_conv_chunk(u_ref, r0, t, GW, GW + SSD_XBC, cw_ref, cb_ref, seg_start, seg_end))
        xs = xbc[:, 0:GW]
        xs_s[pl.ds(r0, t), :] = xs
        bc_s[pl.ds(r0, t), :] = xbc[:, GW:GW + SSD_BC_W].astype(BF16)
        xb = xs.astype(BF16)
        for h in range(SSD_HEADS):
            xm_s[h, pl.ds(r0, t), :] = xb * hm_ref[h]
        dt_s[pl.ds(r0, t), :] = _softplus(u_ref[0, pl.ds(r0, t), SSD_DT0:SSD_DT0 + LANES] + b8_ref[...])
        return carry

    lax.fori_loop(0, SSD_NCH, prepare, 0)
    st_s[...] = jnp.zeros(st_s.shape, F32)

    def chunk(d, c):
        r0 = pl.multiple_of(c * t, t)
        dtn = dt_s[pl.ds(r0, t), :]
        cumn = _dot(tri_ref[d], jnp.concatenate(_split_bf16(dtn * a8, SSD_CUM_SPLIT), axis=0))
        cend = cumn[t - 1:t, :] if d == 0 else cumn[0:1, :]
        cum_t = cumn.T
        dt_t = dtn.T
        colb = _dot(jnp.concatenate(_split_bf16(cumn, SSD_CUM_SPLIT), axis=1), col_ref[d])
        gains = jnp.concatenate([jnp.exp(cumn), dtn * jnp.exp(cend - cumn)], axis=0)
        gw = _dot(jnp.concatenate(_split_bf16(gains, SSD_GAIN_SPLIT), axis=1), wide_ref[d])
        ecum, wend = gw[0:t], gw[t:2 * t]
        bc = bc_s[pl.ds(r0, t), :]
        bmat, cmat = bc[:, 0:LANES], bc[:, LANES:2 * LANES]
        st = st_s[d]
        y = jnp.zeros((t, GW), F32)
        y_in = []
        for g in range(SSD_GROUPS):
            cg = cmat * gm_ref[g]
            scores = _dot_nt(cg, bmat)
            sg = st[:, g * GRP_W:(g + 1) * GRP_W].astype(BF16)
            y_in.append(_dot(cg, jnp.concatenate([sg] * SSD_GROUPS, axis=0)))
            for hh in range(SSD_HEADS // SSD_GROUPS):
                h = g * (SSD_HEADS // SSD_GROUPS) + hh
                row = d * SSD_HEADS + h
                seg = colb[:, h * LANES:(h + 1) * LANES] - cum_t[row:row + 1, :]
                m = jnp.where(keeps[d], jnp.exp(seg), 0.0) * scores * dt_t[row:row + 1, :]
                y = y + _dot(m.astype(BF16), xm_s[h, pl.ds(r0, t), :])
        y_s[d, pl.ds(r0, t), :] = y + jnp.concatenate(y_in, axis=1) * ecum
        xw = (xs_s[pl.ds(r0, t), :] * wend).astype(BF16)
        upd = [_dot_tn(bmat, xw[:, g * GRP_W:(g + 1) * GRP_W])[g * SSD_STATE:(g + 1) * SSD_STATE]
               for g in range(SSD_GROUPS)]
        eend = ecum[t - 1:t, :] if d == 0 else ecum[0:1, :]
        st_s[d] = st * eend + jnp.concatenate(upd, axis=1)

    def body(i, carry):
        chunk(0, (i + SSD_LAT_CH) % SSD_NCH)
        chunk(1, SSD_NCH - 1 - i)
        return carry

    lax.fori_loop(0, SSD_NCH, body, 0, unroll=2)

    def finish(c, carry):
        r0 = pl.multiple_of(c * t, t)
        z = u_ref[0, pl.ds(r0, t), 0:GW]
        yy = (y_s[0, pl.ds(r0, t), :] + y_s[1, pl.ds(r0, t), :] + dsk_ref[...] * xs_s[pl.ds(r0, t), :]) * _silu(z)
        ms = jnp.mean(yy * yy, axis=-1, keepdims=True)
        y_ref[0, pl.ds(r0, t), :] = (yy * lax.rsqrt(ms + EPS) * nw_ref[...]).astype(BF16)
        return carry

    lax.fori_loop(0, SSD_NCH, finish, 0)


def _ssd_call(us, cw, cb, a8, b8, dsk, nw, tables):
    nb = us.shape[0]
    full = lambda shape: pl.BlockSpec(shape, lambda b: (0,) * len(shape))
    return pl.pallas_call(
        _ssd_kernel,
        grid=(nb,),
        in_specs=[
            pl.BlockSpec((1, NT, US_W), lambda b: (b, 0, 0)),
            full((CONV_W, SSD_XBC)), full((1, SSD_XBC)), full((1, LANES)), full((1, LANES)),
            full((1, GW)), full((1, GW)),
        ] + [full(tb.shape) for tb in tables],
        out_specs=pl.BlockSpec((1, NT, GW), lambda b: (b, 0, 0)),
        out_shape=jax.ShapeDtypeStruct((nb, NT, GW), BF16),
        scratch_shapes=[pltpu.VMEM((NT, GW), F32), pltpu.VMEM((NT, SSD_BC_W), BF16),
                        pltpu.VMEM((SSD_HEADS, NT, GW), BF16), pltpu.VMEM((NT, LANES), F32),
                        pltpu.VMEM((2, NT, GW), F32), pltpu.VMEM((2, SSD_STATE, GW), F32)],
        compiler_params=_cparams(("parallel",)),
        name="ssd",
    )(us, cw, cb, a8, b8, dsk, nw, *tables)


HB = LANES
QW = MLA_HEADS * HB
MLA_QB = 256
MLA_AB = 256
MLA_SCALE = (MLA_NOPE + MLA_ROPE) ** -0.5


def _mla_kernel(need_ctx, u_ref, qa_ref, wq_ref, qn_ref, kva_ref, wkv_ref, kn_ref, krw_ref,
                cos_ref, sa_ref, sb_ref, y_ref, q_s, k_s, v_s):
    lane = lax.broadcasted_iota(jnp.int32, (QW, QW), 1)
    row = lax.broadcasted_iota(jnp.int32, (QW, QW), 0)

    def grp(i):
        within = i & (HB - 1)
        return (i >> HB_SHIFT) * 3 + jnp.where(within < MLA_NOPE, 0, jnp.where(within < MLA_NOPE + MLA_ROPE, 1, 2))

    gmat = jnp.where(grp(row) == grp(lane), 1.0, 0.0).astype(BF16)
    l1 = lax.broadcasted_iota(jnp.int32, (1, QW), 1) & (HB - 1)
    inv_size = jnp.where(l1 < MLA_NOPE, 1.0 / MLA_NOPE, 1.0 / MLA_ROPE)
    vlane_head = lax.broadcasted_iota(jnp.int32, (MLA_QB, GW), 1) >> HEAD_SHIFT

    def rope(x, r0, reps):
        cosr = cos_ref[pl.ds(r0, MLA_QB), :]
        sar = sa_ref[pl.ds(r0, MLA_QB), :]
        sbr = sb_ref[pl.ds(r0, MLA_QB), :]
        if reps > 1:
            cosr, sar, sbr = (jnp.concatenate([tbl] * reps, axis=1) for tbl in (cosr, sar, sbr))
        w = x.shape[1]
        return x * cosr + pltpu.roll(x, w - MLA_ROPE // 2, 1) * sar + pltpu.roll(x, MLA_ROPE // 2, 1) * sbr

    def project(ci, carry):
        r0 = pl.multiple_of(ci * MLA_QB, MLA_QB)
        cq = u_ref[0, pl.ds(r0, MLA_QB), 0:256]
        ms = jnp.sum(cq * cq, axis=-1, keepdims=True) * (1.0 / MLA_QR)
        qraw = _dot((cq * lax.rsqrt(ms + EPS) * qa_ref[...]).astype(BF16), wq_ref[...])
        ss = _dot((qraw * qraw).astype(BF16), gmat) * inv_size
        q = rope(qraw * lax.rsqrt(ss + EPS) * qn_ref[...], r0, MLA_HEADS) * MLA_SCALE
        q_s[pl.ds(r0, MLA_QB), :] = q.astype(BF16)
        ckv = u_ref[0, pl.ds(r0, MLA_QB), 256:256 + MLA_KVR]
        ms = jnp.mean(ckv * ckv, axis=-1, keepdims=True)
        kv = _dot((ckv * lax.rsqrt(ms + EPS) * kva_ref[...]).astype(BF16), wkv_ref[...])
        kraw = kv[:, 0:QW]
        ss = _dot((kraw * kraw).astype(BF16), gmat) * inv_size
        knope = kraw * lax.rsqrt(ss + EPS) * kn_ref[...]
        kr = u_ref[0, pl.ds(r0, MLA_QB), 384:512]
        ms = jnp.sum(kr * kr, axis=-1, keepdims=True) * (1.0 / MLA_ROPE)
        krope = rope(kr * lax.rsqrt(ms + EPS) * krw_ref[...], r0, 1)
        k_s[pl.ds(r0, MLA_QB), :] = (knope + jnp.concatenate([krope] * MLA_HEADS, axis=1)).astype(BF16)
        vv = kv[:, QW:QW + GW]
        for h in range(MLA_HEADS):
            v_s[h, pl.ds(r0, MLA_QB), :] = jnp.where(vlane_head == h, vv, 0.0).astype(BF16)
        return carry

    lax.fori_loop(0, NT // MLA_QB, project, 0)

    def attend(r0, rows, k0, klen):
        o = jnp.zeros((rows, GW), F32)
        for h in range(MLA_HEADS):
            qh = q_s[pl.ds(r0, rows), h * HB:(h + 1) * HB]
            s = _dot_nt(qh, k_s[k0:k0 + klen, h * HB:(h + 1) * HB])
            p = jnp.exp(s - jnp.max(s, axis=-1, keepdims=True))
            l = jnp.sum(p, axis=-1, keepdims=True)
            o = o + _dot(p.astype(BF16), v_s[h, k0:k0 + klen, :]) * (1.0 / l)
        y_ref[0, pl.ds(r0, rows), :] = o.astype(BF16)

    def lat_block(qi, carry):
        attend(pl.multiple_of(qi * MLA_AB, MLA_AB), MLA_AB, 0, NT)
        return carry

    lax.fori_loop(0, N_LAT // MLA_AB, lat_block, 0)
    if need_ctx:
        attend(N_LAT, N_CTX, N_LAT, N_CTX)
    else:
        y_ref[0, N_LAT:NT, :] = jnp.zeros((N_CTX, GW), BF16)


def _mla_call(um, need_ctx, qa, wq, qn, kva, wkv, kn, krw, cos_t, sa_t, sb_t):
    nb = um.shape[0]
    full = lambda shape: pl.BlockSpec(shape, lambda b: (0,) * len(shape))
    return pl.pallas_call(
        functools.partial(_mla_kernel, need_ctx),
        grid=(nb,),
        in_specs=[
            pl.BlockSpec((1, NT, UM_W), lambda b: (b, 0, 0)),
            full((1, 256)), full((256, QW)), full((1, QW)),
            full((1, MLA_KVR)), full((MLA_KVR, QW + GW)), full((1, QW)), full((1, HB)),
            full((NT, HB)), full((NT, HB)), full((NT, HB)),
        ],
        out_specs=pl.BlockSpec((1, NT, GW), lambda b: (b, 0, 0)),
        out_shape=jax.ShapeDtypeStruct((nb, NT, GW), BF16),
        scratch_shapes=[pltpu.VMEM((NT, QW), BF16), pltpu.VMEM((NT, QW), BF16),
                        pltpu.VMEM((MLA_HEADS, NT, GW), BF16)],
        compiler_params=_cparams(("parallel",)),
        name="mla",
    )(um, qa, wq, qn, kva, wkv, kn, krw, cos_t, sa_t, sb_t)


def _out_tile(s):
    return jnp.maximum(s - 1, 0)


def _out_kernel(split, *refs):
    if split:
        x_ref, c_ref, ya_ref, yb_ref, ys_ref, ym_ref, wo_ref, mod_ref, nw_ref, wr_ref = refs[:10]
    else:
        x_ref, ya_ref, yb_ref, ys_ref, ym_ref, wo_ref, mod_ref, nw_ref, wr_ref = refs[:9]
    xo_ref, h2_ref, aff_ref, acc_s = refs[-4:]
    s = pl.program_id(1)

    @pl.when(s == 0)
    def _first():
        acc_s[1] = jnp.zeros(acc_s.shape[1:], F32)

    def step(slot):
        y = jnp.concatenate([ya_ref[0], yb_ref[0], ys_ref[0], ym_ref[0]], axis=1)
        acc_s[slot] = _dot(y, wo_ref[...])
        x_in = jnp.where(_out_tile(s) < LAT_TILES, x_ref[0], c_ref[0]) if split else x_ref[0]
        x = x_in + mod_ref[0, 2:3, :] * acc_s[1 - slot]
        xo_ref[0] = x
        h2 = _norm_mod(x, nw_ref[...], mod_ref[0, 3:4, :], mod_ref[0, 4:5, :])
        hi = h2.astype(BF16)
        h2_ref[0] = hi
        lo = (h2 - hi.astype(F32)).astype(BF16)
        wr = wr_ref[...]
        w_hi = wr.astype(BF16)
        w_lo = (wr - w_hi.astype(F32)).astype(BF16)
        logit = _dot_nt(jnp.concatenate([w_hi, w_hi, w_lo], axis=1), jnp.concatenate([hi, lo, hi], axis=1))
        e = jnp.exp(logit - jnp.max(logit, axis=0, keepdims=True))
        aff_ref[0] = e / jnp.sum(e, axis=0, keepdims=True)

    for parity in (0, 1):
        pl.when(s % 2 == parity)(functools.partial(step, parity))


def _out_call(x, ctx, ys, wo, modt, nw, wr_t, n_tiles):
    nb = x.shape[0]
    proj = lambda w: pl.BlockSpec((1, ROW_TILE, w), lambda b, s: (b, jnp.minimum(s, n_tiles - 1), 0))
    done = lambda w: pl.BlockSpec((1, ROW_TILE, w), lambda b, s: (b, _out_tile(s), 0))
    x_args = (x,) if ctx is None else (x, ctx)
    return pl.pallas_call(
        functools.partial(_out_kernel, ctx is not None),
        grid=(nb, n_tiles + 1),
        in_specs=_token_specs(ctx, _out_tile) + [proj(GW), proj(GW), proj(GW), proj(GW),
                  pl.BlockSpec((D, D), lambda b, s: (0, 0)),
                  pl.BlockSpec((1, N_MOD, D), lambda b, s: _mod_index(b, _out_tile(s), nb)),
                  pl.BlockSpec((1, D), lambda b, s: (0, 0)),
                  pl.BlockSpec((N_EXP, D), lambda b, s: (0, 0))],
        out_specs=[done(D), done(D), pl.BlockSpec((1, N_EXP, ROW_TILE), lambda b, s: (b, 0, _out_tile(s)))],
        out_shape=[jax.ShapeDtypeStruct((nb, n_tiles * ROW_TILE, D), F32),
                   jax.ShapeDtypeStruct((nb, n_tiles * ROW_TILE, D), BF16),
                   jax.ShapeDtypeStruct((nb, N_EXP, n_tiles * ROW_TILE), F32)],
        scratch_shapes=[pltpu.VMEM((2, ROW_TILE, D), F32)],
        compiler_params=_cparams(("parallel", "arbitrary")),
        name="out_proj",
    )(*x_args, *ys, wo, modt, nw, wr_t)


PRE_W = 256
ROUTE_MAX_IT = 160


def _prefix_count(m, tri_bf):
    n = m.shape[1]
    off = jnp.zeros((m.shape[0], 1), F32)
    outs = []
    for j in range(n // PRE_W):
        blk = m[:, j * PRE_W:(j + 1) * PRE_W]
        outs.append(_dot(blk.astype(BF16), tri_bf) + off)
        off = off + jnp.sum(blk, axis=1, keepdims=True)
    return outs[0] if len(outs) == 1 else jnp.concatenate(outs, axis=1)


def _route(aff, cap, tri):
    n_lo0 = jnp.full((N_EXP, 1), float(aff.shape[1]), F32)

    def cond(s):
        it, _, _, n_lo, n_hi = s
        return jnp.logical_and(it < ROUTE_MAX_IT, jnp.max(n_lo - n_hi) > 1.0)

    def step(s):
        it, lo, hi, n_lo, n_hi = s
        width = hi - lo
        mids = [lo + frac * width for frac in (0.25, 0.5, 0.75)]
        cnts = [jnp.sum(jnp.where(aff > m, 1.0, 0.0), axis=1, keepdims=True) for m in mids]
        blws = [jnp.max(jnp.where(aff <= m, aff, -1.0), axis=1, keepdims=True) for m in mids]
        u0, u1, u2 = (c >= cap for c in cnts)
        pick = lambda a3, a2, a1, a0: jnp.where(u2, a3, jnp.where(u1, a2, jnp.where(u0, a1, a0)))
        return (it + 1, pick(mids[2], mids[1], mids[0], lo), pick(hi, blws[2], blws[1], blws[0]),
                pick(cnts[2], cnts[1], cnts[0], n_lo), pick(n_hi, cnts[2], cnts[1], cnts[0]))

    init = (jnp.int32(0), jnp.full((N_EXP, 1), -1.0, F32), jnp.max(aff, axis=1, keepdims=True),
            n_lo0, jnp.zeros((N_EXP, 1), F32))
    _, _, thr, _, n_gt = lax.while_loop(cond, step, init)
    gt = jnp.where(aff > thr, 1.0, 0.0)
    eq = jnp.where(aff == thr, 1.0, 0.0)
    sel = gt + eq * jnp.where(_prefix_count(eq, tri) < cap - n_gt, 1.0, 0.0)
    return jnp.where(sel > 0.0, _prefix_count(sel, tri), -1.0)


MOE_EPS = 2


def _moe_kernel(segs, x_hbm, aff_ref, h_ref, gl_ref, gc_ref, wg_ref, wu_ref, wd_ref, o_ref, pos_ref, sem):
    b = pl.program_id(0)
    step = pl.program_id(1)
    rows = o_ref.shape[1]

    def residual_copy():
        return pltpu.make_async_copy(x_hbm.at[b, pl.ds(0, rows), :], o_ref.at[0], sem)

    @pl.when(step == 0)
    def _init():
        residual_copy().start()
        tri = jnp.where(lax.broadcasted_iota(jnp.int32, (PRE_W, PRE_W), 0)
                        < lax.broadcasted_iota(jnp.int32, (PRE_W, PRE_W), 1), 1.0, 0.0).astype(BF16)
        for r0, n, cap in segs:
            pos_ref[:, r0:r0 + n] = _route(aff_ref[0, :, r0:r0 + n], cap, tri)
        residual_copy().wait()

    onehots, gates, xs = [], [], []
    for (r0, n, cap), g_ref in zip(segs, (gl_ref, gc_ref)):
        slot = lax.broadcasted_iota(jnp.int32, (cap, n), 0).astype(F32)
        oh, gt = [], []
        for j in range(MOE_EPS):
            e = step * MOE_EPS + j
            hit = slot == pos_ref[pl.ds(e, 1), r0:r0 + n]
            oh.append(jnp.where(hit, 1.0, 0.0).astype(BF16))
            gsel = jnp.sum(jnp.where(hit, aff_ref[0, pl.ds(e, 1), r0:r0 + n], 0.0), axis=1, keepdims=True)
            gt.append(gsel * g_ref[0, N_MOD - 1:N_MOD, :])
        onehot = jnp.concatenate(oh, axis=0)
        onehots.append(onehot)
        gates.append(gt)
        xs.append(_dot(onehot, h_ref[0, r0:r0 + n, :]).astype(BF16))
    ys = []
    for j in range(MOE_EPS):
        xj = [x[j * cap:(j + 1) * cap] for x, (_, _, cap) in zip(xs, segs)]
        xj = xj[0] if len(xj) == 1 else jnp.concatenate(xj, axis=0)
        act = (_silu(_dot(xj, wg_ref[j])) * _dot(xj, wu_ref[j])).astype(BF16)
        ys.append(_dot(act, wd_ref[j]))
    s0 = 0
    for (r0, n, cap), onehot, gt in zip(segs, onehots, gates):
        ysg = jnp.concatenate([(ys[j][s0:s0 + cap] * gt[j]).astype(BF16) for j in range(MOE_EPS)], axis=0)
        o_ref[0, r0:r0 + n, :] += _dot_tn(onehot, ysg)
        s0 += cap


def _moe_call(x_mid, aff, h2, modt, wg, wu, wd, with_ctx):
    nb = h2.shape[0]
    segs = ((0, N_LAT, EC_FACTOR * N_LAT // N_EXP),)
    if with_ctx:
        segs += ((N_LAT, N_CTX, EC_FACTOR * N_CTX // N_EXP),)
    rows = NT if with_ctx else N_LAT
    return pl.pallas_call(
        functools.partial(_moe_kernel, segs),
        grid=(nb, N_EXP // MOE_EPS),
        in_specs=[
            pl.BlockSpec(memory_space=pl.ANY),
            pl.BlockSpec((1, N_EXP, rows), lambda b, e: (b, 0, 0)),
            pl.BlockSpec((1, rows, D), lambda b, e: (b, 0, 0)),
            pl.BlockSpec((1, N_MOD, D), lambda b, e: (b, 0, 0)),
            pl.BlockSpec((1, N_MOD, D), lambda b, e: (nb, 0, 0)),
            pl.BlockSpec((MOE_EPS, D, FF), lambda b, e: (e, 0, 0)),
            pl.BlockSpec((MOE_EPS, D, FF), lambda b, e: (e, 0, 0)),
            pl.BlockSpec((MOE_EPS, FF, D), lambda b, e: (e, 0, 0)),
        ],
        out_specs=pl.BlockSpec((1, rows, D), lambda b, e: (b, 0, 0)),
        out_shape=jax.ShapeDtypeStruct((nb, rows, D), F32),
        scratch_shapes=[pltpu.VMEM((N_EXP, rows), F32), pltpu.SemaphoreType.DMA(())],
        compiler_params=_cparams(("parallel", "arbitrary")),
        name="moe",
    )(x_mid, aff, h2, modt, modt, wg, wu, wd)


def _pad_cols(w, width):
    return jnp.pad(w, ((0, 0), (0, width - w.shape[1])))


def _rope_perm():
    half = MLA_ROPE // 2
    return jnp.concatenate([jnp.arange(half) * 2, jnp.arange(half) * 2 + 1])


def _head_block(nope, rope):
    pad = jnp.zeros(nope.shape[:-1] + (HB - MLA_NOPE - MLA_ROPE,), nope.dtype)
    blk = jnp.concatenate([nope, rope, pad], axis=-1)
    return blk.reshape(blk.shape[:-2] + (MLA_HEADS * HB,))


def _prep_in_weights(w_in):
    o1, o2, o3 = LRU_COLS, LRU_COLS + HG_COLS, LRU_COLS + HG_COLS + SSD_COLS
    wa, wb, ws, wm = w_in[:, :o1], w_in[:, o1:o2], w_in[:, o2:o3], w_in[:, o3:]
    perm = _rope_perm()
    cq, ckv, kr = wm[:, :MLA_QR], wm[:, MLA_QR:MLA_QR + MLA_KVR], wm[:, MLA_QR + MLA_KVR:]
    zeros = lambda n: jnp.zeros((D, n), w_in.dtype)
    wm_p = jnp.concatenate([cq, zeros(256 - MLA_QR), ckv, zeros(MLA_NOPE), kr[:, perm],
                            zeros(HB - MLA_NOPE - MLA_ROPE)], axis=1)
    return jnp.concatenate([wa, wb, _pad_cols(ws, US_W), wm_p], axis=1).astype(BF16)


def _block_diag(w):
    h, dd, _ = w.shape
    eye = jnp.eye(h, dtype=w.dtype)
    return (eye[:, None, :, None] * w[:, :, None, :]).reshape(h * dd, h * dd)


def _rope_tables():
    rows = N_LAT // GRID_W
    row = jnp.repeat(jnp.arange(rows, dtype=F32), GRID_W)
    col = jnp.tile(jnp.arange(GRID_W, dtype=F32), rows)
    half = MLA_ROPE // 2
    inv = ROPE_BASE ** (-jnp.arange(0, half, 2, dtype=F32) / half)
    ang = jnp.concatenate([row[:, None] * inv, col[:, None] * inv], axis=-1)
    cos, sin = jnp.cos(ang), jnp.sin(ang)
    z = lambda n: jnp.zeros((N_LAT, n), F32)
    o = lambda n: jnp.ones((N_LAT, n), F32)
    cos_t = jnp.concatenate([o(MLA_NOPE), cos, cos, o(HB - MLA_NOPE - MLA_ROPE)], axis=1)
    sa_t = jnp.concatenate([z(MLA_NOPE), -sin, z(half), z(HB - MLA_NOPE - MLA_ROPE)], axis=1)
    sb_t = jnp.concatenate([z(MLA_NOPE), z(half), sin, z(HB - MLA_NOPE - MLA_ROPE)], axis=1)
    cos_t = jnp.concatenate([cos_t, jnp.ones((N_CTX, HB), F32)], axis=0)
    sa_t = jnp.concatenate([sa_t, jnp.zeros((N_CTX, HB), F32)], axis=0)
    sb_t = jnp.concatenate([sb_t, jnp.zeros((N_CTX, HB), F32)], axis=0)
    return cos_t, sa_t, sb_t


def kernel(x, c, ctx, c_ctx, ada_w, ada_b, norm1_w, norm2_w, w_in, w_out, lru_conv_w, lru_conv_b, lru_w_r, lru_b_r, lru_w_i, lru_b_i, lru_lam, hgrn_lb_logits, hgrn_norm_w, ssd_conv_w, ssd_conv_b, ssd_a_log, ssd_dt_bias, ssd_d_skip, ssd_norm_w, mla_q_a_norm, mla_w_q_up, mla_kv_a_norm, mla_w_kv_up, mla_q_norm, mla_k_norm, moe_router, moe_w_gate, moe_w_up, moe_w_down):
    nb = x.shape[0]
    assert x.shape == (nb, N_LAT, D) and ctx.shape == (nb, N_CTX, D)
    assert nb + 1 <= ADA_ROWS
    cc = jnp.concatenate([c, c_ctx[None, :], jnp.zeros((ADA_ROWS - nb - 1, D), F32)], axis=0)
    mod_all = _ada_call(cc, ada_w, ada_b)[:, :nb + 1].reshape(DEPTH, nb + 1, N_MOD, D)
    cos_t, sa_t, sb_t = _rope_tables()
    hg_tables = _hgrn_tables()
    ssd_tables = _ssd_tables()
    perm = _rope_perm()
    lb_w = jax.nn.softmax(hgrn_lb_logits.astype(F32), axis=0)
    lb_all = jnp.cumsum(lb_w, axis=0) - lb_w[0]
    rep = lambda v, n: jnp.repeat(v, n, axis=-1)
    stream = (x, ctx)
    for l in range(DEPTH):
        need_ctx = l < DEPTH - 1
        modt = mod_all[l]
        ua, ub, us, um = _in_call(*stream, modt, norm1_w[l][None, :], _prep_in_weights(w_in[l]))
        wri = jnp.concatenate([_block_diag(lru_w_r[l, 0]), _block_diag(lru_w_i[l, 0]),
                               _block_diag(lru_w_r[l, 1]), _block_diag(lru_w_i[l, 1])], axis=1).astype(BF16)
        bri = jnp.concatenate([lru_b_r[l, 0], lru_b_i[l, 0], lru_b_r[l, 1], lru_b_i[l, 1]])[None, :]
        ya = _lru_call(ua, lru_conv_w[l], lru_conv_b[l][None, :], wri, bri, lru_lam[l])
        yb = _hgrn_call(ub, lb_all[l][None, :], hgrn_norm_w[l][None, :], hg_tables)
        narrow = lambda v: jnp.pad(v.reshape(-1), (0, LANES - 2 * SSD_HEADS))[None, :]
        ysd = _ssd_call(us, ssd_conv_w[l], ssd_conv_b[l][None, :], narrow(ssd_a_log[l]), narrow(ssd_dt_bias[l]),
                        rep(ssd_d_skip[l], SSD_HD)[None, :], ssd_norm_w[l][None, :], ssd_tables)
        wq = mla_w_q_up[l].reshape(MLA_QR, MLA_HEADS, MLA_NOPE + MLA_ROPE)
        wq = _head_block(wq[..., :MLA_NOPE], wq[..., MLA_NOPE:][..., perm])
        wq = jnp.pad(wq, ((0, 256 - MLA_QR), (0, 0))).astype(BF16)
        qn = _head_block(jnp.broadcast_to(mla_q_norm[l][:MLA_NOPE], (MLA_HEADS, MLA_NOPE)),
                         jnp.broadcast_to(mla_q_norm[l][MLA_NOPE:][perm], (MLA_HEADS, MLA_ROPE)))[None, :]
        wkv = mla_w_kv_up[l].reshape(MLA_KVR, MLA_HEADS, MLA_NOPE + MLA_V)
        wk = _head_block(wkv[..., :MLA_NOPE], jnp.zeros((MLA_KVR, MLA_HEADS, MLA_ROPE), F32))
        wv = wkv[..., MLA_NOPE:].reshape(MLA_KVR, GW)
        wkv_p = jnp.concatenate([wk, wv], axis=1).astype(BF16)
        kn = _head_block(jnp.broadcast_to(mla_k_norm[l][:MLA_NOPE], (MLA_HEADS, MLA_NOPE)),
                         jnp.zeros((MLA_HEADS, MLA_ROPE), F32))[None, :]
        krw = jnp.concatenate([jnp.zeros((MLA_NOPE,), F32), mla_k_norm[l][MLA_NOPE:][perm],
                               jnp.zeros((HB - MLA_NOPE - MLA_ROPE,), F32)])[None, :]
        qa = jnp.pad(mla_q_a_norm[l], (0, 256 - MLA_QR))[None, :]
        ym = _mla_call(um, need_ctx, qa, wq, qn, mla_kv_a_norm[l][None, :], wkv_p, kn, krw, cos_t, sa_t, sb_t)
        n_tiles = N_TILES if need_ctx else LAT_TILES
        x_mid, h2, aff = _out_call(*stream, (ya, yb, ysd, ym), w_out[l].astype(BF16), modt, norm2_w[l][None, :],
                                   moe_router[l].T, n_tiles)
        wg, wu, wd = moe_w_gate[l].astype(BF16), moe_w_up[l].astype(BF16), moe_w_down[l].astype(BF16)
        stream = (_moe_call(x_mid, aff, h2, modt, wg, wu, wd, need_ctx), None)
    return stream[0]
```

```python
import functools
import math

import jax
import jax.numpy as jnp
import numpy as np
from jax import lax
from jax.experimental import pallas as pl
from jax.experimental.pallas import tpu as pltpu

F32 = jnp.float32
BF16 = jnp.bfloat16
HI = lax.Precision.HIGHEST

D = 1024
DEPTH = 2
N_LAT = 2048
N_CTX = 256
NT = N_LAT + N_CTX
GRID_W = 64
GW = 256
CONV_W = 4
LRU_HEADS, LRU_HD, LRU_C = 4, 64, 8.0
HG_HEADS, HG_HD = 4, 64
SSD_HEADS, SSD_HD, SSD_GROUPS, SSD_STATE = 4, 64, 2, 64
SSD_XBC = GW + 2 * SSD_GROUPS * SSD_STATE
MLA_HEADS, MLA_QR, MLA_KVR, MLA_NOPE, MLA_ROPE = 4, 192, 128, 64, 32
MLA_V = GW // MLA_HEADS
ROPE_BASE = 10000.0
N_EXP, FF, EC_FACTOR = 16, 512, 2
N_MOD = 6
EPS = 1e-6
LRU_COLS = 2 * GW
HG_COLS = 5 * GW
SSD_COLS = GW + SSD_XBC + 2 * SSD_HEADS
MLA_COLS = MLA_QR + MLA_KVR + MLA_ROPE

LANES = 128
SUBLANES = 8
ROW_TILE = 256
N_TILES = NT // ROW_TILE
LAT_TILES = N_LAT // ROW_TILE
UA_W, UB_W, US_W, UM_W = 512, 1280, 896, 512
U_OFFS = (0, UA_W, UA_W + UB_W, UA_W + UB_W + US_W, UA_W + UB_W + US_W + UM_W)
SSD_CH = 128
HG_CH = 64
VMEM_LIMIT = 52 * 1024 * 1024
HEAD_SHIFT = 6
HB_SHIFT = 7
assert LRU_HD == HG_HD == SSD_HD == MLA_V == 1 << HEAD_SHIFT and LANES == 1 << HB_SHIFT


def _cparams(sem):
    return pltpu.CompilerParams(dimension_semantics=sem, vmem_limit_bytes=VMEM_LIMIT)


def _sigmoid(x):
    return 1.0 / (1.0 + jnp.exp(-x))


def _silu(x):
    return x * _sigmoid(x)


def _softplus(x):
    return jnp.maximum(x, 0.0) + jnp.log(1.0 + jnp.exp(-jnp.abs(x)))


def _gelu_tanh(x):
    return 0.5 * x * (1.0 + jnp.tanh(math.sqrt(2.0 / math.pi) * (x + 0.044715 * (x * x * x))))


def _dot(a, b, **kw):
    return jnp.dot(a, b, preferred_element_type=F32, **kw)


def _dot_nt(a, b, **kw):
    return lax.dot_general(a, b, (((1,), (1,)), ((), ())), preferred_element_type=F32, **kw)


def _dot_tn(a, b, **kw):
    return lax.dot_general(a, b, (((0,), (0,)), ((), ())), preferred_element_type=F32, **kw)


def _split_bf16(x, n):
    parts, rest = [], x
    for _ in range(n):
        p = rest.astype(BF16)
        parts.append(p)
        rest = rest - p.astype(F32)
    return parts


ADA_ROWS = 24
ADA_TN = 512


def _ada_kernel(c_ref, w_ref, b_ref, o_ref):
    s = _silu(c_ref[...])
    o_ref[0] = _dot(s, w_ref[0], precision=HI) + b_ref[0]


def _ada_call(cc, ada_w, ada_b):
    return pl.pallas_call(
        _ada_kernel,
        grid=(DEPTH, N_MOD * D // ADA_TN),
        in_specs=[
            pl.BlockSpec((ADA_ROWS, D), lambda l, j: (0, 0)),
            pl.BlockSpec((1, D, ADA_TN), lambda l, j: (l, 0, j)),
            pl.BlockSpec((1, 1, ADA_TN), lambda l, j: (l, 0, j)),
        ],
        out_specs=pl.BlockSpec((1, ADA_ROWS, ADA_TN), lambda l, j: (l, 0, j)),
        out_shape=jax.ShapeDtypeStruct((DEPTH, ADA_ROWS, N_MOD * D), F32),
        compiler_params=_cparams(("parallel", "parallel")),
        name="ada",
    )(cc, ada_w, ada_b.reshape(DEPTH, 1, N_MOD * D))


def _norm_mod(x, nw, shift, scale):
    ms = jnp.mean(x * x, axis=-1, keepdims=True)
    return (x * lax.rsqrt(ms + EPS) * nw) * (1.0 + scale) + shift


def _in_kernel(split, *refs):
    if split:
        x_ref, c_ref, mod_ref, nw_ref, w_ref, ua_ref, ub_ref, us_ref, um_ref = refs
        x = jnp.where(pl.program_id(1) < LAT_TILES, x_ref[0], c_ref[0])
    else:
        x_ref, mod_ref, nw_ref, w_ref, ua_ref, ub_ref, us_ref, um_ref = refs
        x = x_ref[0]
    h = _norm_mod(x, nw_ref[...], mod_ref[0, 0:1, :], mod_ref[0, 1:2, :]).astype(BF16)
    for k, ref in enumerate((ua_ref, ub_ref, us_ref, um_ref)):
        ref[0] = _dot(h, w_ref[:, U_OFFS[k]:U_OFFS[k + 1]])


def _mod_index(b, t, nb):
    return (jnp.where(t >= LAT_TILES, nb, b), 0, 0)


def _token_specs(ctx, tile_of=lambda s: s):
    if ctx is None:
        return [pl.BlockSpec((1, ROW_TILE, D), lambda b, s: (b, tile_of(s), 0))]
    return [pl.BlockSpec((1, ROW_TILE, D), lambda b, s: (b, jnp.minimum(tile_of(s), LAT_TILES - 1), 0)),
            pl.BlockSpec((1, N_CTX, D), lambda b, s: (b, 0, 0))]


def _in_call(x, ctx, modt, nw, w_all):
    nb = x.shape[0]
    in_specs = _token_specs(ctx) + [
        pl.BlockSpec((1, N_MOD, D), lambda b, t: _mod_index(b, t, nb)),
        pl.BlockSpec((1, D), lambda b, t: (0, 0)),
        pl.BlockSpec((D, U_OFFS[-1]), lambda b, t: (0, 0)),
    ]
    widths = (UA_W, UB_W, US_W, UM_W)
    out_specs = [pl.BlockSpec((1, ROW_TILE, w), lambda b, t: (b, t, 0)) for w in widths]
    out_shape = [jax.ShapeDtypeStruct((nb, NT, w), F32) for w in widths]
    args = ((x,) if ctx is None else (x, ctx)) + (modt, nw, w_all)
    return pl.pallas_call(
        functools.partial(_in_kernel, ctx is not None),
        grid=(nb, N_TILES),
        in_specs=in_specs,
        out_specs=out_specs,
        out_shape=out_shape,
        compiler_params=_cparams(("parallel", "parallel")),
        name="in_proj",
    )(*args)


def _conv_chunk(u_ref, r0, rows, c0, c1, cw_ref, cb_ref, at_seg_start, at_seg_end):
    x = u_ref[0, pl.ds(r0, rows), c0:c1]
    rp = pl.multiple_of(jnp.maximum(r0 - SUBLANES, 0), SUBLANES)
    rn = pl.multiple_of(jnp.minimum(r0 + rows, NT - SUBLANES), SUBLANES)
    xp = u_ref[0, pl.ds(rp, SUBLANES), c0:c1] * jnp.where(at_seg_start, 0.0, 1.0)
    xn = u_ref[0, pl.ds(rn, SUBLANES), c0:c1] * jnp.where(at_seg_end, 0.0, 1.0)
    xe = jnp.concatenate([xp, x, xn], axis=0)
    tot = rows + 2 * SUBLANES
    lo, hi = SUBLANES, SUBLANES + rows
    xm2 = pltpu.roll(xe, 2, 0)[lo:hi]
    xm1 = pltpu.roll(xe, 1, 0)[lo:hi]
    xp1 = pltpu.roll(xe, tot - 1, 0)[lo:hi]
    return cw_ref[0:1, :] * xm2 + cw_ref[1:2, :] * xm1 + cw_ref[2:3, :] * x + cw_ref[3:4, :] * xp1 + cb_ref[...]


LRU_CH = 256
LRU_NCH = NT // LRU_CH
LRU_LAT_CH = N_LAT // LRU_CH


def _scan_chunk(a, b, row_in_tile, hprev, reverse):
    n = a.shape[0]
    for s in (1, 2, 4):
        if reverse:
            a_s = pltpu.roll(a, n - s, 0)
            b_s = pltpu.roll(b, n - s, 0)
            valid = row_in_tile < SUBLANES - s
        else:
            a_s = pltpu.roll(a, s, 0)
            b_s = pltpu.roll(b, s, 0)
            valid = row_in_tile >= s
        b = b + a * jnp.where(valid, b_s, 0.0)
        a = a * jnp.where(valid, a_s, 1.0)
    tiles = n // SUBLANES
    hs = [None] * tiles
    for j in (reversed(range(tiles)) if reverse else range(tiles)):
        lo = j * SUBLANES
        hj = b[lo:lo + SUBLANES] + a[lo:lo + SUBLANES] * hprev
        hprev = hj[0:1] if reverse else hj[SUBLANES - 1:SUBLANES]
        hs[j] = hj
    return jnp.concatenate(hs, axis=0), hprev


def _lru_kernel(u_ref, cw_ref, cb_ref, wri_ref, bri_ref, lam_ref, y_ref, ab_s, h_s):
    row_in_tile = lax.broadcasted_iota(jnp.int32, (LRU_CH, GW), 0) & (SUBLANES - 1)
    sp = _softplus(-lam_ref[...])

    def prepare(c, carry):
        r0 = pl.multiple_of(c * LRU_CH, LRU_CH)
        is_ctx = c == LRU_LAT_CH
        xc = _conv_chunk(u_ref, r0, LRU_CH, 0, GW, cw_ref, cb_ref, (c == 0) | is_ctx, (c == LRU_LAT_CH - 1) | is_ctx)
        g = _dot(xc.astype(BF16), wri_ref[...]) + bri_ref[...]
        for d in (0, 1):
            r = _sigmoid(g[:, 2 * d * GW:(2 * d + 1) * GW])
            ig = _sigmoid(g[:, (2 * d + 1) * GW:(2 * d + 2) * GW])
            a = jnp.exp(-LRU_C * r * sp[d:d + 1, :])
            ab_s[2 * d, pl.ds(r0, LRU_CH), :] = a
            ab_s[2 * d + 1, pl.ds(r0, LRU_CH), :] = jnp.sqrt(1.0 - a * a) * (ig * xc)
        return carry

    lax.fori_loop(0, LRU_NCH, prepare, 0)

    def body(i, carry):
        hf, hb = carry
        rf = pl.multiple_of(((i + LRU_LAT_CH) % LRU_NCH) * LRU_CH, LRU_CH)
        rb = pl.multiple_of((LRU_LAT_CH - i) * LRU_CH, LRU_CH)
        h, hf = _scan_chunk(ab_s[0, pl.ds(rf, LRU_CH), :], ab_s[1, pl.ds(rf, LRU_CH), :], row_in_tile, hf, False)
        h_s[0, pl.ds(rf, LRU_CH), :] = h
        h, hb = _scan_chunk(ab_s[2, pl.ds(rb, LRU_CH), :], ab_s[3, pl.ds(rb, LRU_CH), :], row_in_tile, hb, True)
        h_s[1, pl.ds(rb, LRU_CH), :] = h
        return hf, hb

    zero = jnp.zeros((1, GW), F32)
    lax.fori_loop(0, LRU_NCH, body, (zero, zero))

    def finish(c, carry):
        r0 = pl.multiple_of(c * LRU_CH, LRU_CH)
        gate = u_ref[0, pl.ds(r0, LRU_CH), GW:2 * GW]
        hsum = h_s[0, pl.ds(r0, LRU_CH), :] + h_s[1, pl.ds(r0, LRU_CH), :]
        y_ref[0, pl.ds(r0, LRU_CH), :] = (hsum * _gelu_tanh(gate)).astype(BF16)
        return carry

    lax.fori_loop(0, LRU_NCH, finish, 0)


def _lru_call(ua, cw, cb, wri, bri, lam):
    nb = ua.shape[0]
    full = lambda shape: pl.BlockSpec(shape, lambda b: (0,) * len(shape))
    return pl.pallas_call(
        _lru_kernel,
        grid=(nb,),
        in_specs=[
            pl.BlockSpec((1, NT, UA_W), lambda b: (b, 0, 0)),
            full((CONV_W, GW)), full((1, GW)), full((GW, 4 * GW)), full((1, 4 * GW)), full((2, GW)),
        ],
        out_specs=pl.BlockSpec((1, NT, GW), lambda b: (b, 0, 0)),
        out_shape=jax.ShapeDtypeStruct((nb, NT, GW), BF16),
        scratch_shapes=[pltpu.VMEM((4, NT, GW), F32), pltpu.VMEM((2, NT, GW), F32)],
        compiler_params=_cparams(("parallel",)),
        name="rglru",
    )(ua, cw, cb, wri, bri, lam)


HG_NCH = NT // HG_CH
HG_LAT_CH = N_LAT // HG_CH
HG_LEVELS = HG_CH.bit_length() - 1
HG_EXP_BLOCKS = 1 + HG_LEVELS
HG_SPLIT = 3


def _hgrn_tables():
    t_ = HG_CH
    dm = np.zeros((2, HG_EXP_BLOCKS, t_, t_), np.float32)
    lm = np.zeros((2, HG_LEVELS + 1, t_, t_), np.float32)
    for d in (0, 1):
        for t in range(t_):
            if d == 0:
                dm[d, 0, t, :t + 1] = 1.0
            else:
                dm[d, 0, t, t:] = 1.0
            lm[d, 0, t, t] = 1.0
        for lev in range(1, HG_LEVELS + 1):
            m = 1 << (lev - 1)
            for t in range(t_):
                start = (t // (2 * m)) * 2 * m
                mid = start + m
                upper = t >= mid
                if d == 0:
                    if upper:
                        dm[d, lev, t, mid:t + 1] = 1.0
                        lm[d, lev, t, start:mid] = 1.0
                    else:
                        dm[d, lev, t, t + 1:mid] = 1.0
                else:
                    if not upper:
                        dm[d, lev, t, t:mid] = 1.0
                        lm[d, lev, t, mid:start + 2 * m] = 1.0
                    else:
                        dm[d, lev, t, mid:t] = 1.0
    dm = dm.reshape(2, HG_EXP_BLOCKS * t_, t_)
    d3 = np.concatenate([dm] * HG_SPLIT, axis=-1)
    lm = np.tile(lm, (1, 1, 1, HG_HEADS))
    heads = np.arange(GW) // HG_HD
    bm = (heads[:, None] == heads[None, :]).astype(np.float32)
    return jnp.asarray(d3, BF16), jnp.asarray(lm, F32), jnp.asarray(bm, F32), jnp.asarray(bm, BF16)


def _hgrn_kernel(u_ref, lb_ref, nw_ref, d3_ref, lm_ref, bm_ref, bmb_ref, y_ref, o_s, st_s, stb_s):
    t = HG_CH
    lb = lb_ref[...]
    st_s[...] = jnp.zeros(st_s.shape, F32)
    stb_s[...] = jnp.zeros(stb_s.shape, BF16)

    def by_head(xb):
        return jnp.concatenate([xb] * HG_HEADS, axis=0) * bmb_ref[...]

    def chunk(d, c):
        r0 = pl.multiple_of(c * t, t)
        q = _silu(u_ref[0, pl.ds(r0, t), 0:GW])
        fr = u_ref[0, pl.ds(r0, t), (1 + d) * GW:(2 + d) * GW]
        v = u_ref[0, pl.ds(r0, t), 3 * GW:4 * GW]
        f = lb + (1.0 - lb) * _sigmoid(fr)
        k = 1.0 - f
        ex = _dot(d3_ref[d], jnp.concatenate(_split_bf16(jnp.log2(f), HG_SPLIT), axis=0))
        cum = ex[0:t]
        a = _dot_nt(q.astype(BF16), by_head(k.astype(BF16))) * lm_ref[d, 0]
        for lev in range(1, HG_LEVELS + 1):
            fac = jnp.exp2(ex[lev * t:(lev + 1) * t])
            a = a + _dot_nt((q * fac).astype(BF16), by_head((k * fac).astype(BF16))) * lm_ref[d, lev]
        vb = v.astype(BF16)
        o = _dot(a.astype(BF16), by_head(vb)) + _dot_nt((q * jnp.exp2(cum)).astype(BF16), stb_s[d])
        o_s[d, pl.ds(r0, t), :] = o
        cend = cum[t - 1:t, :] if d == 0 else cum[0:1, :]
        kend = (k * jnp.exp2(cend - cum)).astype(BF16)
        st = st_s[d] * jnp.exp2(cend) + _dot_tn(vb, kend) * bm_ref[...]
        st_s[d] = st
        stb_s[d] = st.astype(BF16)

    def body(i, carry):
        chunk(0, (i + HG_LAT_CH) % HG_NCH)
        chunk(1, HG_NCH - 1 - i)
        return carry

    lax.fori_loop(0, HG_NCH, body, 0, unroll=4)

    def finish(c, carry):
        r0 = pl.multiple_of(c * t, t)
        osum = o_s[0, pl.ds(r0, t), :] + o_s[1, pl.ds(r0, t), :]
        sq = _split_bf16(osum * osum, 2)
        ms = _dot(jnp.concatenate(sq, axis=1), jnp.concatenate([bmb_ref[...]] * 2, axis=0)) * (1.0 / HG_HD)
        g = u_ref[0, pl.ds(r0, t), 4 * GW:5 * GW]
        y_ref[0, pl.ds(r0, t), :] = (osum * lax.rsqrt(ms + EPS) * nw_ref[...] * _silu(g)).astype(BF16)
        return carry

    lax.fori_loop(0, HG_NCH, finish, 0, unroll=4)


def _hgrn_call(ub, lb, nw, tables):
    nb = ub.shape[0]
    d3, lm, bm, bmb = tables
    full = lambda shape: pl.BlockSpec(shape, lambda b: (0,) * len(shape))
    return pl.pallas_call(
        _hgrn_kernel,
        grid=(nb,),
        in_specs=[pl.BlockSpec((1, NT, UB_W), lambda b: (b, 0, 0)), full((1, GW)), full((1, GW)),
                  full(d3.shape), full(lm.shape), full(bm.shape), full(bmb.shape)],
        out_specs=pl.BlockSpec((1, NT, GW), lambda b: (b, 0, 0)),
        out_shape=jax.ShapeDtypeStruct((nb, NT, GW), BF16),
        scratch_shapes=[pltpu.VMEM((2, NT, GW), F32), pltpu.VMEM((2, GW, GW), F32), pltpu.VMEM((2, GW, GW), BF16)],
        compiler_params=_cparams(("parallel",)),
        name="hgrn2",
    )(ub, lb, nw, d3, lm, bm, bmb)


SSD_NCH = NT // SSD_CH
SSD_LAT_CH = N_LAT // SSD_CH
SSD_DT0 = GW + SSD_XBC
GRP_W = SSD_HEADS // SSD_GROUPS * SSD_HD


SSD_BC_W = 2 * SSD_GROUPS * SSD_STATE
SSD_CUM_SPLIT = 3
SSD_GAIN_SPLIT = 2


def _ssd_tables():
    t = SSD_CH
    tri = np.stack([np.tril(np.ones((t, t), np.float32)), np.triu(np.ones((t, t), np.float32))])
    col = np.zeros((2, LANES, SSD_HEADS * LANES), np.float32)
    wide = np.zeros((2, LANES, GW), np.float32)
    for d in range(2):
        for h in range(SSD_HEADS):
            col[d, d * SSD_HEADS + h, h * LANES:(h + 1) * LANES] = 1.0
            wide[d, d * SSD_HEADS + h, h * SSD_HD:(h + 1) * SSD_HD] = 1.0
    hm = (np.arange(GW)[None, :] // SSD_HD == np.arange(SSD_HEADS)[:, None]).astype(np.float32)
    gm = (np.arange(LANES)[None, :] // SSD_STATE == np.arange(SSD_GROUPS)[:, None]).astype(np.float32)
    return (jnp.asarray(np.concatenate([tri] * SSD_CUM_SPLIT, axis=2), BF16),
            jnp.asarray(np.concatenate([col] * SSD_CUM_SPLIT, axis=1), BF16),
            jnp.asarray(np.concatenate([wide] * SSD_GAIN_SPLIT, axis=1), BF16),
            jnp.asarray(hm[:, None, :], BF16), jnp.asarray(gm[:, None, :], BF16))


def _ssd_kernel(u_ref, cw_ref, cb_ref, a8_ref, b8_ref, dsk_ref, nw_ref, tri_ref, col_ref, wide_ref, hm_ref, gm_ref,
                y_ref, xs_s, bc_s, xm_s, dt_s, y_s, st_s):
    t = SSD_CH
    rr = lax.broadcasted_iota(jnp.int32, (t, t), 0)
    cc = lax.broadcasted_iota(jnp.int32, (t, t), 1)
    keeps = (rr >= cc, rr <= cc)
    a8 = -jnp.exp(a8_ref[...])

    def prepare(c, carry):
        r0 = pl.multiple_of(c * t, t)
        seg_start = (c == 0) | (c == SSD_LAT_CH)
        seg_end = (c == SSD_LAT_CH - 1) | (c == SSD_NCH - 1)
        xbc = _silu(_conv_chunk(u_ref, r0, t, GW, GW + SSD_XBC, cw_ref, cb_ref, seg_start, seg_end))
        xs = xbc[:, 0:GW]
        xs_s[pl.ds(r0, t), :] = xs
        bc_s[pl.ds(r0, t), :] = xbc[:, GW:GW + SSD_BC_W].astype(BF16)
        xb = xs.astype(BF16)
        for h in range(SSD_HEADS):
            xm_s[h, pl.ds(r0, t), :] = xb * hm_ref[h]
        dt_s[pl.ds(r0, t), :] = _softplus(u_ref[0, pl.ds(r0, t), SSD_DT0:SSD_DT0 + LANES] + b8_ref[...])
        return carry

    lax.fori_loop(0, SSD_NCH, prepare, 0)
    st_s[...] = jnp.zeros(st_s.shape, F32)

    def chunk(d, c):
        r0 = pl.multiple_of(c * t, t)
        dtn = dt_s[pl.ds(r0, t), :]
        cumn = _dot(tri_ref[d], jnp.concatenate(_split_bf16(dtn * a8, SSD_CUM_SPLIT), axis=0))
        cend = cumn[t - 1:t, :] if d == 0 else cumn[0:1, :]
        cum_t = cumn.T
        dt_t = dtn.T
        colb = _dot(jnp.concatenate(_split_bf16(cumn, SSD_CUM_SPLIT), axis=1), col_ref[d])
        gains = jnp.concatenate([jnp.exp(cumn), dtn * jnp.exp(cend - cumn)], axis=0)
        gw = _dot(jnp.concatenate(_split_bf16(gains, SSD_GAIN_SPLIT), axis=1), wide_ref[d])
        ecum, wend = gw[0:t], gw[t:2 * t]
        bc = bc_s[pl.ds(r0, t), :]
        bmat, cmat = bc[:, 0:LANES], bc[:, LANES:2 * LANES]
        st = st_s[d]
        y = jnp.zeros((t, GW), F32)
        y_in = []
        for g in range(SSD_GROUPS):
            cg = cmat * gm_ref[g]
            scores = _dot_nt(cg, bmat)
            sg = st[:, g * GRP_W:(g + 1) * GRP_W].astype(BF16)
            y_in.append(_dot(cg, jnp.concatenate([sg] * SSD_GROUPS, axis=0)))
            for hh in range(SSD_HEADS // SSD_GROUPS):
                h = g * (SSD_HEADS // SSD_GROUPS) + hh
                row = d * SSD_HEADS + h
                seg = colb[:, h * LANES:(h + 1) * LANES] - cum_t[row:row + 1, :]
                m = jnp.where(keeps[d], jnp.exp(seg), 0.0) * scores * dt_t[row:row + 1, :]
                y = y + _dot(m.astype(BF16), xm_s[h, pl.ds(r0, t), :])
        y_s[d, pl.ds(r0, t), :] = y + jnp.concatenate(y_in, axis=1) * ecum
        xw = (xs_s[pl.ds(r0, t), :] * wend).astype(BF16)
        upd = [_dot_tn(bmat, xw[:, g * GRP_W:(g + 1) * GRP_W])[g * SSD_STATE:(g + 1) * SSD_STATE]
               for g in range(SSD_GROUPS)]
        eend = ecum[t - 1:t, :] if d == 0 else ecum[0:1, :]
        st_s[d] = st * eend + jnp.concatenate(upd, axis=1)

    def body(i, carry):
        chunk(0, (i + SSD_LAT_CH) % SSD_NCH)
        chunk(1, SSD_NCH - 1 - i)
        return carry

    lax.fori_loop(0, SSD_NCH, body, 0, unroll=3)

    def finish(c, carry):
        r0 = pl.multiple_of(c * t, t)
        z = u_ref[0, pl.ds(r0, t), 0:GW]
        yy = (y_s[0, pl.ds(r0, t), :] + y_s[1, pl.ds(r0, t), :] + dsk_ref[...] * xs_s[pl.ds(r0, t), :]) * _silu(z)
        ms = jnp.mean(yy * yy, axis=-1, keepdims=True)
        y_ref[0, pl.ds(r0, t), :] = (yy * lax.rsqrt(ms + EPS) * nw_ref[...]).astype(BF16)
        return carry

    lax.fori_loop(0, SSD_NCH, finish, 0, unroll=2)


def _ssd_call(us, cw, cb, a8, b8, dsk, nw, tables):
    nb = us.shape[0]
    full = lambda shape: pl.BlockSpec(shape, lambda b: (0,) * len(shape))
    return pl.pallas_call(
        _ssd_kernel,
        grid=(nb,),
        in_specs=[
            pl.BlockSpec((1, NT, US_W), lambda b: (b, 0, 0)),
            full((CONV_W, SSD_XBC)), full((1, SSD_XBC)), full((1, LANES)), full((1, LANES)),
            full((1, GW)), full((1, GW)),
        ] + [full(tb.shape) for tb in tables],
        out_specs=pl.BlockSpec((1, NT, GW), lambda b: (b, 0, 0)),
        out_shape=jax.ShapeDtypeStruct((nb, NT, GW), BF16),
        scratch_shapes=[pltpu.VMEM((NT, GW), F32), pltpu.VMEM((NT, SSD_BC_W), BF16),
                        pltpu.VMEM((SSD_HEADS, NT, GW), BF16), pltpu.VMEM((NT, LANES), F32),
                        pltpu.VMEM((2, NT, GW), F32), pltpu.VMEM((2, SSD_STATE, GW), F32)],
        compiler_params=_cparams(("parallel",)),
        name="ssd",
    )(us, cw, cb, a8, b8, dsk, nw, *tables)


HB = LANES
QW = MLA_HEADS * HB
MLA_QB = 256
MLA_AB = 256
MLA_SCALE = (MLA_NOPE + MLA_ROPE) ** -0.5


def _mla_kernel(need_ctx, u_ref, qa_ref, wq_ref, qn_ref, kva_ref, wkv_ref, kn_ref, krw_ref,
                cos_ref, sa_ref, sb_ref, y_ref, q_s, k_s, v_s):
    lane = lax.broadcasted_iota(jnp.int32, (QW, QW), 1)
    row = lax.broadcasted_iota(jnp.int32, (QW, QW), 0)

    def grp(i):
        within = i & (HB - 1)
        return (i >> HB_SHIFT) * 3 + jnp.where(within < MLA_NOPE, 0, jnp.where(within < MLA_NOPE + MLA_ROPE, 1, 2))

    gmat = jnp.where(grp(row) == grp(lane), 1.0, 0.0).astype(BF16)
    l1 = lax.broadcasted_iota(jnp.int32, (1, QW), 1) & (HB - 1)
    inv_size = jnp.where(l1 < MLA_NOPE, 1.0 / MLA_NOPE, 1.0 / MLA_ROPE)
    vlane_head = lax.broadcasted_iota(jnp.int32, (MLA_QB, GW), 1) >> HEAD_SHIFT

    def rope(x, r0, reps):
        cosr = cos_ref[pl.ds(r0, MLA_QB), :]
        sar = sa_ref[pl.ds(r0, MLA_QB), :]
        sbr = sb_ref[pl.ds(r0, MLA_QB), :]
        if reps > 1:
            cosr, sar, sbr = (jnp.concatenate([tbl] * reps, axis=1) for tbl in (cosr, sar, sbr))
        w = x.shape[1]
        return x * cosr + pltpu.roll(x, w - MLA_ROPE // 2, 1) * sar + pltpu.roll(x, MLA_ROPE // 2, 1) * sbr

    def project(ci, carry):
        r0 = pl.multiple_of(ci * MLA_QB, MLA_QB)
        cq = u_ref[0, pl.ds(r0, MLA_QB), 0:256]
        ms = jnp.sum(cq * cq, axis=-1, keepdims=True) * (1.0 / MLA_QR)
        qraw = _dot((cq * lax.rsqrt(ms + EPS) * qa_ref[...]).astype(BF16), wq_ref[...])
        ss = _dot((qraw * qraw).astype(BF16), gmat) * inv_size
        q = rope(qraw * lax.rsqrt(ss + EPS) * qn_ref[...], r0, MLA_HEADS) * MLA_SCALE
        q_s[pl.ds(r0, MLA_QB), :] = q.astype(BF16)
        ckv = u_ref[0, pl.ds(r0, MLA_QB), 256:256 + MLA_KVR]
        ms = jnp.mean(ckv * ckv, axis=-1, keepdims=True)
        kv = _dot((ckv * lax.rsqrt(ms + EPS) * kva_ref[...]).astype(BF16), wkv_ref[...])
        kraw = kv[:, 0:QW]
        ss = _dot((kraw * kraw).astype(BF16), gmat) * inv_size
        knope = kraw * lax.rsqrt(ss + EPS) * kn_ref[...]
        kr = u_ref[0, pl.ds(r0, MLA_QB), 384:512]
        ms = jnp.sum(kr * kr, axis=-1, keepdims=True) * (1.0 / MLA_ROPE)
        krope = rope(kr * lax.rsqrt(ms + EPS) * krw_ref[...], r0, 1)
        k_s[pl.ds(r0, MLA_QB), :] = (knope + jnp.concatenate([krope] * MLA_HEADS, axis=1)).astype(BF16)
        vv = kv[:, QW:QW + GW]
        for h in range(MLA_HEADS):
            v_s[h, pl.ds(r0, MLA_QB), :] = jnp.where(vlane_head == h, vv, 0.0).astype(BF16)
        return carry

    lax.fori_loop(0, NT // MLA_QB, project, 0, unroll=3)

    def attend(r0, rows, k0, klen):
        o = jnp.zeros((rows, GW), F32)
        for h in range(MLA_HEADS):
            qh = q_s[pl.ds(r0, rows), h * HB:(h + 1) * HB]
            s = _dot_nt(qh, k_s[k0:k0 + klen, h * HB:(h + 1) * HB])
            p = jnp.exp(s - jnp.max(s, axis=-1, keepdims=True))
            l = jnp.sum(p, axis=-1, keepdims=True)
            o = o + _dot(p.astype(BF16), v_s[h, k0:k0 + klen, :]) * (1.0 / l)
        y_ref[0, pl.ds(r0, rows), :] = o.astype(BF16)

    def lat_block(qi, carry):
        attend(pl.multiple_of(qi * MLA_AB, MLA_AB), MLA_AB, 0, NT)
        return carry

    lax.fori_loop(0, N_LAT // MLA_AB, lat_block, 0, unroll=4)
    if need_ctx:
        attend(N_LAT, N_CTX, N_LAT, N_CTX)
    else:
        y_ref[0, N_LAT:NT, :] = jnp.zeros((N_CTX, GW), BF16)


def _mla_call(um, need_ctx, qa, wq, qn, kva, wkv, kn, krw, cos_t, sa_t, sb_t):
    nb = um.shape[0]
    full = lambda shape: pl.BlockSpec(shape, lambda b: (0,) * len(shape))
    return pl.pallas_call(
        functools.partial(_mla_kernel, need_ctx),
        grid=(nb,),
        in_specs=[
            pl.BlockSpec((1, NT, UM_W), lambda b: (b, 0, 0)),
            full((1, 256)), full((256, QW)), full((1, QW)),
            full((1, MLA_KVR)), full((MLA_KVR, QW + GW)), full((1, QW)), full((1, HB)),
            full((NT, HB)), full((NT, HB)), full((NT, HB)),
        ],
        out_specs=pl.BlockSpec((1, NT, GW), lambda b: (b, 0, 0)),
        out_shape=jax.ShapeDtypeStruct((nb, NT, GW), BF16),
        scratch_shapes=[pltpu.VMEM((NT, QW), BF16), pltpu.VMEM((NT, QW), BF16),
                        pltpu.VMEM((MLA_HEADS, NT, GW), BF16)],
        compiler_params=_cparams(("parallel",)),
        name="mla",
    )(um, qa, wq, qn, kva, wkv, kn, krw, cos_t, sa_t, sb_t)


def _out_tile(s):
    return jnp.maximum(s - 1, 0)


def _out_kernel(split, *refs):
    if split:
        x_ref, c_ref, ya_ref, yb_ref, ys_ref, ym_ref, wo_ref, mod_ref, nw_ref, wr_ref = refs[:10]
    else:
        x_ref, ya_ref, yb_ref, ys_ref, ym_ref, wo_ref, mod_ref, nw_ref, wr_ref = refs[:9]
    xo_ref, h2_ref, aff_ref, acc_s = refs[-4:]
    s = pl.program_id(1)

    @pl.when(s == 0)
    def _first():
        acc_s[1] = jnp.zeros(acc_s.shape[1:], F32)

    def step(slot):
        y = jnp.concatenate([ya_ref[0], yb_ref[0], ys_ref[0], ym_ref[0]], axis=1)
        acc_s[slot] = _dot(y, wo_ref[...])
        x_in = jnp.where(_out_tile(s) < LAT_TILES, x_ref[0], c_ref[0]) if split else x_ref[0]
        x = x_in + mod_ref[0, 2:3, :] * acc_s[1 - slot]
        xo_ref[0] = x
        h2 = _norm_mod(x, nw_ref[...], mod_ref[0, 3:4, :], mod_ref[0, 4:5, :])
        hi = h2.astype(BF16)
        h2_ref[0] = hi
        lo = (h2 - hi.astype(F32)).astype(BF16)
        wr = wr_ref[...]
        w_hi = wr.astype(BF16)
        w_lo = (wr - w_hi.astype(F32)).astype(BF16)
        logit = _dot_nt(jnp.concatenate([w_hi, w_hi, w_lo], axis=1), jnp.concatenate([hi, lo, hi], axis=1))
        e = jnp.exp(logit - jnp.max(logit, axis=0, keepdims=True))
        aff_ref[0] = e / jnp.sum(e, axis=0, keepdims=True)

    for parity in (0, 1):
        pl.when(s % 2 == parity)(functools.partial(step, parity))


def _out_call(x, ctx, ys, wo, modt, nw, wr_t, n_tiles):
    nb = x.shape[0]
    proj = lambda w: pl.BlockSpec((1, ROW_TILE, w), lambda b, s: (b, jnp.minimum(s, n_tiles - 1), 0))
    done = lambda w: pl.BlockSpec((1, ROW_TILE, w), lambda b, s: (b, _out_tile(s), 0))
    x_args = (x,) if ctx is None else (x, ctx)
    return pl.pallas_call(
        functools.partial(_out_kernel, ctx is not None),
        grid=(nb, n_tiles + 1),
        in_specs=_token_specs(ctx, _out_tile) + [proj(GW), proj(GW), proj(GW), proj(GW),
                  pl.BlockSpec((D, D), lambda b, s: (0, 0)),
                  pl.BlockSpec((1, N_MOD, D), lambda b, s: _mod_index(b, _out_tile(s), nb)),
                  pl.BlockSpec((1, D), lambda b, s: (0, 0)),
                  pl.BlockSpec((N_EXP, D), lambda b, s: (0, 0))],
        out_specs=[done(D), done(D), pl.BlockSpec((1, N_EXP, ROW_TILE), lambda b, s: (b, 0, _out_tile(s)))],
        out_shape=[jax.ShapeDtypeStruct((nb, n_tiles * ROW_TILE, D), F32),
                   jax.ShapeDtypeStruct((nb, n_tiles * ROW_TILE, D), BF16),
                   jax.ShapeDtypeStruct((nb, N_EXP, n_tiles * ROW_TILE), F32)],
        scratch_shapes=[pltpu.VMEM((2, ROW_TILE, D), F32)],
        compiler_params=_cparams(("parallel", "arbitrary")),
        name="out_proj",
    )(*x_args, *ys, wo, modt, nw, wr_t)


PRE_W = 256
ROUTE_MAX_IT = 160


def _prefix_count(m, tri_bf):
    n = m.shape[1]
    off = jnp.zeros((m.shape[0], 1), F32)
    outs = []
    for j in range(n // PRE_W):
        blk = m[:, j * PRE_W:(j + 1) * PRE_W]
        outs.append(_dot(blk.astype(BF16), tri_bf) + off)
        off = off + jnp.sum(blk, axis=1, keepdims=True)
    return outs[0] if len(outs) == 1 else jnp.concatenate(outs, axis=1)


def _route(aff, cap, tri):
    n_lo0 = jnp.full((N_EXP, 1), float(aff.shape[1]), F32)

    def cond(s):
        it, _, _, n_lo, n_hi = s
        return jnp.logical_and(it < ROUTE_MAX_IT, jnp.max(n_lo - n_hi) > 1.0)

    def step(s):
        it, lo, hi, n_lo, n_hi = s
        width = hi - lo
        mids = [lo + frac * width for frac in (0.25, 0.5, 0.75)]
        cnts = [jnp.sum(jnp.where(aff > m, 1.0, 0.0), axis=1, keepdims=True) for m in mids]
        blws = [jnp.max(jnp.where(aff <= m, aff, -1.0), axis=1, keepdims=True) for m in mids]
        u0, u1, u2 = (c >= cap for c in cnts)
        pick = lambda a3, a2, a1, a0: jnp.where(u2, a3, jnp.where(u1, a2, jnp.where(u0, a1, a0)))
        return (it + 1, pick(mids[2], mids[1], mids[0], lo), pick(hi, blws[2], blws[1], blws[0]),
                pick(cnts[2], cnts[1], cnts[0], n_lo), pick(n_hi, cnts[2], cnts[1], cnts[0]))

    init = (jnp.int32(0), jnp.full((N_EXP, 1), -1.0, F32), jnp.max(aff, axis=1, keepdims=True),
            n_lo0, jnp.zeros((N_EXP, 1), F32))
    _, _, thr, _, n_gt = lax.while_loop(cond, step, init)
    gt = jnp.where(aff > thr, 1.0, 0.0)
    eq = jnp.where(aff == thr, 1.0, 0.0)
    sel = gt + eq * jnp.where(_prefix_count(eq, tri) < cap - n_gt, 1.0, 0.0)
    return jnp.where(sel > 0.0, _prefix_count(sel, tri), -1.0)


MOE_EPS = 2


def _moe_kernel(segs, x_hbm, aff_ref, h_ref, gl_ref, gc_ref, wg_ref, wu_ref, wd_ref, o_ref, pos_ref, sem):
    b = pl.program_id(0)
    step = pl.program_id(1)
    rows = o_ref.shape[1]

    def residual_copy():
        return pltpu.make_async_copy(x_hbm.at[b, pl.ds(0, rows), :], o_ref.at[0], sem)

    @pl.when(step == 0)
    def _init():
        residual_copy().start()
        tri = jnp.where(lax.broadcasted_iota(jnp.int32, (PRE_W, PRE_W), 0)
                        < lax.broadcasted_iota(jnp.int32, (PRE_W, PRE_W), 1), 1.0, 0.0).astype(BF16)
        for r0, n, cap in segs:
            pos_ref[:, r0:r0 + n] = _route(aff_ref[0, :, r0:r0 + n], cap, tri)
        residual_copy().wait()

    onehots, gates, xs = [], [], []
    for (r0, n, cap), g_ref in zip(segs, (gl_ref, gc_ref)):
        slot = lax.broadcasted_iota(jnp.int32, (cap, n), 0).astype(F32)
        oh, gt = [], []
        for j in range(MOE_EPS):
            e = step * MOE_EPS + j
            hit = slot == pos_ref[pl.ds(e, 1), r0:r0 + n]
            oh.append(jnp.where(hit, 1.0, 0.0).astype(BF16))
            gsel = jnp.sum(jnp.where(hit, aff_ref[0, pl.ds(e, 1), r0:r0 + n], 0.0), axis=1, keepdims=True)
            gt.append(gsel * g_ref[0, N_MOD - 1:N_MOD, :])
        onehot = jnp.concatenate(oh, axis=0)
        onehots.append(onehot)
        gates.append(gt)
        xs.append(_dot(onehot, h_ref[0, r0:r0 + n, :]).astype(BF16))
    ys = []
    for j in range(MOE_EPS):
        xj = [x[j * cap:(j + 1) * cap] for x, (_, _, cap) in zip(xs, segs)]
        xj = xj[0] if len(xj) == 1 else jnp.concatenate(xj, axis=0)
        act = (_silu(_dot(xj, wg_ref[j])) * _dot(xj, wu_ref[j])).astype(BF16)
        ys.append(_dot(act, wd_ref[j]))
    s0 = 0
    for (r0, n, cap), onehot, gt in zip(segs, onehots, gates):
        ysg = jnp.concatenate([(ys[j][s0:s0 + cap] * gt[j]).astype(BF16) for j in range(MOE_EPS)], axis=0)
        o_ref[0, r0:r0 + n, :] += _dot_tn(onehot, ysg)
        s0 += cap


def _moe_call(x_mid, aff, h2, modt, wg, wu, wd, with_ctx):
    nb = h2.shape[0]
    segs = ((0, N_LAT, EC_FACTOR * N_LAT // N_EXP),)
    if with_ctx:
        segs += ((N_LAT, N_CTX, EC_FACTOR * N_CTX // N_EXP),)
    rows = NT if with_ctx else N_LAT
    return pl.pallas_call(
        functools.partial(_moe_kernel, segs),
        grid=(nb, N_EXP // MOE_EPS),
        in_specs=[
            pl.BlockSpec(memory_space=pl.ANY),
            pl.BlockSpec((1, N_EXP, rows), lambda b, e: (b, 0, 0)),
            pl.BlockSpec((1, rows, D), lambda b, e: (b, 0, 0)),
            pl.BlockSpec((1, N_MOD, D), lambda b, e: (b, 0, 0)),
            pl.BlockSpec((1, N_MOD, D), lambda b, e: (nb, 0, 0)),
            pl.BlockSpec((MOE_EPS, D, FF), lambda b, e: (e, 0, 0)),
            pl.BlockSpec((MOE_EPS, D, FF), lambda b, e: (e, 0, 0)),
            pl.BlockSpec((MOE_EPS, FF, D), lambda b, e: (e, 0, 0)),
        ],
        out_specs=pl.BlockSpec((1, rows, D), lambda b, e: (b, 0, 0)),
        out_shape=jax.ShapeDtypeStruct((nb, rows, D), F32),
        scratch_shapes=[pltpu.VMEM((N_EXP, rows), F32), pltpu.SemaphoreType.DMA(())],
        compiler_params=_cparams(("parallel", "arbitrary")),
        name="moe",
    )(x_mid, aff, h2, modt, modt, wg, wu, wd)


def _pad_cols(w, width):
    return jnp.pad(w, ((0, 0), (0, width - w.shape[1])))


def _rope_perm():
    half = MLA_ROPE // 2
    return jnp.concatenate([jnp.arange(half) * 2, jnp.arange(half) * 2 + 1])


def _head_block(nope, rope):
    pad = jnp.zeros(nope.shape[:-1] + (HB - MLA_NOPE - MLA_ROPE,), nope.dtype)
    blk = jnp.concatenate([nope, rope, pad], axis=-1)
    return blk.reshape(blk.shape[:-2] + (MLA_HEADS * HB,))


def _prep_in_weights(w_in):
    o1, o2, o3 = LRU_COLS, LRU_COLS + HG_COLS, LRU_COLS + HG_COLS + SSD_COLS
    wa, wb, ws, wm = w_in[:, :o1], w_in[:, o1:o2], w_in[:, o2:o3], w_in[:, o3:]
    perm = _rope_perm()
    cq, ckv, kr = wm[:, :MLA_QR], wm[:, MLA_QR:MLA_QR + MLA_KVR], wm[:, MLA_QR + MLA_KVR:]
    zeros = lambda n: jnp.zeros((D, n), w_in.dtype)
    wm_p = jnp.concatenate([cq, zeros(256 - MLA_QR), ckv, zeros(MLA_NOPE), kr[:, perm],
                            zeros(HB - MLA_NOPE - MLA_ROPE)], axis=1)
    return jnp.concatenate([wa, wb, _pad_cols(ws, US_W), wm_p], axis=1).astype(BF16)


def _block_diag(w):
    h, dd, _ = w.shape
    eye = jnp.eye(h, dtype=w.dtype)
    return (eye[:, None, :, None] * w[:, :, None, :]).reshape(h * dd, h * dd)


def _rope_tables():
    rows = N_LAT // GRID_W
    row = jnp.repeat(jnp.arange(rows, dtype=F32), GRID_W)
    col = jnp.tile(jnp.arange(GRID_W, dtype=F32), rows)
    half = MLA_ROPE // 2
    inv = ROPE_BASE ** (-jnp.arange(0, half, 2, dtype=F32) / half)
    ang = jnp.concatenate([row[:, None] * inv, col[:, None] * inv], axis=-1)
    cos, sin = jnp.cos(ang), jnp.sin(ang)
    z = lambda n: jnp.zeros((N_LAT, n), F32)
    o = lambda n: jnp.ones((N_LAT, n), F32)
    cos_t = jnp.concatenate([o(MLA_NOPE), cos, cos, o(HB - MLA_NOPE - MLA_ROPE)], axis=1)
    sa_t = jnp.concatenate([z(MLA_NOPE), -sin, z(half), z(HB - MLA_NOPE - MLA_ROPE)], axis=1)
    sb_t = jnp.concatenate([z(MLA_NOPE), z(half), sin, z(HB - MLA_NOPE - MLA_ROPE)], axis=1)
    cos_t = jnp.concatenate([cos_t, jnp.ones((N_CTX, HB), F32)], axis=0)
    sa_t = jnp.concatenate([sa_t, jnp.zeros((N_CTX, HB), F32)], axis=0)
    sb_t = jnp.concatenate([sb_t, jnp.zeros((N_CTX, HB), F32)], axis=0)
    return cos_t, sa_t, sb_t


def kernel(x, c, ctx, c_ctx, ada_w, ada_b, norm1_w, norm2_w, w_in, w_out, lru_conv_w, lru_conv_b, lru_w_r, lru_b_r, lru_w_i, lru_b_i, lru_lam, hgrn_lb_logits, hgrn_norm_w, ssd_conv_w, ssd_conv_b, ssd_a_log, ssd_dt_bias, ssd_d_skip, ssd_norm_w, mla_q_a_norm, mla_w_q_up, mla_kv_a_norm, mla_w_kv_up, mla_q_norm, mla_k_norm, moe_router, moe_w_gate, moe_w_up, moe_w_down):
    nb = x.shape[0]
    assert x.shape == (nb, N_LAT, D) and ctx.shape == (nb, N_CTX, D)
    assert nb + 1 <= ADA_ROWS
    cc = jnp.concatenate([c, c_ctx[None, :], jnp.zeros((ADA_ROWS - nb - 1, D), F32)], axis=0)
    mod_all = _ada_call(cc, ada_w, ada_b)[:, :nb + 1].reshape(DEPTH, nb + 1, N_MOD, D)
    cos_t, sa_t, sb_t = _rope_tables()
    hg_tables = _hgrn_tables()
    ssd_tables = _ssd_tables()
    perm = _rope_perm()
    lb_w = jax.nn.softmax(hgrn_lb_logits.astype(F32), axis=0)
    lb_all = jnp.cumsum(lb_w, axis=0) - lb_w[0]
    rep = lambda v, n: jnp.repeat(v, n, axis=-1)
    stream = (x, ctx)
    for l in range(DEPTH):
        need_ctx = l < DEPTH - 1
        modt = mod_all[l]
        ua, ub, us, um = _in_call(*stream, modt, norm1_w[l][None, :], _prep_in_weights(w_in[l]))
        wri = jnp.concatenate([_block_diag(lru_w_r[l, 0]), _block_diag(lru_w_i[l, 0]),
                               _block_diag(lru_w_r[l, 1]), _block_diag(lru_w_i[l, 1])], axis=1).astype(BF16)
        bri = jnp.concatenate([lru_b_r[l, 0], lru_b_i[l, 0], lru_b_r[l, 1], lru_b_i[l, 1]])[None, :]
        ya = _lru_call(ua, lru_conv_w[l], lru_conv_b[l][None, :], wri, bri, lru_lam[l])
        yb = _hgrn_call(ub, lb_all[l][None, :], hgrn_norm_w[l][None, :], hg_tables)
        narrow = lambda v: jnp.pad(v.reshape(-1), (0, LANES - 2 * SSD_HEADS))[None, :]
        ysd = _ssd_call(us, ssd_conv_w[l], ssd_conv_b[l][None, :], narrow(ssd_a_log[l]), narrow(ssd_dt_bias[l]),
                        rep(ssd_d_skip[l], SSD_HD)[None, :], ssd_norm_w[l][None, :], ssd_tables)
        wq = mla_w_q_up[l].reshape(MLA_QR, MLA_HEADS, MLA_NOPE + MLA_ROPE)
        wq = _head_block(wq[..., :MLA_NOPE], wq[..., MLA_NOPE:][..., perm])
        wq = jnp.pad(wq, ((0, 256 - MLA_QR), (0, 0))).astype(BF16)
        qn = _head_block(jnp.broadcast_to(mla_q_norm[l][:MLA_NOPE], (MLA_HEADS, MLA_NOPE)),
                         jnp.broadcast_to(mla_q_norm[l][MLA_NOPE:][perm], (MLA_HEADS, MLA_ROPE)))[None, :]
        wkv = mla_w_kv_up[l].reshape(MLA_KVR, MLA_HEADS, MLA_NOPE + MLA_V)
        wk = _head_block(wkv[..., :MLA_NOPE], jnp.zeros((MLA_KVR, MLA_HEADS, MLA_ROPE), F32))
        wv = wkv[..., MLA_NOPE:].reshape(MLA_KVR, GW)
        wkv_p = jnp.concatenate([wk, wv], axis=1).astype(BF16)
        kn = _head_block(jnp.broadcast_to(mla_k_norm[l][:MLA_NOPE], (MLA_HEADS, MLA_NOPE)),
                         jnp.zeros((MLA_HEADS, MLA_ROPE), F32))[None, :]
        krw = jnp.concatenate([jnp.zeros((MLA_NOPE,), F32), mla_k_norm[l][MLA_NOPE:][perm],
                               jnp.zeros((HB - MLA_NOPE - MLA_ROPE,), F32)])[None, :]
        qa = jnp.pad(mla_q_a_norm[l], (0, 256 - MLA_QR))[None, :]
        ym = _mla_call(um, need_ctx, qa, wq, qn, mla_kv_a_norm[l][None, :], wkv_p, kn, krw, cos_t, sa_t, sb_t)
        n_tiles = N_TILES if need_ctx else LAT_TILES
        x_mid, h2, aff = _out_call(*stream, (ya, yb, ysd, ym), w_out[l].astype(BF16), modt, norm2_w[l][None, :],
                                   moe_router[l].T, n_tiles)
        wg, wu, wd = moe_w_gate[l].astype(BF16), moe_w_up[l].astype(BF16), moe_w_down[l].astype(BF16)
        stream = (_moe_call(x_mid, aff, h2, modt, wg, wu, wd, need_ctx), None)
    return stream[0]
```

```python
import functools
import math

import jax
import jax.numpy as jnp
import numpy as np
from jax import lax
from jax.experimental import pallas as pl
from jax.experimental.pallas import tpu as pltpu

F32 = jnp.float32
BF16 = jnp.bfloat16
HI = lax.Precision.HIGHEST

D = 1024
DEPTH = 2
N_LAT = 2048
N_CTX = 256
NT = N_LAT + N_CTX
GRID_W = 64
GW = 256
CONV_W = 4
LRU_HEADS, LRU_HD, LRU_C = 4, 64, 8.0
HG_HEADS, HG_HD = 4, 64
SSD_HEADS, SSD_HD, SSD_GROUPS, SSD_STATE = 4, 64, 2, 64
SSD_XBC = GW + 2 * SSD_GROUPS * SSD_STATE
MLA_HEADS, MLA_QR, MLA_KVR, MLA_NOPE, MLA_ROPE = 4, 192, 128, 64, 32
MLA_V = GW // MLA_HEADS
ROPE_BASE = 10000.0
N_EXP, FF, EC_FACTOR = 16, 512, 2
N_MOD = 6
EPS = 1e-6
LRU_COLS = 2 * GW
HG_COLS = 5 * GW
SSD_COLS = GW + SSD_XBC + 2 * SSD_HEADS

LANES = 128
SUBLANES = 8
ROW_TILE = 256
N_TILES = NT // ROW_TILE
LAT_TILES = N_LAT // ROW_TILE
UA_W, UB_W, US_W, UM_W = 512, 1280, 896, 512
U_OFFS = (0, UA_W, UA_W + UB_W, UA_W + UB_W + US_W, UA_W + UB_W + US_W + UM_W)
SSD_CH = 128
HG_CH = 64
VMEM_LIMIT = 52 * 1024 * 1024
HEAD_SHIFT = 6
HB_SHIFT = 7
assert LRU_HD == HG_HD == SSD_HD == MLA_V == 1 << HEAD_SHIFT and LANES == 1 << HB_SHIFT


def _cparams(sem):
    return pltpu.CompilerParams(dimension_semantics=sem, vmem_limit_bytes=VMEM_LIMIT)


def _sigmoid(x):
    return 1.0 / (1.0 + jnp.exp(-x))


def _silu(x):
    return x * _sigmoid(x)


def _softplus(x):
    return jnp.maximum(x, 0.0) + jnp.log(1.0 + jnp.exp(-jnp.abs(x)))


def _gelu_tanh(x):
    return 0.5 * x * (1.0 + jnp.tanh(math.sqrt(2.0 / math.pi) * (x + 0.044715 * (x * x * x))))


def _dot(a, b, **kw):
    return jnp.dot(a, b, preferred_element_type=F32, **kw)


def _dot_nt(a, b, **kw):
    return lax.dot_general(a, b, (((1,), (1,)), ((), ())), preferred_element_type=F32, **kw)


def _dot_tn(a, b, **kw):
    return lax.dot_general(a, b, (((0,), (0,)), ((), ())), preferred_element_type=F32, **kw)


def _split_bf16(x, n):
    parts, rest = [], x
    for _ in range(n):
        p = rest.astype(BF16)
        parts.append(p)
        rest = rest - p.astype(F32)
    return parts


ADA_ROWS = 24
ADA_TN = 512


def _ada_kernel(c_ref, w_ref, b_ref, o_ref):
    s = _silu(c_ref[...])
    o_ref[0] = _dot(s, w_ref[0], precision=HI) + b_ref[0]


def _ada_call(cc, ada_w, ada_b):
    return pl.pallas_call(
        _ada_kernel,
        grid=(DEPTH, N_MOD * D // ADA_TN),
        in_specs=[
            pl.BlockSpec((ADA_ROWS, D), lambda l, j: (0, 0)),
            pl.BlockSpec((1, D, ADA_TN), lambda l, j: (l, 0, j)),
            pl.BlockSpec((1, 1, ADA_TN), lambda l, j: (l, 0, j)),
        ],
        out_specs=pl.BlockSpec((1, ADA_ROWS, ADA_TN), lambda l, j: (l, 0, j)),
        out_shape=jax.ShapeDtypeStruct((DEPTH, ADA_ROWS, N_MOD * D), F32),
        compiler_params=_cparams(("parallel", "parallel")),
        name="ada",
    )(cc, ada_w, ada_b.reshape(DEPTH, 1, N_MOD * D))


def _norm_mod(x, nw, shift, scale):
    ms = jnp.mean(x * x, axis=-1, keepdims=True)
    return (x * lax.rsqrt(ms + EPS) * nw) * (1.0 + scale) + shift


def _in_kernel(split, *refs):
    if split:
        x_ref, c_ref, mod_ref, nw_ref, w_ref, ua_ref, ub_ref, us_ref, um_ref = refs
        x = jnp.where(pl.program_id(1) < LAT_TILES, x_ref[0], c_ref[0])
    else:
        x_ref, mod_ref, nw_ref, w_ref, ua_ref, ub_ref, us_ref, um_ref = refs
        x = x_ref[0]
    h = _norm_mod(x, nw_ref[...], mod_ref[0, 0:1, :], mod_ref[0, 1:2, :]).astype(BF16)
    for k, ref in enumerate((ua_ref, ub_ref, us_ref, um_ref)):
        ref[0] = _dot(h, w_ref[:, U_OFFS[k]:U_OFFS[k + 1]])


def _mod_index(b, t, nb):
    return (jnp.where(t >= LAT_TILES, nb, b), 0, 0)


def _token_specs(ctx, tile_of=lambda s: s):
    if ctx is None:
        return [pl.BlockSpec((1, ROW_TILE, D), lambda b, s: (b, tile_of(s), 0))]
    return [pl.BlockSpec((1, ROW_TILE, D), lambda b, s: (b, jnp.minimum(tile_of(s), LAT_TILES - 1), 0)),
            pl.BlockSpec((1, N_CTX, D), lambda b, s: (b, 0, 0))]


def _in_call(x, ctx, modt, nw, w_all):
    nb = x.shape[0]
    in_specs = _token_specs(ctx) + [
        pl.BlockSpec((1, N_MOD, D), lambda b, t: _mod_index(b, t, nb)),
        pl.BlockSpec((1, D), lambda b, t: (0, 0)),
        pl.BlockSpec((D, U_OFFS[-1]), lambda b, t: (0, 0)),
    ]
    widths = (UA_W, UB_W, US_W, UM_W)
    out_specs = [pl.BlockSpec((1, ROW_TILE, w), lambda b, t: (b, t, 0)) for w in widths]
    out_shape = [jax.ShapeDtypeStruct((nb, NT, w), F32) for w in widths]
    args = ((x,) if ctx is None else (x, ctx)) + (modt, nw, w_all)
    return pl.pallas_call(
        functools.partial(_in_kernel, ctx is not None),
        grid=(nb, N_TILES),
        in_specs=in_specs,
        out_specs=out_specs,
        out_shape=out_shape,
        compiler_params=_cparams(("parallel", "parallel")),
        name="in_proj",
    )(*args)


def _conv_chunk(u_ref, r0, rows, c0, c1, cw_ref, cb_ref, at_seg_start, at_seg_end):
    x = u_ref[0, pl.ds(r0, rows), c0:c1]
    rp = pl.multiple_of(jnp.maximum(r0 - SUBLANES, 0), SUBLANES)
    rn = pl.multiple_of(jnp.minimum(r0 + rows, NT - SUBLANES), SUBLANES)
    xp = u_ref[0, pl.ds(rp, SUBLANES), c0:c1] * jnp.where(at_seg_start, 0.0, 1.0)
    xn = u_ref[0, pl.ds(rn, SUBLANES), c0:c1] * jnp.where(at_seg_end, 0.0, 1.0)
    xe = jnp.concatenate([xp, x, xn], axis=0)
    tot = rows + 2 * SUBLANES
    lo, hi = SUBLANES, SUBLANES + rows
    xm2 = pltpu.roll(xe, 2, 0)[lo:hi]
    xm1 = pltpu.roll(xe, 1, 0)[lo:hi]
    xp1 = pltpu.roll(xe, tot - 1, 0)[lo:hi]
    return cw_ref[0:1, :] * xm2 + cw_ref[1:2, :] * xm1 + cw_ref[2:3, :] * x + cw_ref[3:4, :] * xp1 + cb_ref[...]


LRU_CH = 256
LRU_NCH = NT // LRU_CH
LRU_LAT_CH = N_LAT // LRU_CH


def _scan_chunk(a, b, row_in_tile, hprev, reverse):
    n = a.shape[0]
    for s in (1, 2, 4):
        if reverse:
            a_s = pltpu.roll(a, n - s, 0)
            b_s = pltpu.roll(b, n - s, 0)
            valid = row_in_tile < SUBLANES - s
        else:
            a_s = pltpu.roll(a, s, 0)
            b_s = pltpu.roll(b, s, 0)
            valid = row_in_tile >= s
        b = b + a * jnp.where(valid, b_s, 0.0)
        a = a * jnp.where(valid, a_s, 1.0)
    tiles = n // SUBLANES
    hs = [None] * tiles
    for j in (reversed(range(tiles)) if reverse else range(tiles)):
        lo = j * SUBLANES
        hj = b[lo:lo + SUBLANES] + a[lo:lo + SUBLANES] * hprev
        hprev = hj[0:1] if reverse else hj[SUBLANES - 1:SUBLANES]
        hs[j] = hj
    return jnp.concatenate(hs, axis=0), hprev


def _lru_kernel(u_ref, cw_ref, cb_ref, wri_ref, bri_ref, lam_ref, y_ref, ab_s, h_s):
    row_in_tile = lax.broadcasted_iota(jnp.int32, (LRU_CH, GW), 0) & (SUBLANES - 1)
    sp = _softplus(-lam_ref[...])

    def prepare(c, carry):
        r0 = pl.multiple_of(c * LRU_CH, LRU_CH)
        is_ctx = c == LRU_LAT_CH
        xc = _conv_chunk(u_ref, r0, LRU_CH, 0, GW, cw_ref, cb_ref, (c == 0) | is_ctx, (c == LRU_LAT_CH - 1) | is_ctx)
        g = _dot(xc.astype(BF16), wri_ref[...]) + bri_ref[...]
        for d in (0, 1):
            r = _sigmoid(g[:, 2 * d * GW:(2 * d + 1) * GW])
            ig = _sigmoid(g[:, (2 * d + 1) * GW:(2 * d + 2) * GW])
            a = jnp.exp(-LRU_C * r * sp[d:d + 1, :])
            ab_s[2 * d, pl.ds(r0, LRU_CH), :] = a
            ab_s[2 * d + 1, pl.ds(r0, LRU_CH), :] = jnp.sqrt(1.0 - a * a) * (ig * xc)
        return carry

    lax.fori_loop(0, LRU_NCH, prepare, 0)

    def body(i, carry):
        hf, hb = carry
        rf = pl.multiple_of(((i + LRU_LAT_CH) % LRU_NCH) * LRU_CH, LRU_CH)
        rb = pl.multiple_of((LRU_LAT_CH - i) * LRU_CH, LRU_CH)
        h, hf = _scan_chunk(ab_s[0, pl.ds(rf, LRU_CH), :], ab_s[1, pl.ds(rf, LRU_CH), :], row_in_tile, hf, False)
        h_s[0, pl.ds(rf, LRU_CH), :] = h
        h, hb = _scan_chunk(ab_s[2, pl.ds(rb, LRU_CH), :], ab_s[3, pl.ds(rb, LRU_CH), :], row_in_tile, hb, True)
        h_s[1, pl.ds(rb, LRU_CH), :] = h
        return hf, hb

    zero = jnp.zeros((1, GW), F32)
    lax.fori_loop(0, LRU_NCH, body, (zero, zero))

    def finish(c, carry):
        r0 = pl.multiple_of(c * LRU_CH, LRU_CH)
        gate = u_ref[0, pl.ds(r0, LRU_CH), GW:2 * GW]
        hsum = h_s[0, pl.ds(r0, LRU_CH), :] + h_s[1, pl.ds(r0, LRU_CH), :]
        y_ref[0, pl.ds(r0, LRU_CH), :] = (hsum * _gelu_tanh(gate)).astype(BF16)
        return carry

    lax.fori_loop(0, LRU_NCH, finish, 0)


def _lru_call(ua, cw, cb, wri, bri, lam):
    nb = ua.shape[0]
    full = lambda shape: pl.BlockSpec(shape, lambda b: (0,) * len(shape))
    return pl.pallas_call(
        _lru_kernel,
        grid=(nb,),
        in_specs=[
            pl.BlockSpec((1, NT, UA_W), lambda b: (b, 0, 0)),
            full((CONV_W, GW)), full((1, GW)), full((GW, 4 * GW)), full((1, 4 * GW)), full((2, GW)),
        ],
        out_specs=pl.BlockSpec((1, NT, GW), lambda b: (b, 0, 0)),
        out_shape=jax.ShapeDtypeStruct((nb, NT, GW), BF16),
        scratch_shapes=[pltpu.VMEM((4, NT, GW), F32), pltpu.VMEM((2, NT, GW), F32)],
        compiler_params=_cparams(("parallel",)),
        name="rglru",
    )(ua, cw, cb, wri, bri, lam)


HG_NCH = NT // HG_CH
HG_LAT_CH = N_LAT // HG_CH
HG_LEVELS = HG_CH.bit_length() - 1
HG_EXP_BLOCKS = 1 + HG_LEVELS
HG_SPLIT = 3


def _hgrn_tables():
    t_ = HG_CH
    dm = np.zeros((2, HG_EXP_BLOCKS, t_, t_), np.float32)
    lm = np.zeros((2, HG_LEVELS + 1, t_, t_), np.float32)
    for d in (0, 1):
        for t in range(t_):
            if d == 0:
                dm[d, 0, t, :t + 1] = 1.0
            else:
                dm[d, 0, t, t:] = 1.0
            lm[d, 0, t, t] = 1.0
        for lev in range(1, HG_LEVELS + 1):
            m = 1 << (lev - 1)
            for t in range(t_):
                start = (t // (2 * m)) * 2 * m
                mid = start + m
                upper = t >= mid
                if d == 0:
                    if upper:
                        dm[d, lev, t, mid:t + 1] = 1.0
                        lm[d, lev, t, start:mid] = 1.0
                    else:
                        dm[d, lev, t, t + 1:mid] = 1.0
                else:
                    if not upper:
                        dm[d, lev, t, t:mid] = 1.0
                        lm[d, lev, t, mid:start + 2 * m] = 1.0
                    else:
                        dm[d, lev, t, mid:t] = 1.0
    dm = dm.reshape(2, HG_EXP_BLOCKS * t_, t_)
    d3 = np.concatenate([dm] * HG_SPLIT, axis=-1)
    lm = np.tile(lm, (1, 1, 1, HG_HEADS))
    heads = np.arange(GW) // HG_HD
    bm = (heads[:, None] == heads[None, :]).astype(np.float32)
    return jnp.asarray(d3, BF16), jnp.asarray(lm, F32), jnp.asarray(bm, F32), jnp.asarray(bm, BF16)


def _hgrn_kernel(u_ref, lb_ref, nw_ref, d3_ref, lm_ref, bm_ref, bmb_ref, y_ref, o_s, st_s, stb_s):
    t = HG_CH
    lb = lb_ref[...]
    st_s[...] = jnp.zeros(st_s.shape, F32)
    stb_s[...] = jnp.zeros(stb_s.shape, BF16)

    def by_head(xb):
        return jnp.concatenate([xb] * HG_HEADS, axis=0) * bmb_ref[...]

    def chunk(d, c):
        r0 = pl.multiple_of(c * t, t)
        q = _silu(u_ref[0, pl.ds(r0, t), 0:GW])
        fr = u_ref[0, pl.ds(r0, t), (1 + d) * GW:(2 + d) * GW]
        v = u_ref[0, pl.ds(r0, t), 3 * GW:4 * GW]
        f = lb + (1.0 - lb) * _sigmoid(fr)
        k = 1.0 - f
        ex = _dot(d3_ref[d], jnp.concatenate(_split_bf16(jnp.log2(f), HG_SPLIT), axis=0))
        cum = ex[0:t]
        a = _dot_nt(q.astype(BF16), by_head(k.astype(BF16))) * lm_ref[d, 0]
        for lev in range(1, HG_LEVELS + 1):
            fac = jnp.exp2(ex[lev * t:(lev + 1) * t])
            a = a + _dot_nt((q * fac).astype(BF16), by_head((k * fac).astype(BF16))) * lm_ref[d, lev]
        vb = v.astype(BF16)
        o = _dot(a.astype(BF16), by_head(vb)) + _dot_nt((q * jnp.exp2(cum)).astype(BF16), stb_s[d])
        o_s[d, pl.ds(r0, t), :] = o
        cend = cum[t - 1:t, :] if d == 0 else cum[0:1, :]
        kend = (k * jnp.exp2(cend - cum)).astype(BF16)
        st = st_s[d] * jnp.exp2(cend) + _dot_tn(vb, kend) * bm_ref[...]
        st_s[d] = st
        stb_s[d] = st.astype(BF16)

    def body(i, carry):
        chunk(0, (i + HG_LAT_CH) % HG_NCH)
        chunk(1, HG_NCH - 1 - i)
        return carry

    lax.fori_loop(0, HG_NCH, body, 0, unroll=4)

    def finish(c, carry):
        r0 = pl.multiple_of(c * t, t)
        osum = o_s[0, pl.ds(r0, t), :] + o_s[1, pl.ds(r0, t), :]
        sq = _split_bf16(osum * osum, 2)
        ms = _dot(jnp.concatenate(sq, axis=1), jnp.concatenate([bmb_ref[...]] * 2, axis=0)) * (1.0 / HG_HD)
        g = u_ref[0, pl.ds(r0, t), 4 * GW:5 * GW]
        y_ref[0, pl.ds(r0, t), :] = (osum * lax.rsqrt(ms + EPS) * nw_ref[...] * _silu(g)).astype(BF16)
        return carry

    lax.fori_loop(0, HG_NCH, finish, 0, unroll=4)


def _hgrn_call(ub, lb, nw, tables):
    nb = ub.shape[0]
    d3, lm, bm, bmb = tables
    full = lambda shape: pl.BlockSpec(shape, lambda b: (0,) * len(shape))
    return pl.pallas_call(
        _hgrn_kernel,
        grid=(nb,),
        in_specs=[pl.BlockSpec((1, NT, UB_W), lambda b: (b, 0, 0)), full((1, GW)), full((1, GW)),
                  full(d3.shape), full(lm.shape), full(bm.shape), full(bmb.shape)],
        out_specs=pl.BlockSpec((1, NT, GW), lambda b: (b, 0, 0)),
        out_shape=jax.ShapeDtypeStruct((nb, NT, GW), BF16),
        scratch_shapes=[pltpu.VMEM((2, NT, GW), F32), pltpu.VMEM((2, GW, GW), F32), pltpu.VMEM((2, GW, GW), BF16)],
        compiler_params=_cparams(("parallel",)),
        name="hgrn2",
    )(ub, lb, nw, d3, lm, bm, bmb)


SSD_NCH = NT // SSD_CH
SSD_LAT_CH = N_LAT // SSD_CH
SSD_DT0 = GW + SSD_XBC
GRP_W = SSD_HEADS // SSD_GROUPS * SSD_HD


SSD_BC_W = 2 * SSD_GROUPS * SSD_STATE
SSD_CUM_SPLIT = 3
SSD_GAIN_SPLIT = 2


def _ssd_tables():
    t = SSD_CH
    tri = np.stack([np.tril(np.ones((t, t), np.float32)), np.triu(np.ones((t, t), np.float32))])
    col = np.zeros((2, LANES, SSD_HEADS * LANES), np.float32)
    wide = np.zeros((2, LANES, GW), np.float32)
    for d in range(2):
        for h in range(SSD_HEADS):
            col[d, d * SSD_HEADS + h, h * LANES:(h + 1) * LANES] = 1.0
            wide[d, d * SSD_HEADS + h, h * SSD_HD:(h + 1) * SSD_HD] = 1.0
    hm = (np.arange(GW)[None, :] // SSD_HD == np.arange(SSD_HEADS)[:, None]).astype(np.float32)
    gm = (np.arange(LANES)[None, :] // SSD_STATE == np.arange(SSD_GROUPS)[:, None]).astype(np.float32)
    return (jnp.asarray(np.concatenate([tri] * SSD_CUM_SPLIT, axis=2), BF16),
            jnp.asarray(np.concatenate([col] * SSD_CUM_SPLIT, axis=1), BF16),
            jnp.asarray(np.concatenate([wide] * SSD_GAIN_SPLIT, axis=1), BF16),
            jnp.asarray(hm[:, None, :], BF16), jnp.asarray(gm[:, None, :], BF16))


def _ssd_kernel(u_ref, cw_ref, cb_ref, a8_ref, b8_ref, dsk_ref, nw_ref, tri_ref, col_ref, wide_ref, hm_ref, gm_ref,
                y_ref, xs_s, bc_s, xm_s, dt_s, y_s, st_s):
    t = SSD_CH
    rr = lax.broadcasted_iota(jnp.int32, (t, t), 0)
    cc = lax.broadcasted_iota(jnp.int32, (t, t), 1)
    keeps = (rr >= cc, rr <= cc)
    a8 = -jnp.exp(a8_ref[...])

    def prepare(c, carry):
        r0 = pl.multiple_of(c * t, t)
        seg_start = (c == 0) | (c == SSD_LAT_CH)
        seg_end = (c == SSD_LAT_CH - 1) | (c == SSD_NCH - 1)
        xbc = _silu(_conv_chunk(u_ref, r0, t, GW, GW + SSD_XBC, cw_ref, cb_ref, seg_start, seg_end))
        xs = xbc[:, 0:GW]
        xs_s[pl.ds(r0, t), :] = xs
        bc_s[pl.ds(r0, t), :] = xbc[:, GW:GW + SSD_BC_W].astype(BF16)
        xb = xs.astype(BF16)
        for h in range(SSD_HEADS):
            xm_s[h, pl.ds(r0, t), :] = xb * hm_ref[h]
        dt_s[pl.ds(r0, t), :] = _softplus(u_ref[0, pl.ds(r0, t), SSD_DT0:SSD_DT0 + LANES] + b8_ref[...])
        return carry

    lax.fori_loop(0, SSD_NCH, prepare, 0)
    st_s[...] = jnp.zeros(st_s.shape, F32)

    def chunk(d, c):
        r0 = pl.multiple_of(c * t, t)
        dtn = dt_s[pl.ds(r0, t), :]
        cumn = _dot(tri_ref[d], jnp.concatenate(_split_bf16(dtn * a8, SSD_CUM_SPLIT), axis=0))
        cend = cumn[t - 1:t, :] if d == 0 else cumn[0:1, :]
        cum_t = cumn.T
        dt_t = dtn.T
        colb = _dot(jnp.concatenate(_split_bf16(cumn, SSD_CUM_SPLIT), axis=1), col_ref[d])
        gains = jnp.concatenate([jnp.exp(cumn), dtn * jnp.exp(cend - cumn)], axis=0)
        gw = _dot(jnp.concatenate(_split_bf16(gains, SSD_GAIN_SPLIT), axis=1), wide_ref[d])
        ecum, wend = gw[0:t], gw[t:2 * t]
        bc = bc_s[pl.ds(r0, t), :]
        bmat, cmat = bc[:, 0:LANES], bc[:, LANES:2 * LANES]
        st = st_s[d]
        y = jnp.zeros((t, GW), F32)
        y_in = []
        for g in range(SSD_GROUPS):
            cg = cmat * gm_ref[g]
            scores = _dot_nt(cg, bmat)
            sg = st[:, g * GRP_W:(g + 1) * GRP_W].astype(BF16)
            y_in.append(_dot(cg, jnp.concatenate([sg] * SSD_GROUPS, axis=0)))
            for hh in range(SSD_HEADS // SSD_GROUPS):
                h = g * (SSD_HEADS // SSD_GROUPS) + hh
                row = d * SSD_HEADS + h
                seg = colb[:, h * LANES:(h + 1) * LANES] - cum_t[row:row + 1, :]
                m = jnp.where(keeps[d], jnp.exp(seg), 0.0) * scores * dt_t[row:row + 1, :]
                y = y + _dot(m.astype(BF16), xm_s[h, pl.ds(r0, t), :])
        y_s[d, pl.ds(r0, t), :] = y + jnp.concatenate(y_in, axis=1) * ecum
        xw = (xs_s[pl.ds(r0, t), :] * wend).astype(BF16)
        upd = [_dot_tn(bmat, xw[:, g * GRP_W:(g + 1) * GRP_W])[g * SSD_STATE:(g + 1) * SSD_STATE]
               for g in range(SSD_GROUPS)]
        eend = ecum[t - 1:t, :] if d == 0 else ecum[0:1, :]
        st_s[d] = st * eend + jnp.concatenate(upd, axis=1)

    def body(i, carry):
        chunk(0, (i + SSD_LAT_CH) % SSD_NCH)
        chunk(1, SSD_NCH - 1 - i)
        return carry

    lax.fori_loop(0, SSD_NCH, body, 0, unroll=3)

    def finish(c, carry):
        r0 = pl.multiple_of(c * t, t)
        z = u_ref[0, pl.ds(r0, t), 0:GW]
        yy = (y_s[0, pl.ds(r0, t), :] + y_s[1, pl.ds(r0, t), :] + dsk_ref[...] * xs_s[pl.ds(r0, t), :]) * _silu(z)
        ms = jnp.mean(yy * yy, axis=-1, keepdims=True)
        y_ref[0, pl.ds(r0, t), :] = (yy * lax.rsqrt(ms + EPS) * nw_ref[...]).astype(BF16)
        return carry

    lax.fori_loop(0, SSD_NCH, finish, 0, unroll=2)


def _ssd_call(us, cw, cb, a8, b8, dsk, nw, tables):
    nb = us.shape[0]
    full = lambda shape: pl.BlockSpec(shape, lambda b: (0,) * len(shape))
    return pl.pallas_call(
        _ssd_kernel,
        grid=(nb,),
        in_specs=[
            pl.BlockSpec((1, NT, US_W), lambda b: (b, 0, 0)),
            full((CONV_W, SSD_XBC)), full((1, SSD_XBC)), full((1, LANES)), full((1, LANES)),
            full((1, GW)), full((1, GW)),
        ] + [full(tb.shape) for tb in tables],
        out_specs=pl.BlockSpec((1, NT, GW), lambda b: (b, 0, 0)),
        out_shape=jax.ShapeDtypeStruct((nb, NT, GW), BF16),
        scratch_shapes=[pltpu.VMEM((NT, GW), F32), pltpu.VMEM((NT, SSD_BC_W), BF16),
                        pltpu.VMEM((SSD_HEADS, NT, GW), BF16), pltpu.VMEM((NT, LANES), F32),
                        pltpu.VMEM((2, NT, GW), F32), pltpu.VMEM((2, SSD_STATE, GW), F32)],
        compiler_params=_cparams(("parallel",)),
        name="ssd",
    )(us, cw, cb, a8, b8, dsk, nw, *tables)


HB = LANES
MLA_CQ_W = 2 * LANES
MLA_KR0 = MLA_CQ_W + MLA_KVR
assert MLA_QR <= MLA_CQ_W and MLA_KR0 + HB == UM_W
QW = MLA_HEADS * HB
MLA_QB = 256
MLA_AB = 256
MLA_SCALE = (MLA_NOPE + MLA_ROPE) ** -0.5


def _mla_kernel(need_ctx, u_ref, qa_ref, wq_ref, qn_ref, kva_ref, wkv_ref, kn_ref, krw_ref,
                cos_ref, sa_ref, sb_ref, y_ref, q_s, k_s, v_s):
    lane = lax.broadcasted_iota(jnp.int32, (QW, QW), 1)
    row = lax.broadcasted_iota(jnp.int32, (QW, QW), 0)

    def grp(i):
        within = i & (HB - 1)
        return (i >> HB_SHIFT) * 3 + jnp.where(within < MLA_NOPE, 0, jnp.where(within < MLA_NOPE + MLA_ROPE, 1, 2))

    gmat = jnp.where(grp(row) == grp(lane), 1.0, 0.0).astype(BF16)
    l1 = lax.broadcasted_iota(jnp.int32, (1, QW), 1) & (HB - 1)
    inv_size = jnp.where(l1 < MLA_NOPE, 1.0 / MLA_NOPE, 1.0 / MLA_ROPE)
    vlane_head = lax.broadcasted_iota(jnp.int32, (MLA_QB, GW), 1) >> HEAD_SHIFT

    def rope(x, r0, reps):
        cosr = cos_ref[pl.ds(r0, MLA_QB), :]
        sar = sa_ref[pl.ds(r0, MLA_QB), :]
        sbr = sb_ref[pl.ds(r0, MLA_QB), :]
        if reps > 1:
            cosr, sar, sbr = (jnp.concatenate([tbl] * reps, axis=1) for tbl in (cosr, sar, sbr))
        w = x.shape[1]
        return x * cosr + pltpu.roll(x, w - MLA_ROPE // 2, 1) * sar + pltpu.roll(x, MLA_ROPE // 2, 1) * sbr

    def project(ci, carry):
        r0 = pl.multiple_of(ci * MLA_QB, MLA_QB)
        cq = u_ref[0, pl.ds(r0, MLA_QB), 0:MLA_CQ_W]
        ms = jnp.sum(cq * cq, axis=-1, keepdims=True) * (1.0 / MLA_QR)
        qraw = _dot((cq * lax.rsqrt(ms + EPS) * qa_ref[...]).astype(BF16), wq_ref[...])
        ss = _dot((qraw * qraw).astype(BF16), gmat) * inv_size
        q = rope(qraw * lax.rsqrt(ss + EPS) * qn_ref[...], r0, MLA_HEADS) * MLA_SCALE
        q_s[pl.ds(r0, MLA_QB), :] = q.astype(BF16)
        ckv = u_ref[0, pl.ds(r0, MLA_QB), MLA_CQ_W:MLA_KR0]
        ms = jnp.mean(ckv * ckv, axis=-1, keepdims=True)
        kv = _dot((ckv * lax.rsqrt(ms + EPS) * kva_ref[...]).astype(BF16), wkv_ref[...])
        kraw = kv[:, 0:QW]
        ss = _dot((kraw * kraw).astype(BF16), gmat) * inv_size
        knope = kraw * lax.rsqrt(ss + EPS) * kn_ref[...]
        kr = u_ref[0, pl.ds(r0, MLA_QB), MLA_KR0:MLA_KR0 + HB]
        ms = jnp.sum(kr * kr, axis=-1, keepdims=True) * (1.0 / MLA_ROPE)
        krope = rope(kr * lax.rsqrt(ms + EPS) * krw_ref[...], r0, 1)
        k_s[pl.ds(r0, MLA_QB), :] = (knope + jnp.concatenate([krope] * MLA_HEADS, axis=1)).astype(BF16)
        vv = kv[:, QW:QW + GW]
        for h in range(MLA_HEADS):
            v_s[h, pl.ds(r0, MLA_QB), :] = jnp.where(vlane_head == h, vv, 0.0).astype(BF16)
        return carry

    lax.fori_loop(0, NT // MLA_QB, project, 0, unroll=3)

    def attend(r0, rows, k0, klen):
        o = jnp.zeros((rows, GW), F32)
        for h in range(MLA_HEADS):
            qh = q_s[pl.ds(r0, rows), h * HB:(h + 1) * HB]
            s = _dot_nt(qh, k_s[k0:k0 + klen, h * HB:(h + 1) * HB])
            p = jnp.exp(s - jnp.max(s, axis=-1, keepdims=True))
            l = jnp.sum(p, axis=-1, keepdims=True)
            o = o + _dot(p.astype(BF16), v_s[h, k0:k0 + klen, :]) * (1.0 / l)
        y_ref[0, pl.ds(r0, rows), :] = o.astype(BF16)

    def lat_block(qi, carry):
        attend(pl.multiple_of(qi * MLA_AB, MLA_AB), MLA_AB, 0, NT)
        return carry

    lax.fori_loop(0, N_LAT // MLA_AB, lat_block, 0, unroll=4)
    if need_ctx:
        attend(N_LAT, N_CTX, N_LAT, N_CTX)
    else:
        y_ref[0, N_LAT:NT, :] = jnp.zeros((N_CTX, GW), BF16)


def _mla_call(um, need_ctx, qa, wq, qn, kva, wkv, kn, krw, cos_t, sa_t, sb_t):
    nb = um.shape[0]
    full = lambda shape: pl.BlockSpec(shape, lambda b: (0,) * len(shape))
    return pl.pallas_call(
        functools.partial(_mla_kernel, need_ctx),
        grid=(nb,),
        in_specs=[
            pl.BlockSpec((1, NT, UM_W), lambda b: (b, 0, 0)),
            full((1, MLA_CQ_W)), full((MLA_CQ_W, QW)), full((1, QW)),
            full((1, MLA_KVR)), full((MLA_KVR, QW + GW)), full((1, QW)), full((1, HB)),
            full((NT, HB)), full((NT, HB)), full((NT, HB)),
        ],
        out_specs=pl.BlockSpec((1, NT, GW), lambda b: (b, 0, 0)),
        out_shape=jax.ShapeDtypeStruct((nb, NT, GW), BF16),
        scratch_shapes=[pltpu.VMEM((NT, QW), BF16), pltpu.VMEM((NT, QW), BF16),
                        pltpu.VMEM((MLA_HEADS, NT, GW), BF16)],
        compiler_params=_cparams(("parallel",)),
        name="mla",
    )(um, qa, wq, qn, kva, wkv, kn, krw, cos_t, sa_t, sb_t)


def _out_tile(s):
    return jnp.maximum(s - 1, 0)


def _out_kernel(split, *refs):
    if split:
        x_ref, c_ref, ya_ref, yb_ref, ys_ref, ym_ref, wo_ref, mod_ref, nw_ref, wr_ref = refs[:10]
    else:
        x_ref, ya_ref, yb_ref, ys_ref, ym_ref, wo_ref, mod_ref, nw_ref, wr_ref = refs[:9]
    xo_ref, h2_ref, aff_ref, acc_s = refs[-4:]
    s = pl.program_id(1)

    @pl.when(s == 0)
    def _first():
        acc_s[1] = jnp.zeros(acc_s.shape[1:], F32)

    def step(slot):
        y = jnp.concatenate([ya_ref[0], yb_ref[0], ys_ref[0], ym_ref[0]], axis=1)
        acc_s[slot] = _dot(y, wo_ref[...])
        x_in = jnp.where(_out_tile(s) < LAT_TILES, x_ref[0], c_ref[0]) if split else x_ref[0]
        x = x_in + mod_ref[0, 2:3, :] * acc_s[1 - slot]
        xo_ref[0] = x
        h2 = _norm_mod(x, nw_ref[...], mod_ref[0, 3:4, :], mod_ref[0, 4:5, :])
        hi = h2.astype(BF16)
        h2_ref[0] = hi
        lo = (h2 - hi.astype(F32)).astype(BF16)
        wr = wr_ref[...]
        w_hi = wr.astype(BF16)
        w_lo = (wr - w_hi.astype(F32)).astype(BF16)
        logit = _dot_nt(jnp.concatenate([w_hi, w_hi, w_lo], axis=1), jnp.concatenate([hi, lo, hi], axis=1))
        e = jnp.exp(logit - jnp.max(logit, axis=0, keepdims=True))
        aff_ref[0] = e / jnp.sum(e, axis=0, keepdims=True)

    for parity in (0, 1):
        pl.when(s % 2 == parity)(functools.partial(step, parity))


def _out_call(x, ctx, ys, wo, modt, nw, wr_t, n_tiles):
    nb = x.shape[0]
    proj = lambda w: pl.BlockSpec((1, ROW_TILE, w), lambda b, s: (b, jnp.minimum(s, n_tiles - 1), 0))
    done = lambda w: pl.BlockSpec((1, ROW_TILE, w), lambda b, s: (b, _out_tile(s), 0))
    x_args = (x,) if ctx is None else (x, ctx)
    return pl.pallas_call(
        functools.partial(_out_kernel, ctx is not None),
        grid=(nb, n_tiles + 1),
        in_specs=_token_specs(ctx, _out_tile) + [proj(GW), proj(GW), proj(GW), proj(GW),
                  pl.BlockSpec((D, D), lambda b, s: (0, 0)),
                  pl.BlockSpec((1, N_MOD, D), lambda b, s: _mod_index(b, _out_tile(s), nb)),
                  pl.BlockSpec((1, D), lambda b, s: (0, 0)),
                  pl.BlockSpec((N_EXP, D), lambda b, s: (0, 0))],
        out_specs=[done(D), done(D), pl.BlockSpec((1, N_EXP, ROW_TILE), lambda b, s: (b, 0, _out_tile(s)))],
        out_shape=[jax.ShapeDtypeStruct((nb, n_tiles * ROW_TILE, D), F32),
                   jax.ShapeDtypeStruct((nb, n_tiles * ROW_TILE, D), BF16),
                   jax.ShapeDtypeStruct((nb, N_EXP, n_tiles * ROW_TILE), F32)],
        scratch_shapes=[pltpu.VMEM((2, ROW_TILE, D), F32)],
        compiler_params=_cparams(("parallel", "arbitrary")),
        name="out_proj",
    )(*x_args, *ys, wo, modt, nw, wr_t)


PRE_W = 256
ROUTE_MAX_IT = 160


def _prefix_count(m, tri_bf):
    n = m.shape[1]
    off = jnp.zeros((m.shape[0], 1), F32)
    outs = []
    for j in range(n // PRE_W):
        blk = m[:, j * PRE_W:(j + 1) * PRE_W]
        outs.append(_dot(blk.astype(BF16), tri_bf) + off)
        off = off + jnp.sum(blk, axis=1, keepdims=True)
    return outs[0] if len(outs) == 1 else jnp.concatenate(outs, axis=1)


def _route(aff, cap, tri):
    n_lo0 = jnp.full((N_EXP, 1), float(aff.shape[1]), F32)

    def cond(s):
        it, _, _, n_lo, n_hi = s
        return jnp.logical_and(it < ROUTE_MAX_IT, jnp.max(n_lo - n_hi) > 1.0)

    def step(s):
        it, lo, hi, n_lo, n_hi = s
        width = hi - lo
        mids = [lo + frac * width for frac in (0.25, 0.5, 0.75)]
        cnts = [jnp.sum(jnp.where(aff > m, 1.0, 0.0), axis=1, keepdims=True) for m in mids]
        blws = [jnp.max(jnp.where(aff <= m, aff, -1.0), axis=1, keepdims=True) for m in mids]
        u0, u1, u2 = (c >= cap for c in cnts)
        pick = lambda a3, a2, a1, a0: jnp.where(u2, a3, jnp.where(u1, a2, jnp.where(u0, a1, a0)))
        return (it + 1, pick(mids[2], mids[1], mids[0], lo), pick(hi, blws[2], blws[1], blws[0]),
                pick(cnts[2], cnts[1], cnts[0], n_lo), pick(n_hi, cnts[2], cnts[1], cnts[0]))

    init = (jnp.int32(0), jnp.full((N_EXP, 1), -1.0, F32), jnp.max(aff, axis=1, keepdims=True),
            n_lo0, jnp.zeros((N_EXP, 1), F32))
    _, _, thr, _, n_gt = lax.while_loop(cond, step, init)
    gt = jnp.where(aff > thr, 1.0, 0.0)
    eq = jnp.where(aff == thr, 1.0, 0.0)
    sel = gt + eq * jnp.where(_prefix_count(eq, tri) < cap - n_gt, 1.0, 0.0)
    return jnp.where(sel > 0.0, _prefix_count(sel, tri), -1.0)


MOE_EPS = 2


def _moe_kernel(segs, x_hbm, aff_ref, h_ref, gl_ref, gc_ref, wg_ref, wu_ref, wd_ref, o_ref, pos_ref, sem):
    b = pl.program_id(0)
    step = pl.program_id(1)
    rows = o_ref.shape[1]

    def residual_copy():
        return pltpu.make_async_copy(x_hbm.at[b, pl.ds(0, rows), :], o_ref.at[0], sem)

    @pl.when(step == 0)
    def _init():
        residual_copy().start()
        tri = jnp.where(lax.broadcasted_iota(jnp.int32, (PRE_W, PRE_W), 0)
                        < lax.broadcasted_iota(jnp.int32, (PRE_W, PRE_W), 1), 1.0, 0.0).astype(BF16)
        for r0, n, cap in segs:
            pos_ref[:, r0:r0 + n] = _route(aff_ref[0, :, r0:r0 + n], cap, tri)
        residual_copy().wait()

    onehots, gates, xs = [], [], []
    for (r0, n, cap), g_ref in zip(segs, (gl_ref, gc_ref)):
        slot = lax.broadcasted_iota(jnp.int32, (cap, n), 0).astype(F32)
        oh, gt = [], []
        for j in range(MOE_EPS):
            e = step * MOE_EPS + j
            hit = slot == pos_ref[pl.ds(e, 1), r0:r0 + n]
            oh.append(jnp.where(hit, 1.0, 0.0).astype(BF16))
            gsel = jnp.sum(jnp.where(hit, aff_ref[0, pl.ds(e, 1), r0:r0 + n], 0.0), axis=1, keepdims=True)
            gt.append(gsel * g_ref[0, N_MOD - 1:N_MOD, :])
        onehot = jnp.concatenate(oh, axis=0)
        onehots.append(onehot)
        gates.append(gt)
        xs.append(_dot(onehot, h_ref[0, r0:r0 + n, :]).astype(BF16))
    ys = []
    for j in range(MOE_EPS):
        xj = [x[j * cap:(j + 1) * cap] for x, (_, _, cap) in zip(xs, segs)]
        xj = xj[0] if len(xj) == 1 else jnp.concatenate(xj, axis=0)
        act = (_silu(_dot(xj, wg_ref[j])) * _dot(xj, wu_ref[j])).astype(BF16)
        ys.append(_dot(act, wd_ref[j]))
    s0 = 0
    for (r0, n, cap), onehot, gt in zip(segs, onehots, gates):
        ysg = jnp.concatenate([(ys[j][s0:s0 + cap] * gt[j]).astype(BF16) for j in range(MOE_EPS)], axis=0)
        o_ref[0, r0:r0 + n, :] += _dot_tn(onehot, ysg)
        s0 += cap


def _moe_call(x_mid, aff, h2, modt, wg, wu, wd, with_ctx):
    nb = h2.shape[0]
    segs = ((0, N_LAT, EC_FACTOR * N_LAT // N_EXP),)
    if with_ctx:
        segs += ((N_LAT, N_CTX, EC_FACTOR * N_CTX // N_EXP),)
    rows = NT if with_ctx else N_LAT
    return pl.pallas_call(
        functools.partial(_moe_kernel, segs),
        grid=(nb, N_EXP // MOE_EPS),
        in_specs=[
            pl.BlockSpec(memory_space=pl.ANY),
            pl.BlockSpec((1, N_EXP, rows), lambda b, e: (b, 0, 0)),
            pl.BlockSpec((1, rows, D), lambda b, e: (b, 0, 0)),
            pl.BlockSpec((1, N_MOD, D), lambda b, e: (b, 0, 0)),
            pl.BlockSpec((1, N_MOD, D), lambda b, e: (nb, 0, 0)),
            pl.BlockSpec((MOE_EPS, D, FF), lambda b, e: (e, 0, 0)),
            pl.BlockSpec((MOE_EPS, D, FF), lambda b, e: (e, 0, 0)),
            pl.BlockSpec((MOE_EPS, FF, D), lambda b, e: (e, 0, 0)),
        ],
        out_specs=pl.BlockSpec((1, rows, D), lambda b, e: (b, 0, 0)),
        out_shape=jax.ShapeDtypeStruct((nb, rows, D), F32),
        scratch_shapes=[pltpu.VMEM((N_EXP, rows), F32), pltpu.SemaphoreType.DMA(())],
        compiler_params=_cparams(("parallel", "arbitrary")),
        name="moe",
    )(x_mid, aff, h2, modt, modt, wg, wu, wd)


def _pad_cols(w, width):
    return jnp.pad(w, ((0, 0), (0, width - w.shape[1])))


def _rope_perm():
    half = MLA_ROPE // 2
    return jnp.concatenate([jnp.arange(half) * 2, jnp.arange(half) * 2 + 1])


def _head_block(nope, rope):
    pad = jnp.zeros(nope.shape[:-1] + (HB - MLA_NOPE - MLA_ROPE,), nope.dtype)
    blk = jnp.concatenate([nope, rope, pad], axis=-1)
    return blk.reshape(blk.shape[:-2] + (MLA_HEADS * HB,))


def _prep_in_weights(w_in):
    o1, o2, o3 = LRU_COLS, LRU_COLS + HG_COLS, LRU_COLS + HG_COLS + SSD_COLS
    wa, wb, ws, wm = w_in[:, :o1], w_in[:, o1:o2], w_in[:, o2:o3], w_in[:, o3:]
    perm = _rope_perm()
    cq, ckv, kr = wm[:, :MLA_QR], wm[:, MLA_QR:MLA_QR + MLA_KVR], wm[:, MLA_QR + MLA_KVR:]
    zeros = lambda n: jnp.zeros((D, n), w_in.dtype)
    wm_p = jnp.concatenate([cq, zeros(MLA_CQ_W - MLA_QR), ckv, zeros(MLA_NOPE), kr[:, perm],
                            zeros(HB - MLA_NOPE - MLA_ROPE)], axis=1)
    return jnp.concatenate([wa, wb, _pad_cols(ws, US_W), wm_p], axis=1).astype(BF16)


def _block_diag(w):
    h, dd, _ = w.shape
    eye = jnp.eye(h, dtype=w.dtype)
    return (eye[:, None, :, None] * w[:, :, None, :]).reshape(h * dd, h * dd)


def _rope_tables():
    rows = N_LAT // GRID_W
    row = jnp.repeat(jnp.arange(rows, dtype=F32), GRID_W)
    col = jnp.tile(jnp.arange(GRID_W, dtype=F32), rows)
    half = MLA_ROPE // 2
    inv = ROPE_BASE ** (-jnp.arange(0, half, 2, dtype=F32) / half)
    ang = jnp.concatenate([row[:, None] * inv, col[:, None] * inv], axis=-1)
    cos, sin = jnp.cos(ang), jnp.sin(ang)
    z = lambda n: jnp.zeros((N_LAT, n), F32)
    o = lambda n: jnp.ones((N_LAT, n), F32)
    cos_t = jnp.concatenate([o(MLA_NOPE), cos, cos, o(HB - MLA_NOPE - MLA_ROPE)], axis=1)
    sa_t = jnp.concatenate([z(MLA_NOPE), -sin, z(half), z(HB - MLA_NOPE - MLA_ROPE)], axis=1)
    sb_t = jnp.concatenate([z(MLA_NOPE), z(half), sin, z(HB - MLA_NOPE - MLA_ROPE)], axis=1)
    cos_t = jnp.concatenate([cos_t, jnp.ones((N_CTX, HB), F32)], axis=0)
    sa_t = jnp.concatenate([sa_t, jnp.zeros((N_CTX, HB), F32)], axis=0)
    sb_t = jnp.concatenate([sb_t, jnp.zeros((N_CTX, HB), F32)], axis=0)
    return cos_t, sa_t, sb_t


def kernel(x, c, ctx, c_ctx, ada_w, ada_b, norm1_w, norm2_w, w_in, w_out, lru_conv_w, lru_conv_b, lru_w_r, lru_b_r, lru_w_i, lru_b_i, lru_lam, hgrn_lb_logits, hgrn_norm_w, ssd_conv_w, ssd_conv_b, ssd_a_log, ssd_dt_bias, ssd_d_skip, ssd_norm_w, mla_q_a_norm, mla_w_q_up, mla_kv_a_norm, mla_w_kv_up, mla_q_norm, mla_k_norm, moe_router, moe_w_gate, moe_w_up, moe_w_down):
    nb = x.shape[0]
    assert x.shape == (nb, N_LAT, D) and ctx.shape == (nb, N_CTX, D)
    assert nb + 1 <= ADA_ROWS
    cc = jnp.concatenate([c, c_ctx[None, :], jnp.zeros((ADA_ROWS - nb - 1, D), F32)], axis=0)
    mod_all = _ada_call(cc, ada_w, ada_b)[:, :nb + 1].reshape(DEPTH, nb + 1, N_MOD, D)
    cos_t, sa_t, sb_t = _rope_tables()
    hg_tables = _hgrn_tables()
    ssd_tables = _ssd_tables()
    perm = _rope_perm()
    lb_w = jax.nn.softmax(hgrn_lb_logits.astype(F32), axis=0)
    lb_all = jnp.cumsum(lb_w, axis=0) - lb_w[0]
    rep = lambda v, n: jnp.repeat(v, n, axis=-1)
    stream = (x, ctx)
    for l in range(DEPTH):
        need_ctx = l < DEPTH - 1
        modt = mod_all[l]
        ua, ub, us, um = _in_call(*stream, modt, norm1_w[l][None, :], _prep_in_weights(w_in[l]))
        wri = jnp.concatenate([_block_diag(lru_w_r[l, 0]), _block_diag(lru_w_i[l, 0]),
                               _block_diag(lru_w_r[l, 1]), _block_diag(lru_w_i[l, 1])], axis=1).astype(BF16)
        bri = jnp.concatenate([lru_b_r[l, 0], lru_b_i[l, 0], lru_b_r[l, 1], lru_b_i[l, 1]])[None, :]
        ya = _lru_call(ua, lru_conv_w[l], lru_conv_b[l][None, :], wri, bri, lru_lam[l])
        yb = _hgrn_call(ub, lb_all[l][None, :], hgrn_norm_w[l][None, :], hg_tables)
        narrow = lambda v: jnp.pad(v.reshape(-1), (0, LANES - 2 * SSD_HEADS))[None, :]
        ysd = _ssd_call(us, ssd_conv_w[l], ssd_conv_b[l][None, :], narrow(ssd_a_log[l]), narrow(ssd_dt_bias[l]),
                        rep(ssd_d_skip[l], SSD_HD)[None, :], ssd_norm_w[l][None, :], ssd_tables)
        wq = mla_w_q_up[l].reshape(MLA_QR, MLA_HEADS, MLA_NOPE + MLA_ROPE)
        wq = _head_block(wq[..., :MLA_NOPE], wq[..., MLA_NOPE:][..., perm])
        wq = jnp.pad(wq, ((0, MLA_CQ_W - MLA_QR), (0, 0))).astype(BF16)
        qn = _head_block(jnp.broadcast_to(mla_q_norm[l][:MLA_NOPE], (MLA_HEADS, MLA_NOPE)),
                         jnp.broadcast_to(mla_q_norm[l][MLA_NOPE:][perm], (MLA_HEADS, MLA_ROPE)))[None, :]
        wkv = mla_w_kv_up[l].reshape(MLA_KVR, MLA_HEADS, MLA_NOPE + MLA_V)
        wk = _head_block(wkv[..., :MLA_NOPE], jnp.zeros((MLA_KVR, MLA_HEADS, MLA_ROPE), F32))
        wv = wkv[..., MLA_NOPE:].reshape(MLA_KVR, GW)
        wkv_p = jnp.concatenate([wk, wv], axis=1).astype(BF16)
        kn = _head_block(jnp.broadcast_to(mla_k_norm[l][:MLA_NOPE], (MLA_HEADS, MLA_NOPE)),
                         jnp.zeros((MLA_HEADS, MLA_ROPE), F32))[None, :]
        krw = jnp.concatenate([jnp.zeros((MLA_NOPE,), F32), mla_k_norm[l][MLA_NOPE:][perm],
                               jnp.zeros((HB - MLA_NOPE - MLA_ROPE,), F32)])[None, :]
        qa = jnp.pad(mla_q_a_norm[l], (0, MLA_CQ_W - MLA_QR))[None, :]
        ym = _mla_call(um, need_ctx, qa, wq, qn, mla_kv_a_norm[l][None, :], wkv_p, kn, krw, cos_t, sa_t, sb_t)
        n_tiles = N_TILES if need_ctx else LAT_TILES
        x_mid, h2, aff = _out_call(*stream, (ya, yb, ysd, ym), w_out[l].astype(BF16), modt, norm2_w[l][None, :],
                                   moe_router[l].T, n_tiles)
        wg, wu, wd = moe_w_gate[l].astype(BF16), moe_w_up[l].astype(BF16), moe_w_down[l].astype(BF16)
        stream = (_moe_call(x_mid, aff, h2, modt, wg, wu, wd, need_ctx), None)
    return stream[0]
```

```python
import functools
import math

import jax
import jax.numpy as jnp
import numpy as np
from jax import lax
from jax.experimental import pallas as pl
from jax.experimental.pallas import tpu as pltpu

F32 = jnp.float32
BF16 = jnp.bfloat16
HI = lax.Precision.HIGHEST

D = 1024
DEPTH = 2
N_LAT = 2048
N_CTX = 256
NT = N_LAT + N_CTX
GRID_W = 64
GW = 256
CONV_W = 4
LRU_HEADS, LRU_HD, LRU_C = 4, 64, 8.0
HG_HEADS, HG_HD = 4, 64
SSD_HEADS, SSD_HD, SSD_GROUPS, SSD_STATE = 4, 64, 2, 64
SSD_XBC = GW + 2 * SSD_GROUPS * SSD_STATE
MLA_HEADS, MLA_QR, MLA_KVR, MLA_NOPE, MLA_ROPE = 4, 192, 128, 64, 32
MLA_V = GW // MLA_HEADS
ROPE_BASE = 10000.0
N_EXP, FF, EC_FACTOR = 16, 512, 2
N_MOD = 6
EPS = 1e-6
LRU_COLS = 2 * GW
HG_COLS = 5 * GW
SSD_COLS = GW + SSD_XBC + 2 * SSD_HEADS

LANES = 128
SUBLANES = 8
ROW_TILE = 256
N_TILES = NT // ROW_TILE
LAT_TILES = N_LAT // ROW_TILE
UA_W, UB_W, US_W, UM_W = 512, 1280, 896, 512
U_OFFS = (0, UA_W, UA_W + UB_W, UA_W + UB_W + US_W, UA_W + UB_W + US_W + UM_W)
SSD_CH = 128
HG_CH = 64
VMEM_LIMIT = 52 * 1024 * 1024
HEAD_SHIFT = 6
HB_SHIFT = 7
assert LRU_HD == HG_HD == SSD_HD == MLA_V == 1 << HEAD_SHIFT and LANES == 1 << HB_SHIFT


def _cparams(sem):
    return pltpu.CompilerParams(dimension_semantics=sem, vmem_limit_bytes=VMEM_LIMIT)


def _sigmoid(x):
    return 1.0 / (1.0 + jnp.exp(-x))


def _silu(x):
    return x * _sigmoid(x)


def _softplus(x):
    return jnp.maximum(x, 0.0) + jnp.log(1.0 + jnp.exp(-jnp.abs(x)))


def _gelu_tanh(x):
    return 0.5 * x * (1.0 + jnp.tanh(math.sqrt(2.0 / math.pi) * (x + 0.044715 * (x * x * x))))


def _dot(a, b, **kw):
    return jnp.dot(a, b, preferred_element_type=F32, **kw)


def _dot_nt(a, b, **kw):
    return lax.dot_general(a, b, (((1,), (1,)), ((), ())), preferred_element_type=F32, **kw)


def _dot_tn(a, b, **kw):
    return lax.dot_general(a, b, (((0,), (0,)), ((), ())), preferred_element_type=F32, **kw)


def _split_bf16(x, n):
    parts, rest = [], x
    for _ in range(n):
        p = rest.astype(BF16)
        parts.append(p)
        rest = rest - p.astype(F32)
    return parts


ADA_ROWS = 24
ADA_TN = 512


def _ada_kernel(c_ref, w_ref, b_ref, o_ref):
    s = _silu(c_ref[...])
    o_ref[0] = _dot(s, w_ref[0], precision=HI) + b_ref[0]


def _ada_call(cc, ada_w, ada_b):
    return pl.pallas_call(
        _ada_kernel,
        grid=(DEPTH, N_MOD * D // ADA_TN),
        in_specs=[
            pl.BlockSpec((ADA_ROWS, D), lambda l, j: (0, 0)),
            pl.BlockSpec((1, D, ADA_TN), lambda l, j: (l, 0, j)),
            pl.BlockSpec((1, 1, ADA_TN), lambda l, j: (l, 0, j)),
        ],
        out_specs=pl.BlockSpec((1, ADA_ROWS, ADA_TN), lambda l, j: (l, 0, j)),
        out_shape=jax.ShapeDtypeStruct((DEPTH, ADA_ROWS, N_MOD * D), F32),
        compiler_params=_cparams(("parallel", "parallel")),
        name="ada",
    )(cc, ada_w, ada_b.reshape(DEPTH, 1, N_MOD * D))


def _norm_mod(x, nw, shift, scale):
    ms = jnp.mean(x * x, axis=-1, keepdims=True)
    return (x * lax.rsqrt(ms + EPS) * nw) * (1.0 + scale) + shift


def _in_kernel(split, *refs):
    if split:
        x_ref, c_ref, mod_ref, nw_ref, w_ref, ua_ref, ub_ref, us_ref, um_ref = refs
        x = jnp.where(pl.program_id(1) < LAT_TILES, x_ref[0], c_ref[0])
    else:
        x_ref, mod_ref, nw_ref, w_ref, ua_ref, ub_ref, us_ref, um_ref = refs
        x = x_ref[0]
    h = _norm_mod(x, nw_ref[...], mod_ref[0, 0:1, :], mod_ref[0, 1:2, :]).astype(BF16)
    for k, ref in enumerate((ua_ref, ub_ref, us_ref, um_ref)):
        ref[0] = _dot(h, w_ref[:, U_OFFS[k]:U_OFFS[k + 1]])


def _mod_index(b, t, nb):
    return (jnp.where(t >= LAT_TILES, nb, b), 0, 0)


def _token_specs(ctx, tile_of=lambda s: s):
    if ctx is None:
        return [pl.BlockSpec((1, ROW_TILE, D), lambda b, s: (b, tile_of(s), 0))]
    return [pl.BlockSpec((1, ROW_TILE, D), lambda b, s: (b, jnp.minimum(tile_of(s), LAT_TILES - 1), 0)),
            pl.BlockSpec((1, N_CTX, D), lambda b, s: (b, 0, 0))]


def _in_call(x, ctx, modt, nw, w_all):
    nb = x.shape[0]
    in_specs = _token_specs(ctx) + [
        pl.BlockSpec((1, N_MOD, D), lambda b, t: _mod_index(b, t, nb)),
        pl.BlockSpec((1, D), lambda b, t: (0, 0)),
        pl.BlockSpec((D, U_OFFS[-1]), lambda b, t: (0, 0)),
    ]
    widths = (UA_W, UB_W, US_W, UM_W)
    out_specs = [pl.BlockSpec((1, ROW_TILE, w), lambda b, t: (b, t, 0)) for w in widths]
    out_shape = [jax.ShapeDtypeStruct((nb, NT, w), F32) for w in widths]
    args = ((x,) if ctx is None else (x, ctx)) + (modt, nw, w_all)
    return pl.pallas_call(
        functools.partial(_in_kernel, ctx is not None),
        grid=(nb, N_TILES),
        in_specs=in_specs,
        out_specs=out_specs,
        out_shape=out_shape,
        compiler_params=_cparams(("parallel", "parallel")),
        name="in_proj",
    )(*args)


def _conv_chunk(u_ref, r0, rows, c0, c1, cw_ref, cb_ref, at_seg_start, at_seg_end):
    x = u_ref[0, pl.ds(r0, rows), c0:c1]
    rp = pl.multiple_of(jnp.maximum(r0 - SUBLANES, 0), SUBLANES)
    rn = pl.multiple_of(jnp.minimum(r0 + rows, NT - SUBLANES), SUBLANES)
    xp = u_ref[0, pl.ds(rp, SUBLANES), c0:c1] * jnp.where(at_seg_start, 0.0, 1.0)
    xn = u_ref[0, pl.ds(rn, SUBLANES), c0:c1] * jnp.where(at_seg_end, 0.0, 1.0)
    xe = jnp.concatenate([xp, x, xn], axis=0)
    tot = rows + 2 * SUBLANES
    lo, hi = SUBLANES, SUBLANES + rows
    xm2 = pltpu.roll(xe, 2, 0)[lo:hi]
    xm1 = pltpu.roll(xe, 1, 0)[lo:hi]
    xp1 = pltpu.roll(xe, tot - 1, 0)[lo:hi]
    return cw_ref[0:1, :] * xm2 + cw_ref[1:2, :] * xm1 + cw_ref[2:3, :] * x + cw_ref[3:4, :] * xp1 + cb_ref[...]


LRU_CH = 256
LRU_NCH = NT // LRU_CH
LRU_LAT_CH = N_LAT // LRU_CH


def _scan_chunk(a, b, row_in_tile, hprev, reverse):
    n = a.shape[0]
    for s in (1, 2, 4):
        if reverse:
            a_s = pltpu.roll(a, n - s, 0)
            b_s = pltpu.roll(b, n - s, 0)
            valid = row_in_tile < SUBLANES - s
        else:
            a_s = pltpu.roll(a, s, 0)
            b_s = pltpu.roll(b, s, 0)
            valid = row_in_tile >= s
        b = b + a * jnp.where(valid, b_s, 0.0)
        a = a * jnp.where(valid, a_s, 1.0)
    tiles = n // SUBLANES
    hs = [None] * tiles
    for j in (reversed(range(tiles)) if reverse else range(tiles)):
        lo = j * SUBLANES
        hj = b[lo:lo + SUBLANES] + a[lo:lo + SUBLANES] * hprev
        hprev = hj[0:1] if reverse else hj[SUBLANES - 1:SUBLANES]
        hs[j] = hj
    return jnp.concatenate(hs, axis=0), hprev


def _lru_kernel(u_ref, cw_ref, cb_ref, wri_ref, bri_ref, lam_ref, y_ref, ab_s, h_s):
    row_in_tile = lax.broadcasted_iota(jnp.int32, (LRU_CH, GW), 0) & (SUBLANES - 1)
    sp = _softplus(-lam_ref[...])

    def prepare(c, carry):
        r0 = pl.multiple_of(c * LRU_CH, LRU_CH)
        is_ctx = c == LRU_LAT_CH
        xc = _conv_chunk(u_ref, r0, LRU_CH, 0, GW, cw_ref, cb_ref, (c == 0) | is_ctx, (c == LRU_LAT_CH - 1) | is_ctx)
        g = _dot(xc.astype(BF16), wri_ref[...]) + bri_ref[...]
        for d in (0, 1):
            r = _sigmoid(g[:, 2 * d * GW:(2 * d + 1) * GW])
            ig = _sigmoid(g[:, (2 * d + 1) * GW:(2 * d + 2) * GW])
            a = jnp.exp(-LRU_C * r * sp[d:d + 1, :])
            ab_s[2 * d, pl.ds(r0, LRU_CH), :] = a
            ab_s[2 * d + 1, pl.ds(r0, LRU_CH), :] = jnp.sqrt(1.0 - a * a) * (ig * xc)
        return carry

    lax.fori_loop(0, LRU_NCH, prepare, 0)

    def body(i, carry):
        hf, hb = carry
        rf = pl.multiple_of(((i + LRU_LAT_CH) % LRU_NCH) * LRU_CH, LRU_CH)
        rb = pl.multiple_of((LRU_LAT_CH - i) * LRU_CH, LRU_CH)
        h, hf = _scan_chunk(ab_s[0, pl.ds(rf, LRU_CH), :], ab_s[1, pl.ds(rf, LRU_CH), :], row_in_tile, hf, False)
        h_s[0, pl.ds(rf, LRU_CH), :] = h
        h, hb = _scan_chunk(ab_s[2, pl.ds(rb, LRU_CH), :], ab_s[3, pl.ds(rb, LRU_CH), :], row_in_tile, hb, True)
        h_s[1, pl.ds(rb, LRU_CH), :] = h
        return hf, hb

    zero = jnp.zeros((1, GW), F32)
    lax.fori_loop(0, LRU_NCH, body, (zero, zero))

    def finish(c, carry):
        r0 = pl.multiple_of(c * LRU_CH, LRU_CH)
        gate = u_ref[0, pl.ds(r0, LRU_CH), GW:2 * GW]
        hsum = h_s[0, pl.ds(r0, LRU_CH), :] + h_s[1, pl.ds(r0, LRU_CH), :]
        y_ref[0, pl.ds(r0, LRU_CH), :] = (hsum * _gelu_tanh(gate)).astype(BF16)
        return carry

    lax.fori_loop(0, LRU_NCH, finish, 0)


def _lru_call(ua, cw, cb, wri, bri, lam):
    nb = ua.shape[0]
    full = lambda shape: pl.BlockSpec(shape, lambda b: (0,) * len(shape))
    return pl.pallas_call(
        _lru_kernel,
        grid=(nb,),
        in_specs=[
            pl.BlockSpec((1, NT, UA_W), lambda b: (b, 0, 0)),
            full((CONV_W, GW)), full((1, GW)), full((GW, 4 * GW)), full((1, 4 * GW)), full((2, GW)),
        ],
        out_specs=pl.BlockSpec((1, NT, GW), lambda b: (b, 0, 0)),
        out_shape=jax.ShapeDtypeStruct((nb, NT, GW), BF16),
        scratch_shapes=[pltpu.VMEM((4, NT, GW), F32), pltpu.VMEM((2, NT, GW), F32)],
        compiler_params=_cparams(("parallel",)),
        name="rglru",
    )(ua, cw, cb, wri, bri, lam)


HG_NCH = NT // HG_CH
HG_LAT_CH = N_LAT // HG_CH
HG_LEVELS = HG_CH.bit_length() - 1
HG_EXP_BLOCKS = 1 + HG_LEVELS
HG_SPLIT = 3


def _hgrn_tables():
    t_ = HG_CH
    dm = np.zeros((2, HG_EXP_BLOCKS, t_, t_), np.float32)
    lm = np.zeros((2, HG_LEVELS + 1, t_, t_), np.float32)
    for d in (0, 1):
        for t in range(t_):
            if d == 0:
                dm[d, 0, t, :t + 1] = 1.0
            else:
                dm[d, 0, t, t:] = 1.0
            lm[d, 0, t, t] = 1.0
        for lev in range(1, HG_LEVELS + 1):
            m = 1 << (lev - 1)
            for t in range(t_):
                start = (t // (2 * m)) * 2 * m
                mid = start + m
                upper = t >= mid
                if d == 0:
                    if upper:
                        dm[d, lev, t, mid:t + 1] = 1.0
                        lm[d, lev, t, start:mid] = 1.0
                    else:
                        dm[d, lev, t, t + 1:mid] = 1.0
                else:
                    if not upper:
                        dm[d, lev, t, t:mid] = 1.0
                        lm[d, lev, t, mid:start + 2 * m] = 1.0
                    else:
                        dm[d, lev, t, mid:t] = 1.0
    dm = dm.reshape(2, HG_EXP_BLOCKS * t_, t_)
    d3 = np.concatenate([dm] * HG_SPLIT, axis=-1)
    lm = np.tile(lm, (1, 1, 1, HG_HEADS))
    heads = np.arange(GW) // HG_HD
    bm = (heads[:, None] == heads[None, :]).astype(np.float32)
    return jnp.asarray(d3, BF16), jnp.asarray(lm, F32), jnp.asarray(bm, F32), jnp.asarray(bm, BF16)


def _hgrn_kernel(u_ref, lb_ref, nw_ref, d3_ref, lm_ref, bm_ref, bmb_ref, y_ref, o_s, st_s, stb_s):
    t = HG_CH
    lb = lb_ref[...]
    st_s[...] = jnp.zeros(st_s.shape, F32)
    stb_s[...] = jnp.zeros(stb_s.shape, BF16)

    def by_head(xb):
        return jnp.concatenate([xb] * HG_HEADS, axis=0) * bmb_ref[...]

    def chunk(d, c):
        r0 = pl.multiple_of(c * t, t)
        q = _silu(u_ref[0, pl.ds(r0, t), 0:GW])
        fr = u_ref[0, pl.ds(r0, t), (1 + d) * GW:(2 + d) * GW]
        v = u_ref[0, pl.ds(r0, t), 3 * GW:4 * GW]
        f = lb + (1.0 - lb) * _sigmoid(fr)
        k = 1.0 - f
        ex = _dot(d3_ref[d], jnp.concatenate(_split_bf16(jnp.log2(f), HG_SPLIT), axis=0))
        cum = ex[0:t]
        a = _dot_nt(q.astype(BF16), by_head(k.astype(BF16))) * lm_ref[d, 0]
        for lev in range(1, HG_LEVELS + 1):
            fac = jnp.exp2(ex[lev * t:(lev + 1) * t])
            a = a + _dot_nt((q * fac).astype(BF16), by_head((k * fac).astype(BF16))) * lm_ref[d, lev]
        vb = v.astype(BF16)
        o = _dot(a.astype(BF16), by_head(vb)) + _dot_nt((q * jnp.exp2(cum)).astype(BF16), stb_s[d])
        o_s[d, pl.ds(r0, t), :] = o
        cend = cum[t - 1:t, :] if d == 0 else cum[0:1, :]
        kend = (k * jnp.exp2(cend - cum)).astype(BF16)
        st = st_s[d] * jnp.exp2(cend) + _dot_tn(vb, kend) * bm_ref[...]
        st_s[d] = st
        stb_s[d] = st.astype(BF16)

    def body(i, carry):
        chunk(0, (i + HG_LAT_CH) % HG_NCH)
        chunk(1, HG_NCH - 1 - i)
        return carry

    lax.fori_loop(0, HG_NCH, body, 0, unroll=4)

    def finish(c, carry):
        r0 = pl.multiple_of(c * t, t)
        osum = o_s[0, pl.ds(r0, t), :] + o_s[1, pl.ds(r0, t), :]
        sq = _split_bf16(osum * osum, 2)
        ms = _dot(jnp.concatenate(sq, axis=1), jnp.concatenate([bmb_ref[...]] * 2, axis=0)) * (1.0 / HG_HD)
        g = u_ref[0, pl.ds(r0, t), 4 * GW:5 * GW]
        y_ref[0, pl.ds(r0, t), :] = (osum * lax.rsqrt(ms + EPS) * nw_ref[...] * _silu(g)).astype(BF16)
        return carry

    lax.fori_loop(0, HG_NCH, finish, 0, unroll=4)


def _hgrn_call(ub, lb, nw, tables):
    nb = ub.shape[0]
    d3, lm, bm, bmb = tables
    full = lambda shape: pl.BlockSpec(shape, lambda b: (0,) * len(shape))
    return pl.pallas_call(
        _hgrn_kernel,
        grid=(nb,),
        in_specs=[pl.BlockSpec((1, NT, UB_W), lambda b: (b, 0, 0)), full((1, GW)), full((1, GW)),
                  full(d3.shape), full(lm.shape), full(bm.shape), full(bmb.shape)],
        out_specs=pl.BlockSpec((1, NT, GW), lambda b: (b, 0, 0)),
        out_shape=jax.ShapeDtypeStruct((nb, NT, GW), BF16),
        scratch_shapes=[pltpu.VMEM((2, NT, GW), F32), pltpu.VMEM((2, GW, GW), F32), pltpu.VMEM((2, GW, GW), BF16)],
        compiler_params=_cparams(("parallel",)),
        name="hgrn2",
    )(ub, lb, nw, d3, lm, bm, bmb)


SSD_NCH = NT // SSD_CH
SSD_LAT_CH = N_LAT // SSD_CH
SSD_DT0 = GW + SSD_XBC
GRP_W = SSD_HEADS // SSD_GROUPS * SSD_HD


SSD_BC_W = 2 * SSD_GROUPS * SSD_STATE
SSD_CUM_SPLIT = 3
SSD_GAIN_SPLIT = 2


def _ssd_tables():
    t = SSD_CH
    tri = np.stack([np.tril(np.ones((t, t), np.float32)), np.triu(np.ones((t, t), np.float32))])
    col = np.zeros((2, LANES, SSD_HEADS * LANES), np.float32)
    wide = np.zeros((2, LANES, GW), np.float32)
    for d in range(2):
        for h in range(SSD_HEADS):
            col[d, d * SSD_HEADS + h, h * LANES:(h + 1) * LANES] = 1.0
            wide[d, d * SSD_HEADS + h, h * SSD_HD:(h + 1) * SSD_HD] = 1.0
    hm = (np.arange(GW)[None, :] // SSD_HD == np.arange(SSD_HEADS)[:, None]).astype(np.float32)
    gm = (np.arange(LANES)[None, :] // SSD_STATE == np.arange(SSD_GROUPS)[:, None]).astype(np.float32)
    return (jnp.asarray(np.concatenate([tri] * SSD_CUM_SPLIT, axis=2), BF16),
            jnp.asarray(np.concatenate([col] * SSD_CUM_SPLIT, axis=1), BF16),
            jnp.asarray(np.concatenate([wide] * SSD_GAIN_SPLIT, axis=1), BF16),
            jnp.asarray(hm[:, None, :], BF16), jnp.asarray(gm[:, None, :], BF16))


def _ssd_kernel(u_ref, cw_ref, cb_ref, a8_ref, b8_ref, dsk_ref, nw_ref, tri_ref, col_ref, wide_ref, hm_ref, gm_ref,
                y_ref, xs_s, bc_s, xm_s, dt_s, y_s, st_s):
    t = SSD_CH
    rr = lax.broadcasted_iota(jnp.int32, (t, t), 0)
    cc = lax.broadcasted_iota(jnp.int32, (t, t), 1)
    keeps = (rr >= cc, rr <= cc)
    a8 = -jnp.exp(a8_ref[...])

    def prepare(c, carry):
        r0 = pl.multiple_of(c * t, t)
        seg_start = (c == 0) | (c == SSD_LAT_CH)
        seg_end = (c == SSD_LAT_CH - 1) | (c == SSD_NCH - 1)
        xbc = _silu(_conv_chunk(u_ref, r0, t, GW, GW + SSD_XBC, cw_ref, cb_ref, seg_start, seg_end))
        xs = xbc[:, 0:GW]
        xs_s[pl.ds(r0, t), :] = xs
        bc_s[pl.ds(r0, t), :] = xbc[:, GW:GW + SSD_BC_W].astype(BF16)
        xb = xs.astype(BF16)
        for h in range(SSD_HEADS):
            xm_s[h, pl.ds(r0, t), :] = xb * hm_ref[h]
        dt_s[pl.ds(r0, t), :] = _softplus(u_ref[0, pl.ds(r0, t), SSD_DT0:SSD_DT0 + LANES] + b8_ref[...])
        return carry

    lax.fori_loop(0, SSD_NCH, prepare, 0)
    st_s[...] = jnp.zeros(st_s.shape, F32)

    def chunk(d, c):
        r0 = pl.multiple_of(c * t, t)
        dtn = dt_s[pl.ds(r0, t), :]
        cumn = _dot(tri_ref[d], jnp.concatenate(_split_bf16(dtn * a8, SSD_CUM_SPLIT), axis=0))
        cend = cumn[t - 1:t, :] if d == 0 else cumn[0:1, :]
        cum_t = cumn.T
        dt_t = dtn.T
        colb = _dot(jnp.concatenate(_split_bf16(cumn, SSD_CUM_SPLIT), axis=1), col_ref[d])
        gains = jnp.concatenate([jnp.exp(cumn), dtn * jnp.exp(cend - cumn)], axis=0)
        gw = _dot(jnp.concatenate(_split_bf16(gains, SSD_GAIN_SPLIT), axis=1), wide_ref[d])
        ecum, wend = gw[0:t], gw[t:2 * t]
        bc = bc_s[pl.ds(r0, t), :]
        bmat, cmat = bc[:, 0:LANES], bc[:, LANES:2 * LANES]
        st = st_s[d]
        y = jnp.zeros((t, GW), F32)
        y_in = []
        for g in range(SSD_GROUPS):
            cg = cmat * gm_ref[g]
            scores = _dot_nt(cg, bmat)
            sg = st[:, g * GRP_W:(g + 1) * GRP_W].astype(BF16)
            y_in.append(_dot(cg, jnp.concatenate([sg] * SSD_GROUPS, axis=0)))
            for hh in range(SSD_HEADS // SSD_GROUPS):
                h = g * (SSD_HEADS // SSD_GROUPS) + hh
                row = d * SSD_HEADS + h
                seg = colb[:, h * LANES:(h + 1) * LANES] - cum_t[row:row + 1, :]
                m = jnp.where(keeps[d], jnp.exp(seg), 0.0) * scores * dt_t[row:row + 1, :]
                y = y + _dot(m.astype(BF16), xm_s[h, pl.ds(r0, t), :])
        y_s[d, pl.ds(r0, t), :] = y + jnp.concatenate(y_in, axis=1) * ecum
        xw = (xs_s[pl.ds(r0, t), :] * wend).astype(BF16)
        upd = [_dot_tn(bmat, xw[:, g * GRP_W:(g + 1) * GRP_W])[g * SSD_STATE:(g + 1) * SSD_STATE]
               for g in range(SSD_GROUPS)]
        eend = ecum[t - 1:t, :] if d == 0 else ecum[0:1, :]
        st_s[d] = st * eend + jnp.concatenate(upd, axis=1)

    def body(i, carry):
        chunk(0, (i + SSD_LAT_CH) % SSD_NCH)
        chunk(1, SSD_NCH - 1 - i)
        return carry

    lax.fori_loop(0, SSD_NCH, body, 0, unroll=3)

    def finish(c, carry):
        r0 = pl.multiple_of(c * t, t)
        z = u_ref[0, pl.ds(r0, t), 0:GW]
        yy = (y_s[0, pl.ds(r0, t), :] + y_s[1, pl.ds(r0, t), :] + dsk_ref[...] * xs_s[pl.ds(r0, t), :]) * _silu(z)
        ms = jnp.mean(yy * yy, axis=-1, keepdims=True)
        y_ref[0, pl.ds(r0, t), :] = (yy * lax.rsqrt(ms + EPS) * nw_ref[...]).astype(BF16)
        return carry

    lax.fori_loop(0, SSD_NCH, finish, 0, unroll=2)


def _ssd_call(us, cw, cb, a8, b8, dsk, nw, tables):
    nb = us.shape[0]
    full = lambda shape: pl.BlockSpec(shape, lambda b: (0,) * len(shape))
    return pl.pallas_call(
        _ssd_kernel,
        grid=(nb,),
        in_specs=[
            pl.BlockSpec((1, NT, US_W), lambda b: (b, 0, 0)),
            full((CONV_W, SSD_XBC)), full((1, SSD_XBC)), full((1, LANES)), full((1, LANES)),
            full((1, GW)), full((1, GW)),
        ] + [full(tb.shape) for tb in tables],
        out_specs=pl.BlockSpec((1, NT, GW), lambda b: (b, 0, 0)),
        out_shape=jax.ShapeDtypeStruct((nb, NT, GW), BF16),
        scratch_shapes=[pltpu.VMEM((NT, GW), F32), pltpu.VMEM((NT, SSD_BC_W), BF16),
                        pltpu.VMEM((SSD_HEADS, NT, GW), BF16), pltpu.VMEM((NT, LANES), F32),
                        pltpu.VMEM((2, NT, GW), F32), pltpu.VMEM((2, SSD_STATE, GW), F32)],
        compiler_params=_cparams(("parallel",)),
        name="ssd",
    )(us, cw, cb, a8, b8, dsk, nw, *tables)


HB = LANES
MLA_CQ_W = 2 * LANES
MLA_KR0 = MLA_CQ_W + MLA_KVR
assert MLA_QR <= MLA_CQ_W and MLA_KR0 + HB == UM_W
QW = MLA_HEADS * HB
MLA_QB = 256
MLA_AB = 256
MLA_SCALE = (MLA_NOPE + MLA_ROPE) ** -0.5


def _mla_kernel(need_ctx, u_ref, qa_ref, wq_ref, qn_ref, kva_ref, wkv_ref, kn_ref, krw_ref,
                cos_ref, sa_ref, sb_ref, y_ref, q_s, k_s, v_s):
    lane = lax.broadcasted_iota(jnp.int32, (QW, QW), 1)
    row = lax.broadcasted_iota(jnp.int32, (QW, QW), 0)

    def grp(i):
        within = i & (HB - 1)
        return (i >> HB_SHIFT) * 3 + jnp.where(within < MLA_NOPE, 0, jnp.where(within < MLA_NOPE + MLA_ROPE, 1, 2))

    gmat = jnp.where(grp(row) == grp(lane), 1.0, 0.0).astype(BF16)
    l1 = lax.broadcasted_iota(jnp.int32, (1, QW), 1) & (HB - 1)
    inv_size = jnp.where(l1 < MLA_NOPE, 1.0 / MLA_NOPE, 1.0 / MLA_ROPE)
    vlane_head = lax.broadcasted_iota(jnp.int32, (MLA_QB, GW), 1) >> HEAD_SHIFT

    def rope(x, r0, reps):
        cosr = cos_ref[pl.ds(r0, MLA_QB), :]
        sar = sa_ref[pl.ds(r0, MLA_QB), :]
        sbr = sb_ref[pl.ds(r0, MLA_QB), :]
        if reps > 1:
            cosr, sar, sbr = (jnp.concatenate([tbl] * reps, axis=1) for tbl in (cosr, sar, sbr))
        w = x.shape[1]
        return x * cosr + pltpu.roll(x, w - MLA_ROPE // 2, 1) * sar + pltpu.roll(x, MLA_ROPE // 2, 1) * sbr

    def project(ci, carry):
        r0 = pl.multiple_of(ci * MLA_QB, MLA_QB)
        cq = u_ref[0, pl.ds(r0, MLA_QB), 0:MLA_CQ_W]
        ms = jnp.sum(cq * cq, axis=-1, keepdims=True) * (1.0 / MLA_QR)
        qraw = _dot((cq * lax.rsqrt(ms + EPS) * qa_ref[...]).astype(BF16), wq_ref[...])
        ss = _dot((qraw * qraw).astype(BF16), gmat) * inv_size
        q = rope(qraw * lax.rsqrt(ss + EPS) * qn_ref[...], r0, MLA_HEADS) * MLA_SCALE
        q_s[pl.ds(r0, MLA_QB), :] = q.astype(BF16)
        ckv = u_ref[0, pl.ds(r0, MLA_QB), MLA_CQ_W:MLA_KR0]
        ms = jnp.mean(ckv * ckv, axis=-1, keepdims=True)
        kv = _dot((ckv * lax.rsqrt(ms + EPS) * kva_ref[...]).astype(BF16), wkv_ref[...])
        kraw = kv[:, 0:QW]
        ss = _dot((kraw * kraw).astype(BF16), gmat) * inv_size
        knope = kraw * lax.rsqrt(ss + EPS) * kn_ref[...]
        kr = u_ref[0, pl.ds(r0, MLA_QB), MLA_KR0:MLA_KR0 + HB]
        ms = jnp.sum(kr * kr, axis=-1, keepdims=True) * (1.0 / MLA_ROPE)
        krope = rope(kr * lax.rsqrt(ms + EPS) * krw_ref[...], r0, 1)
        k_s[pl.ds(r0, MLA_QB), :] = (knope + jnp.concatenate([krope] * MLA_HEADS, axis=1)).astype(BF16)
        vv = kv[:, QW:QW + GW]
        for h in range(MLA_HEADS):
            v_s[h, pl.ds(r0, MLA_QB), :] = jnp.where(vlane_head == h, vv, 0.0).astype(BF16)
        return carry

    lax.fori_loop(0, NT // MLA_QB, project, 0, unroll=3)

    def attend(r0, rows, k0, klen):
        o = jnp.zeros((rows, GW), F32)
        for h in range(MLA_HEADS):
            qh = q_s[pl.ds(r0, rows), h * HB:(h + 1) * HB]
            s = _dot_nt(qh, k_s[k0:k0 + klen, h * HB:(h + 1) * HB])
            p = jnp.exp(s - jnp.max(s, axis=-1, keepdims=True))
            l = jnp.sum(p, axis=-1, keepdims=True)
            o = o + _dot(p.astype(BF16), v_s[h, k0:k0 + klen, :]) * (1.0 / l)
        y_ref[0, pl.ds(r0, rows), :] = o.astype(BF16)

    def lat_block(qi, carry):
        attend(pl.multiple_of(qi * MLA_AB, MLA_AB), MLA_AB, 0, NT)
        return carry

    lax.fori_loop(0, N_LAT // MLA_AB, lat_block, 0, unroll=4)
    if need_ctx:
        attend(N_LAT, N_CTX, N_LAT, N_CTX)
    else:
        y_ref[0, N_LAT:NT, :] = jnp.zeros((N_CTX, GW), BF16)


def _mla_call(um, need_ctx, qa, wq, qn, kva, wkv, kn, krw, cos_t, sa_t, sb_t):
    nb = um.shape[0]
    full = lambda shape: pl.BlockSpec(shape, lambda b: (0,) * len(shape))
    return pl.pallas_call(
        functools.partial(_mla_kernel, need_ctx),
        grid=(nb,),
        in_specs=[
            pl.BlockSpec((1, NT, UM_W), lambda b: (b, 0, 0)),
            full((1, MLA_CQ_W)), full((MLA_CQ_W, QW)), full((1, QW)),
            full((1, MLA_KVR)), full((MLA_KVR, QW + GW)), full((1, QW)), full((1, HB)),
            full((NT, HB)), full((NT, HB)), full((NT, HB)),
        ],
        out_specs=pl.BlockSpec((1, NT, GW), lambda b: (b, 0, 0)),
        out_shape=jax.ShapeDtypeStruct((nb, NT, GW), BF16),
        scratch_shapes=[pltpu.VMEM((NT, QW), BF16), pltpu.VMEM((NT, QW), BF16),
                        pltpu.VMEM((MLA_HEADS, NT, GW), BF16)],
        compiler_params=_cparams(("parallel",)),
        name="mla",
    )(um, qa, wq, qn, kva, wkv, kn, krw, cos_t, sa_t, sb_t)


def _out_tile(s):
    return jnp.maximum(s - 1, 0)


def _out_kernel(split, *refs):
    if split:
        x_ref, c_ref, ya_ref, yb_ref, ys_ref, ym_ref, wo_ref, mod_ref, nw_ref, wr_ref = refs[:10]
    else:
        x_ref, ya_ref, yb_ref, ys_ref, ym_ref, wo_ref, mod_ref, nw_ref, wr_ref = refs[:9]
    xo_ref, h2_ref, aff_ref, acc_s = refs[-4:]
    s = pl.program_id(1)

    @pl.when(s == 0)
    def _first():
        acc_s[1] = jnp.zeros(acc_s.shape[1:], F32)

    def step(slot):
        y = jnp.concatenate([ya_ref[0], yb_ref[0], ys_ref[0], ym_ref[0]], axis=1)
        acc_s[slot] = _dot(y, wo_ref[...])
        x_in = jnp.where(_out_tile(s) < LAT_TILES, x_ref[0], c_ref[0]) if split else x_ref[0]
        x = x_in + mod_ref[0, 2:3, :] * acc_s[1 - slot]
        xo_ref[0] = x
        h2 = _norm_mod(x, nw_ref[...], mod_ref[0, 3:4, :], mod_ref[0, 4:5, :])
        hi = h2.astype(BF16)
        h2_ref[0] = hi
        lo = (h2 - hi.astype(F32)).astype(BF16)
        wr = wr_ref[...]
        w_hi = wr.astype(BF16)
        w_lo = (wr - w_hi.astype(F32)).astype(BF16)
        logit = _dot_nt(jnp.concatenate([w_hi, w_hi, w_lo], axis=1), jnp.concatenate([hi, lo, hi], axis=1))
        e = jnp.exp(logit - jnp.max(logit, axis=0, keepdims=True))
        aff_ref[0] = e / jnp.sum(e, axis=0, keepdims=True)

    for parity in (0, 1):
        pl.when(s % 2 == parity)(functools.partial(step, parity))


def _out_call(x, ctx, ys, wo, modt, nw, wr_t, n_tiles):
    nb = x.shape[0]
    proj = lambda w: pl.BlockSpec((1, ROW_TILE, w), lambda b, s: (b, jnp.minimum(s, n_tiles - 1), 0))
    done = lambda w: pl.BlockSpec((1, ROW_TILE, w), lambda b, s: (b, _out_tile(s), 0))
    x_args = (x,) if ctx is None else (x, ctx)
    return pl.pallas_call(
        functools.partial(_out_kernel, ctx is not None),
        grid=(nb, n_tiles + 1),
        in_specs=_token_specs(ctx, _out_tile) + [proj(GW), proj(GW), proj(GW), proj(GW),
                  pl.BlockSpec((D, D), lambda b, s: (0, 0)),
                  pl.BlockSpec((1, N_MOD, D), lambda b, s: _mod_index(b, _out_tile(s), nb)),
                  pl.BlockSpec((1, D), lambda b, s: (0, 0)),
                  pl.BlockSpec((N_EXP, D), lambda b, s: (0, 0))],
        out_specs=[done(D), done(D), pl.BlockSpec((1, N_EXP, ROW_TILE), lambda b, s: (b, 0, _out_tile(s)))],
        out_shape=[jax.ShapeDtypeStruct((nb, n_tiles * ROW_TILE, D), F32),
                   jax.ShapeDtypeStruct((nb, n_tiles * ROW_TILE, D), BF16),
                   jax.ShapeDtypeStruct((nb, N_EXP, n_tiles * ROW_TILE), F32)],
        scratch_shapes=[pltpu.VMEM((2, ROW_TILE, D), F32)],
        compiler_params=_cparams(("parallel", "arbitrary")),
        name="out_proj",
    )(*x_args, *ys, wo, modt, nw, wr_t)


PRE_W = 256
ROUTE_MAX_IT = 160


def _prefix_count(m, tri_bf):
    n = m.shape[1]
    off = jnp.zeros((m.shape[0], 1), F32)
    outs = []
    for j in range(n // PRE_W):
        blk = m[:, j * PRE_W:(j + 1) * PRE_W]
        outs.append(_dot(blk.astype(BF16), tri_bf) + off)
        off = off + jnp.sum(blk, axis=1, keepdims=True)
    return outs[0] if len(outs) == 1 else jnp.concatenate(outs, axis=1)


def _route(aff, cap, tri):
    n_lo0 = jnp.full((N_EXP, 1), float(aff.shape[1]), F32)

    def cond(s):
        it, _, _, n_lo, n_hi = s
        return jnp.logical_and(it < ROUTE_MAX_IT, jnp.max(n_lo - n_hi) > 1.0)

    def step(s):
        it, lo, hi, n_lo, n_hi = s
        width = hi - lo
        mids = [lo + frac * width for frac in (0.25, 0.5, 0.75)]
        cnts = [jnp.sum(jnp.where(aff > m, 1.0, 0.0), axis=1, keepdims=True) for m in mids]
        blws = [jnp.max(jnp.where(aff <= m, aff, -1.0), axis=1, keepdims=True) for m in mids]
        u0, u1, u2 = (c >= cap for c in cnts)
        pick = lambda a3, a2, a1, a0: jnp.where(u2, a3, jnp.where(u1, a2, jnp.where(u0, a1, a0)))
        return (it + 1, pick(mids[2], mids[1], mids[0], lo), pick(hi, blws[2], blws[1], blws[0]),
                pick(cnts[2], cnts[1], cnts[0], n_lo), pick(n_hi, cnts[2], cnts[1], cnts[0]))

    init = (jnp.int32(0), jnp.full((N_EXP, 1), -1.0, F32), jnp.max(aff, axis=1, keepdims=True),
            n_lo0, jnp.zeros((N_EXP, 1), F32))
    _, _, thr, _, n_gt = lax.while_loop(cond, step, init)
    gt = jnp.where(aff > thr, 1.0, 0.0)
    eq = jnp.where(aff == thr, 1.0, 0.0)
    sel = gt + eq * jnp.where(_prefix_count(eq, tri) < cap - n_gt, 1.0, 0.0)
    return jnp.where(sel > 0.0, _prefix_count(sel, tri), -1.0)


MOE_EPS = 2


def _cast_kernel(g_ref, u_ref, d_ref, go_ref, uo_ref, do_ref):
    go_ref[...] = g_ref[0].astype(BF16)
    uo_ref[...] = u_ref[0].astype(BF16)
    do_ref[...] = d_ref[0].astype(BF16)


def _expert_weights_bf16(wg, wu, wd, layer):
    spec = lambda a, b: (pl.BlockSpec((1, MOE_EPS, a, b), lambda e: (layer, e, 0, 0)),
                         pl.BlockSpec((MOE_EPS, a, b), lambda e: (e, 0, 0)))
    (up_in, up_out), (down_in, down_out) = spec(D, FF), spec(FF, D)
    return pl.pallas_call(
        _cast_kernel,
        grid=(N_EXP // MOE_EPS,),
        in_specs=[up_in, up_in, down_in],
        out_specs=[up_out, up_out, down_out],
        out_shape=[jax.ShapeDtypeStruct(w.shape[1:], BF16) for w in (wg, wu, wd)],
        compiler_params=_cparams(("parallel",)),
        name="expert_weights_bf16",
    )(wg, wu, wd)


def _moe_kernel(segs, x_hbm, aff_ref, h_ref, gl_ref, gc_ref, wg_ref, wu_ref, wd_ref, o_ref, pos_ref, sem):
    b = pl.program_id(0)
    step = pl.program_id(1)
    rows = o_ref.shape[1]

    def residual_copy():
        return pltpu.make_async_copy(x_hbm.at[b, pl.ds(0, rows), :], o_ref.at[0], sem)

    @pl.when(step == 0)
    def _init():
        residual_copy().start()
        tri = jnp.where(lax.broadcasted_iota(jnp.int32, (PRE_W, PRE_W), 0)
                        < lax.broadcasted_iota(jnp.int32, (PRE_W, PRE_W), 1), 1.0, 0.0).astype(BF16)
        for r0, n, cap in segs:
            pos_ref[:, r0:r0 + n] = _route(aff_ref[0, :, r0:r0 + n], cap, tri)
        residual_copy().wait()

    onehots, gates, xs = [], [], []
    for (r0, n, cap), g_ref in zip(segs, (gl_ref, gc_ref)):
        slot = lax.broadcasted_iota(jnp.int32, (cap, n), 0).astype(F32)
        oh, gt = [], []
        for j in range(MOE_EPS):
            e = step * MOE_EPS + j
            hit = slot == pos_ref[pl.ds(e, 1), r0:r0 + n]
            oh.append(jnp.where(hit, 1.0, 0.0).astype(BF16))
            gsel = jnp.sum(jnp.where(hit, aff_ref[0, pl.ds(e, 1), r0:r0 + n], 0.0), axis=1, keepdims=True)
            gt.append(gsel * g_ref[0, N_MOD - 1:N_MOD, :])
        onehot = jnp.concatenate(oh, axis=0)
        onehots.append(onehot)
        gates.append(gt)
        xs.append(_dot(onehot, h_ref[0, r0:r0 + n, :]).astype(BF16))
    ys = []
    for j in range(MOE_EPS):
        xj = [x[j * cap:(j + 1) * cap] for x, (_, _, cap) in zip(xs, segs)]
        xj = xj[0] if len(xj) == 1 else jnp.concatenate(xj, axis=0)
        act = (_silu(_dot(xj, wg_ref[j])) * _dot(xj, wu_ref[j])).astype(BF16)
        ys.append(_dot(act, wd_ref[j]))
    s0 = 0
    for (r0, n, cap), onehot, gt in zip(segs, onehots, gates):
        ysg = jnp.concatenate([(ys[j][s0:s0 + cap] * gt[j]).astype(BF16) for j in range(MOE_EPS)], axis=0)
        o_ref[0, r0:r0 + n, :] += _dot_tn(onehot, ysg)
        s0 += cap


def _moe_call(x_mid, aff, h2, modt, wg, wu, wd, with_ctx):
    nb = h2.shape[0]
    segs = ((0, N_LAT, EC_FACTOR * N_LAT // N_EXP),)
    if with_ctx:
        segs += ((N_LAT, N_CTX, EC_FACTOR * N_CTX // N_EXP),)
    rows = NT if with_ctx else N_LAT
    return pl.pallas_call(
        functools.partial(_moe_kernel, segs),
        grid=(nb, N_EXP // MOE_EPS),
        in_specs=[
            pl.BlockSpec(memory_space=pl.ANY),
            pl.BlockSpec((1, N_EXP, rows), lambda b, e: (b, 0, 0)),
            pl.BlockSpec((1, rows, D), lambda b, e: (b, 0, 0)),
            pl.BlockSpec((1, N_MOD, D), lambda b, e: (b, 0, 0)),
            pl.BlockSpec((1, N_MOD, D), lambda b, e: (nb, 0, 0)),
            pl.BlockSpec((MOE_EPS, D, FF), lambda b, e: (e, 0, 0)),
            pl.BlockSpec((MOE_EPS, D, FF), lambda b, e: (e, 0, 0)),
            pl.BlockSpec((MOE_EPS, FF, D), lambda b, e: (e, 0, 0)),
        ],
        out_specs=pl.BlockSpec((1, rows, D), lambda b, e: (b, 0, 0)),
        out_shape=jax.ShapeDtypeStruct((nb, rows, D), F32),
        scratch_shapes=[pltpu.VMEM((N_EXP, rows), F32), pltpu.SemaphoreType.DMA(())],
        compiler_params=_cparams(("parallel", "arbitrary")),
        name="moe",
    )(x_mid, aff, h2, modt, modt, wg, wu, wd)


def _pad_cols(w, width):
    return jnp.pad(w, ((0, 0), (0, width - w.shape[1])))


def _rope_perm():
    half = MLA_ROPE // 2
    return jnp.concatenate([jnp.arange(half) * 2, jnp.arange(half) * 2 + 1])


def _head_block(nope, rope):
    pad = jnp.zeros(nope.shape[:-1] + (HB - MLA_NOPE - MLA_ROPE,), nope.dtype)
    blk = jnp.concatenate([nope, rope, pad], axis=-1)
    return blk.reshape(blk.shape[:-2] + (MLA_HEADS * HB,))


def _prep_in_weights(w_in):
    o1, o2, o3 = LRU_COLS, LRU_COLS + HG_COLS, LRU_COLS + HG_COLS + SSD_COLS
    wa, wb, ws, wm = w_in[:, :o1], w_in[:, o1:o2], w_in[:, o2:o3], w_in[:, o3:]
    perm = _rope_perm()
    cq, ckv, kr = wm[:, :MLA_QR], wm[:, MLA_QR:MLA_QR + MLA_KVR], wm[:, MLA_QR + MLA_KVR:]
    zeros = lambda n: jnp.zeros((D, n), w_in.dtype)
    wm_p = jnp.concatenate([cq, zeros(MLA_CQ_W - MLA_QR), ckv, zeros(MLA_NOPE), kr[:, perm],
                            zeros(HB - MLA_NOPE - MLA_ROPE)], axis=1)
    return jnp.concatenate([wa, wb, _pad_cols(ws, US_W), wm_p], axis=1).astype(BF16)


def _block_diag(w):
    h, dd, _ = w.shape
    eye = jnp.eye(h, dtype=w.dtype)
    return (eye[:, None, :, None] * w[:, :, None, :]).reshape(h * dd, h * dd)


def _rope_tables():
    rows = N_LAT // GRID_W
    row = jnp.repeat(jnp.arange(rows, dtype=F32), GRID_W)
    col = jnp.tile(jnp.arange(GRID_W, dtype=F32), rows)
    half = MLA_ROPE // 2
    inv = ROPE_BASE ** (-jnp.arange(0, half, 2, dtype=F32) / half)
    ang = jnp.concatenate([row[:, None] * inv, col[:, None] * inv], axis=-1)
    cos, sin = jnp.cos(ang), jnp.sin(ang)
    z = lambda n: jnp.zeros((N_LAT, n), F32)
    o = lambda n: jnp.ones((N_LAT, n), F32)
    cos_t = jnp.concatenate([o(MLA_NOPE), cos, cos, o(HB - MLA_NOPE - MLA_ROPE)], axis=1)
    sa_t = jnp.concatenate([z(MLA_NOPE), -sin, z(half), z(HB - MLA_NOPE - MLA_ROPE)], axis=1)
    sb_t = jnp.concatenate([z(MLA_NOPE), z(half), sin, z(HB - MLA_NOPE - MLA_ROPE)], axis=1)
    cos_t = jnp.concatenate([cos_t, jnp.ones((N_CTX, HB), F32)], axis=0)
    sa_t = jnp.concatenate([sa_t, jnp.zeros((N_CTX, HB), F32)], axis=0)
    sb_t = jnp.concatenate([sb_t, jnp.zeros((N_CTX, HB), F32)], axis=0)
    return cos_t, sa_t, sb_t


def kernel(x, c, ctx, c_ctx, ada_w, ada_b, norm1_w, norm2_w, w_in, w_out, lru_conv_w, lru_conv_b, lru_w_r, lru_b_r, lru_w_i, lru_b_i, lru_lam, hgrn_lb_logits, hgrn_norm_w, ssd_conv_w, ssd_conv_b, ssd_a_log, ssd_dt_bias, ssd_d_skip, ssd_norm_w, mla_q_a_norm, mla_w_q_up, mla_kv_a_norm, mla_w_kv_up, mla_q_norm, mla_k_norm, moe_router, moe_w_gate, moe_w_up, moe_w_down):
    nb = x.shape[0]
    assert x.shape == (nb, N_LAT, D) and ctx.shape == (nb, N_CTX, D)
    assert nb + 1 <= ADA_ROWS
    cc = jnp.concatenate([c, c_ctx[None, :], jnp.zeros((ADA_ROWS - nb - 1, D), F32)], axis=0)
    mod_all = _ada_call(cc, ada_w, ada_b)[:, :nb + 1].reshape(DEPTH, nb + 1, N_MOD, D)
    cos_t, sa_t, sb_t = _rope_tables()
    hg_tables = _hgrn_tables()
    ssd_tables = _ssd_tables()
    perm = _rope_perm()
    lb_w = jax.nn.softmax(hgrn_lb_logits.astype(F32), axis=0)
    lb_all = jnp.cumsum(lb_w, axis=0) - lb_w[0]
    rep = lambda v, n: jnp.repeat(v, n, axis=-1)
    stream = (x, ctx)
    for l in range(DEPTH):
        need_ctx = l < DEPTH - 1
        modt = mod_all[l]
        ua, ub, us, um = _in_call(*stream, modt, norm1_w[l][None, :], _prep_in_weights(w_in[l]))
        wri = jnp.concatenate([_block_diag(lru_w_r[l, 0]), _block_diag(lru_w_i[l, 0]),
                               _block_diag(lru_w_r[l, 1]), _block_diag(lru_w_i[l, 1])], axis=1).astype(BF16)
        bri = jnp.concatenate([lru_b_r[l, 0], lru_b_i[l, 0], lru_b_r[l, 1], lru_b_i[l, 1]])[None, :]
        ya = _lru_call(ua, lru_conv_w[l], lru_conv_b[l][None, :], wri, bri, lru_lam[l])
        yb = _hgrn_call(ub, lb_all[l][None, :], hgrn_norm_w[l][None, :], hg_tables)
        narrow = lambda v: jnp.pad(v.reshape(-1), (0, LANES - 2 * SSD_HEADS))[None, :]
        ysd = _ssd_call(us, ssd_conv_w[l], ssd_conv_b[l][None, :], narrow(ssd_a_log[l]), narrow(ssd_dt_bias[l]),
                        rep(ssd_d_skip[l], SSD_HD)[None, :], ssd_norm_w[l][None, :], ssd_tables)
        wq = mla_w_q_up[l].reshape(MLA_QR, MLA_HEADS, MLA_NOPE + MLA_ROPE)
        wq = _head_block(wq[..., :MLA_NOPE], wq[..., MLA_NOPE:][..., perm])
        wq = jnp.pad(wq, ((0, MLA_CQ_W - MLA_QR), (0, 0))).astype(BF16)
        qn = _head_block(jnp.broadcast_to(mla_q_norm[l][:MLA_NOPE], (MLA_HEADS, MLA_NOPE)),
                         jnp.broadcast_to(mla_q_norm[l][MLA_NOPE:][perm], (MLA_HEADS, MLA_ROPE)))[None, :]
        wkv = mla_w_kv_up[l].reshape(MLA_KVR, MLA_HEADS, MLA_NOPE + MLA_V)
        wk = _head_block(wkv[..., :MLA_NOPE], jnp.zeros((MLA_KVR, MLA_HEADS, MLA_ROPE), F32))
        wv = wkv[..., MLA_NOPE:].reshape(MLA_KVR, GW)
        wkv_p = jnp.concatenate([wk, wv], axis=1).astype(BF16)
        kn = _head_block(jnp.broadcast_to(mla_k_norm[l][:MLA_NOPE], (MLA_HEADS, MLA_NOPE)),
                         jnp.zeros((MLA_HEADS, MLA_ROPE), F32))[None, :]
        krw = jnp.concatenate([jnp.zeros((MLA_NOPE,), F32), mla_k_norm[l][MLA_NOPE:][perm],
                               jnp.zeros((HB - MLA_NOPE - MLA_ROPE,), F32)])[None, :]
        qa = jnp.pad(mla_q_a_norm[l], (0, MLA_CQ_W - MLA_QR))[None, :]
        ym = _mla_call(um, need_ctx, qa, wq, qn, mla_kv_a_norm[l][None, :], wkv_p, kn, krw, cos_t, sa_t, sb_t)
        n_tiles = N_TILES if need_ctx else LAT_TILES
        x_mid, h2, aff = _out_call(*stream, (ya, yb, ysd, ym), w_out[l].astype(BF16), modt, norm2_w[l][None, :],
                                   moe_router[l].T, n_tiles)
        wg, wu, wd = _expert_weights_bf16(moe_w_gate, moe_w_up, moe_w_down, l)
        stream = (_moe_call(x_mid, aff, h2, modt, wg, wu, wd, need_ctx), None)
    return stream[0]
```
